```python
import math
import jax
import jax.numpy as jnp
from jax import lax
import numpy as np

D_MODEL = 1024
BATCH = 16
SEQ = 4096
DEPTH = 2
DEC_BATCH = 8
DEC_SEQ = 2048
PAST_LEN = 128

ROPE_THETA = 10000.0
NORM_EPS = 1e-6
HEAD_NORM_EPS = 1e-5
CHUNK = 128
Q_BLOCK = 128
N_BRANCH = 4
BRANCH_WIDTH = 512

RET_HEADS = 4
RET_QK_DIM = 64
RET_V_DIM = 128
MLA_HEADS = 8
MLA_NOPE_DIM = 64
MLA_ROPE_DIM = 32
MLA_V_DIM = 64
MLA_Q_RANK = 256
MLA_KV_RANK = 128
DIFF_HEADS = 4
DIFF_HEAD_DIM = 64
DIFF_V_DIM = 2 * DIFF_HEAD_DIM
SSM_HEADS = 8
SSM_HEAD_DIM = 64
SSM_GROUPS = 2
SSM_STATE = 128
SSM_CONV = 3
SSM_INNER = SSM_HEADS * SSM_HEAD_DIM
SSM_CONV_CH = SSM_INNER + 2 * SSM_GROUPS * SSM_STATE
D_FF = 2816
FFN_CONV = 3

IN_SPLIT_SIZES = (
    RET_HEADS * RET_QK_DIM,
    RET_HEADS * RET_QK_DIM,
    RET_HEADS * RET_V_DIM,
    RET_HEADS * RET_V_DIM,
    MLA_Q_RANK,
    MLA_KV_RANK,
    MLA_ROPE_DIM,
    DIFF_HEADS * 2 * DIFF_HEAD_DIM,
    DIFF_HEADS * 2 * DIFF_HEAD_DIM,
    DIFF_HEADS * DIFF_V_DIM,
    SSM_INNER,
    SSM_CONV_CH,
    2 * SSM_HEADS,
    N_BRANCH * D_MODEL,
)
IN_COLS = sum(IN_SPLIT_SIZES)

kernel_name = "hybrid_bidir_encoder_gated_merge"


def rms_norm(x, w, eps=NORM_EPS):
    xf = x.astype(jnp.float32)
    y = xf * lax.rsqrt(jnp.mean(xf * xf, axis=-1, keepdims=True) + eps)
    return (y * w.astype(jnp.float32)).astype(x.dtype)


def split_cols(t, sizes):
    return jnp.split(t, np.cumsum(sizes)[:-1].tolist(), axis=-1)


def seq_flip(t):
    return jnp.flip(t, axis=1)


def rope_tables(seq, dim):
    inv_freq = 1.0 / (ROPE_THETA ** (jnp.arange(0, dim, 2, dtype=jnp.float32) / dim))
    ang = jnp.arange(seq, dtype=jnp.float32)[:, None] * inv_freq[None, :]
    return jnp.cos(ang), jnp.sin(ang)


def apply_rope(x, cos, sin):
    shape = (1, x.shape[1]) + (1,) * (x.ndim - 3) + (cos.shape[-1],)
    c = cos.reshape(shape).astype(x.dtype)
    sn = sin.reshape(shape).astype(x.dtype)
    x1, x2 = jnp.split(x, 2, axis=-1)
    return jnp.concatenate([x1 * c - x2 * sn, x2 * c + x1 * sn], axis=-1)


def depthwise_conv_centred(x, w, bias):
    k, c = w.shape
    pad = k // 2
    y = lax.conv_general_dilated(x, w.astype(x.dtype)[:, None, :], window_strides=(1,),
                                 padding=[(pad, pad)], dimension_numbers=("NWC", "WIO", "NWC"),
                                 feature_group_count=c)
    return y + bias.astype(x.dtype)


def split_query_blocks(t):
    b, s, h, d = t.shape
    return t.reshape(b, s // Q_BLOCK, Q_BLOCK, h, d).transpose(1, 0, 2, 3, 4)


def merge_query_blocks(t):
    n, b, qb, h, d = t.shape
    return t.transpose(1, 0, 2, 3, 4).reshape(b, n * qb, h, d)


def softmax_attention(q, k, v, scale):
    def one_block(qb):
        sc = jnp.einsum("bqhd,bkhd->bhqk", qb, k, preferred_element_type=jnp.float32) * scale
        pr = jax.nn.softmax(sc, axis=-1)
        return jnp.einsum("bhqk,bkhe->bqhe", pr.astype(v.dtype), v)
    return merge_query_blocks(lax.map(one_block, split_query_blocks(q)))


def diff_attention(q1, q2, k1, k2, v, lam, scale):
    def one_block(qs):
        qa, qb = qs
        s1 = jnp.einsum("bqhd,bkhd->bhqk", qa, k1, preferred_element_type=jnp.float32) * scale
        s2 = jnp.einsum("bqhd,bkhd->bhqk", qb, k2, preferred_element_type=jnp.float32) * scale
        amap = jax.nn.softmax(s1, axis=-1) - lam * jax.nn.softmax(s2, axis=-1)
        return jnp.einsum("bhqk,bkhe->bqhe", amap.astype(v.dtype), v)
    return merge_query_blocks(lax.map(one_block, (split_query_blocks(q1), split_query_blocks(q2))))


def retention_causal(q, k, v, log_gamma):
    b, s, h, dk = q.shape
    dv = v.shape[-1]
    n = s // CHUNK
    f32 = jnp.float32
    lg = log_gamma.astype(f32)
    pos = jnp.arange(CHUNK, dtype=f32)
    rel = pos[:, None] - pos[None, :]
    intra = jnp.where(rel >= 0, jnp.exp(jnp.maximum(rel, 0.0)[None] * lg[:, None, None]), 0.0)
    q_dec = jnp.exp((pos + 1.0)[:, None] * lg)
    k_dec = jnp.exp((CHUNK - 1.0 - pos)[:, None] * lg)
    chunk_dec = jnp.exp(CHUNK * lg)
    qc = q.astype(f32).reshape(b, n, CHUNK, h, dk)
    kc = k.astype(f32).reshape(b, n, CHUNK, h, dk)
    vc = v.astype(f32).reshape(b, n, CHUNK, h, dv)
    scores = jnp.einsum("bnihd,bnjhd->bnhij", qc, kc) * intra
    y_intra = jnp.einsum("bnhij,bnjhe->bnihe", scores, vc)
    kv = jnp.einsum("bnjhd,bnjhe->nbhde", kc * k_dec[:, :, None], vc)

    def step(state, kv_c):
        return chunk_dec[:, None, None] * state + kv_c, state

    _, prev = lax.scan(step, jnp.zeros((b, h, dk, dv), f32), kv)
    y_cross = jnp.einsum("bnihd,nbhde->bnihe", qc * q_dec[:, :, None], prev)
    return (y_intra + y_cross).reshape(b, s, h, dv)


def ssd_causal(x, dt, a, bmat, cmat):
    b, s, h, p = x.shape
    g, nst = bmat.shape[-2], bmat.shape[-1]
    r = h // g
    n = s // CHUNK
    f32 = jnp.float32
    xc = x.astype(f32).reshape(b, n, CHUNK, g, r, p)
    dtc = dt.astype(f32).reshape(b, n, CHUNK, g, r)
    bc = bmat.astype(f32).reshape(b, n, CHUNK, g, nst)
    cc = cmat.astype(f32).reshape(b, n, CHUNK, g, nst)
    cum = jnp.cumsum(dtc * a.astype(f32).reshape(g, r), axis=2)
    cum_t = jnp.moveaxis(cum, 2, -1)
    causal = jnp.tril(jnp.ones((CHUNK, CHUNK), dtype=bool))
    seg = jnp.exp(jnp.where(causal, cum_t[..., :, None] - cum_t[..., None, :], -jnp.inf))
    cb = jnp.einsum("bnigk,bnjgk->bngij", cc, bc)
    w = cb[:, :, :, None] * seg * jnp.moveaxis(dtc, 2, -1)[..., None, :]
    y_diag = jnp.einsum("bngrij,bnjgrp->bnigrp", w, xc)
    decay_to_end = jnp.exp(cum[:, :, -1:] - cum)
    xw = xc * (decay_to_end * dtc)[..., None]
    states = jnp.einsum("bnjgk,bnjgrp->nbgrpk", bc, xw)
    chunk_decay = jnp.moveaxis(jnp.exp(cum[:, :, -1]), 1, 0)

    def step(hstate, inp):
        st, dec = inp
        return dec[..., None, None] * hstate + st, hstate

    _, prev = lax.scan(step, jnp.zeros((b, g, r, p, nst), f32), (states, chunk_decay))
    y_off = jnp.einsum("bnigk,nbgrpk->bnigrp", cc, prev) * jnp.exp(cum)[..., None]
    return (y_diag + y_off).reshape(b, s, h, p)


def retention_branch(q, k, v, g, log_decay, norm_w, cos, sin):
    b, s, _ = q.shape
    q = apply_rope(q.reshape(b, s, RET_HEADS, RET_QK_DIM), cos, sin)
    k = apply_rope(k.reshape(b, s, RET_HEADS, RET_QK_DIM), cos, sin) * (RET_QK_DIM ** -0.5)
    v = v.reshape(b, s, RET_HEADS, RET_V_DIM)
    y = (retention_causal(q, k, v, log_decay[0])
         + seq_flip(retention_causal(seq_flip(q), seq_flip(k), seq_flip(v), log_decay[1])))
    y = rms_norm(y, norm_w.reshape(RET_HEADS, RET_V_DIM), HEAD_NORM_EPS).reshape(b, s, -1)
    return jax.nn.silu(g) * y.astype(g.dtype)


def mla_branch(c_q, c_kv, k_rope, q_norm_w, w_uq, kv_norm_w, w_ukv, cos, sin):
    b, s, _ = c_q.shape
    q = (rms_norm(c_q, q_norm_w) @ w_uq).reshape(b, s, MLA_HEADS, MLA_NOPE_DIM + MLA_ROPE_DIM)
    q_nope, q_rope = jnp.split(q, [MLA_NOPE_DIM], axis=-1)
    q_rope = apply_rope(q_rope, cos, sin)
    kv = (rms_norm(c_kv, kv_norm_w) @ w_ukv).reshape(b, s, MLA_HEADS, MLA_NOPE_DIM + MLA_V_DIM)
    k_nope, v = jnp.split(kv, [MLA_NOPE_DIM], axis=-1)
    k_r = apply_rope(k_rope[:, :, None, :], cos, sin)
    q = jnp.concatenate([q_nope, q_rope], axis=-1)
    k = jnp.concatenate([k_nope, jnp.broadcast_to(k_r, (b, s, MLA_HEADS, MLA_ROPE_DIM))], axis=-1)
    out = softmax_attention(q, k, v, (MLA_NOPE_DIM + MLA_ROPE_DIM) ** -0.5)
    return out.reshape(b, s, -1)


def diff_branch(q, k, v, lam_vec, norm_w, lam_init, cos, sin):
    b, s, _ = q.shape
    q = apply_rope(q.reshape(b, s, DIFF_HEADS, 2, DIFF_HEAD_DIM), cos, sin)
    k = apply_rope(k.reshape(b, s, DIFF_HEADS, 2, DIFF_HEAD_DIM), cos, sin)
    v = v.reshape(b, s, DIFF_HEADS, DIFF_V_DIM)
    lv = lam_vec.astype(jnp.float32)
    lam = jnp.exp(jnp.sum(lv[0] * lv[1])) - jnp.exp(jnp.sum(lv[2] * lv[3])) + lam_init
    out = diff_attention(q[:, :, :, 0], q[:, :, :, 1], k[:, :, :, 0], k[:, :, :, 1], v, lam,
                         DIFF_HEAD_DIM ** -0.5)
    out = rms_norm(out, norm_w, HEAD_NORM_EPS) * (1.0 - lam_init)
    return out.reshape(b, s, -1)


def ssm_branch(z, xbc, dt_raw, conv_w, conv_b, dt_bias, a_log, d_skip, norm_w):
    b, s, _ = z.shape
    f32 = jnp.float32
    xbc = jax.nn.silu(depthwise_conv_centred(xbc, conv_w, conv_b))
    xs, bm, cm = split_cols(xbc, (SSM_INNER, SSM_GROUPS * SSM_STATE, SSM_GROUPS * SSM_STATE))
    xs = xs.reshape(b, s, SSM_HEADS, SSM_HEAD_DIM)
    bm = bm.reshape(b, s, SSM_GROUPS, SSM_STATE)
    cm = cm.reshape(b, s, SSM_GROUPS, SSM_STATE)
    dt = jax.nn.softplus(dt_raw.astype(f32).reshape(b, s, 2, SSM_HEADS) + dt_bias.astype(f32))
    a = -jnp.exp(a_log.astype(f32))
    y_f = ssd_causal(xs, dt[:, :, 0], a[0], bm, cm)
    y_b = seq_flip(ssd_causal(seq_flip(xs), seq_flip(dt[:, :, 1]), a[1], seq_flip(bm), seq_flip(cm)))
    y = y_f + y_b + d_skip.astype(f32)[:, None] * xs.astype(f32)
    y = y.reshape(b, s, SSM_INNER) * jax.nn.silu(z.astype(f32))
    y = rms_norm(y.reshape(b, s, SSM_GROUPS, SSM_INNER // SSM_GROUPS),
                 norm_w.reshape(SSM_GROUPS, SSM_INNER // SSM_GROUPS), HEAD_NORM_EPS)
    return y.reshape(b, s, SSM_INNER).astype(z.dtype)


def encoder_layer(x, l, p, rope_ret, rope_mla, rope_diff):
    b, s, _ = x.shape
    h = rms_norm(x, p["norm_mix_w"][l])
    proj = h @ p["w_in"][l]
    (ret_q, ret_k, ret_v, ret_g, mla_cq, mla_ckv, mla_kr, diff_q, diff_k, diff_v,
     ssm_z, ssm_xbc, ssm_dt, gate_logits) = split_cols(proj, IN_SPLIT_SIZES)
    lam_init = 0.8 - 0.6 * math.exp(-0.3 * l)
    outs = (
        retention_branch(ret_q, ret_k, ret_v, ret_g, p["ret_log_decay"][l], p["ret_norm_w"][l], *rope_ret),
        mla_branch(mla_cq, mla_ckv, mla_kr, p["mla_q_norm_w"][l], p["mla_w_uq"][l],
                   p["mla_kv_norm_w"][l], p["mla_w_ukv"][l], *rope_mla),
        diff_branch(diff_q, diff_k, diff_v, p["diff_lambda"][l], p["diff_norm_w"][l], lam_init, *rope_diff),
        ssm_branch(ssm_z, ssm_xbc, ssm_dt, p["ssm_conv_w"][l], p["ssm_conv_b"][l], p["ssm_dt_bias"][l],
                   p["ssm_a_log"][l], p["ssm_d"][l], p["ssm_norm_w"][l]),
    )
    gates = jax.nn.sigmoid(gate_logits.reshape(b, s, N_BRANCH, D_MODEL))
    merged = sum(gates[:, :, i] * (outs[i].astype(x.dtype) @ p["w_branch"][l, i]) for i in range(N_BRANCH))
    x = x + merged @ p["w_out"][l]
    h = rms_norm(x, p["norm_ffn_w"][l])
    u = depthwise_conv_centred(h @ p["ffn_w_gate"][l], p["ffn_conv_w"][l], p["ffn_conv_b"][l])
    x = x + (jax.nn.silu(u) * (h @ p["ffn_w_up"][l])) @ p["ffn_w_down"][l]
    return x


def trunk(x, p):
    s = x.shape[1]
    rope_ret = rope_tables(s, RET_QK_DIM)
    rope_mla = rope_tables(s, MLA_ROPE_DIM)
    rope_diff = rope_tables(s, DIFF_HEAD_DIM)
    for l in range(DEPTH):
        x = encoder_layer(x, l, p, rope_ret, rope_mla, rope_diff)
    return rms_norm(x, p["final_norm_w"])


def setup_inputs(seed: int = 0) -> dict:
    key = jax.random.key(seed)
    ks = jax.random.split(key, 27)
    f32 = jnp.float32

    def nrm(k, shape, scale):
        return jax.random.normal(k, shape, f32) * scale

    def gain(k, shape):
        return 1.0 + 0.02 * jax.random.normal(k, shape, f32)

    base_decay = jnp.log(1.0 - 2.0 ** (-5.0 - jnp.arange(RET_HEADS, dtype=f32)))
    dt0 = jnp.exp(jax.random.uniform(ks[14], (DEPTH, 2, SSM_HEADS), f32, math.log(1e-3), math.log(1e-1)))
    return {
        "x_prompt": nrm(ks[0], (BATCH, SEQ, D_MODEL), 1.0),
        "x_sample": nrm(ks[1], (DEC_BATCH, DEC_SEQ, D_MODEL), 1.0),
        "norm_mix_w": gain(ks[2], (DEPTH, D_MODEL)),
        "w_in": nrm(ks[3], (DEPTH, D_MODEL, IN_COLS), D_MODEL ** -0.5),
        "ret_log_decay": base_decay * (1.0 + 0.05 * jax.random.normal(ks[4], (DEPTH, 2, RET_HEADS), f32)),
        "ret_norm_w": gain(ks[5], (DEPTH, RET_HEADS * RET_V_DIM)),
        "mla_q_norm_w": gain(ks[6], (DEPTH, MLA_Q_RANK)),
        "mla_w_uq": nrm(ks[7], (DEPTH, MLA_Q_RANK, MLA_HEADS * (MLA_NOPE_DIM + MLA_ROPE_DIM)), MLA_Q_RANK ** -0.5),
        "mla_kv_norm_w": gain(ks[8], (DEPTH, MLA_KV_RANK)),
        "mla_w_ukv": nrm(ks[9], (DEPTH, MLA_KV_RANK, MLA_HEADS * (MLA_NOPE_DIM + MLA_V_DIM)), MLA_KV_RANK ** -0.5),
        "diff_lambda": nrm(ks[10], (DEPTH, 4, DIFF_HEAD_DIM), 0.1),
        "diff_norm_w": gain(ks[11], (DEPTH, DIFF_V_DIM)),
        "ssm_conv_w": nrm(ks[12], (DEPTH, SSM_CONV, SSM_CONV_CH), SSM_CONV ** -0.5),
        "ssm_conv_b": nrm(ks[13], (DEPTH, SSM_CONV_CH), 0.02),
        "ssm_dt_bias": dt0 + jnp.log(-jnp.expm1(-dt0)),
        "ssm_a_log": jnp.log(jax.random.uniform(ks[15], (DEPTH, 2, SSM_HEADS), f32, 1.0, 16.0)),
        "ssm_d": gain(ks[16], (DEPTH, SSM_HEADS)),
        "ssm_norm_w": gain(ks[17], (DEPTH, SSM_INNER)),
        "w_branch": nrm(ks[18], (DEPTH, N_BRANCH, BRANCH_WIDTH, D_MODEL), BRANCH_WIDTH ** -0.5),
        "w_out": nrm(ks[19], (DEPTH, D_MODEL, D_MODEL), D_MODEL ** -0.5),
        "norm_ffn_w": gain(ks[20], (DEPTH, D_MODEL)),
        "ffn_w_gate": nrm(ks[21], (DEPTH, D_MODEL, D_FF), D_MODEL ** -0.5),
        "ffn_w_up": nrm(ks[22], (DEPTH, D_MODEL, D_FF), D_MODEL ** -0.5),
        "ffn_conv_w": nrm(ks[23], (DEPTH, FFN_CONV, D_FF), FFN_CONV ** -0.5),
        "ffn_conv_b": nrm(ks[24], (DEPTH, D_FF), 0.02),
        "ffn_w_down": nrm(ks[25], (DEPTH, D_FF, D_MODEL), D_FF ** -0.5),
        "final_norm_w": gain(ks[26], (D_MODEL,)),
    }


def reference(x_prompt, x_sample, norm_mix_w, w_in, ret_log_decay, ret_norm_w, mla_q_norm_w, mla_w_uq,
              mla_kv_norm_w, mla_w_ukv, diff_lambda, diff_norm_w, ssm_conv_w, ssm_conv_b, ssm_dt_bias,
              ssm_a_log, ssm_d, ssm_norm_w, w_branch, w_out, norm_ffn_w, ffn_w_gate, ffn_w_up, ffn_conv_w,
              ffn_conv_b, ffn_w_down, final_norm_w):
    p = {
        "norm_mix_w": norm_mix_w, "w_in": w_in, "ret_log_decay": ret_log_decay, "ret_norm_w": ret_norm_w,
        "mla_q_norm_w": mla_q_norm_w, "mla_w_uq": mla_w_uq, "mla_kv_norm_w": mla_kv_norm_w,
        "mla_w_ukv": mla_w_ukv, "diff_lambda": diff_lambda, "diff_norm_w": diff_norm_w,
        "ssm_conv_w": ssm_conv_w, "ssm_conv_b": ssm_conv_b, "ssm_dt_bias": ssm_dt_bias,
        "ssm_a_log": ssm_a_log, "ssm_d": ssm_d, "ssm_norm_w": ssm_norm_w, "w_branch": w_branch,
        "w_out": w_out, "norm_ffn_w": norm_ffn_w, "ffn_w_gate": ffn_w_gate, "ffn_w_up": ffn_w_up,
        "ffn_conv_w": ffn_conv_w, "ffn_conv_b": ffn_conv_b, "ffn_w_down": ffn_w_down,
        "final_norm_w": final_norm_w,
    }
    y_prompt = trunk(x_prompt, p)
    y_sample = trunk(x_sample, p)
    return (y_prompt, y_sample)
```

```python
import functools
import math

import jax
import jax.numpy as jnp
from jax import lax
from jax.experimental import pallas as pl
from jax.experimental.pallas import tpu as pltpu

F32 = jnp.float32
BF16 = jnp.bfloat16

D_MODEL = 1024
DEPTH = 2
ROPE_THETA = 10000.0
NORM_EPS = 1e-6
HEAD_NORM_EPS = 1e-5
CHUNK = 128
N_BRANCH = 4
BRANCH_WIDTH = 512

RET_HEADS, RET_QK_DIM, RET_V_DIM = 4, 64, 128
MLA_HEADS, MLA_NOPE_DIM, MLA_ROPE_DIM, MLA_V_DIM = 8, 64, 32, 64
MLA_Q_RANK, MLA_KV_RANK = 256, 128
DIFF_HEADS, DIFF_HEAD_DIM = 4, 64
DIFF_V_DIM = 2 * DIFF_HEAD_DIM
SSM_HEADS, SSM_HEAD_DIM, SSM_GROUPS, SSM_STATE = 8, 64, 2, 128
SSM_INNER = SSM_HEADS * SSM_HEAD_DIM
SSM_CONV_CH = SSM_INNER + 2 * SSM_GROUPS * SSM_STATE
D_FF = 2816

_O_RET = 0
_O_MLA_CQ = 1536
_O_MLA_CKV = 1792
_O_MLA_KR = 1920
_O_DIFF_QK = 1952
_O_DIFF_V = 2976
_O_SSM_Z = 3488
_O_SSM_XBC = 4000
_O_SSM_DT = 5024
_O_GATE = 5040
_IN_COLS = 9136

LANES = 128
SUBLANES = 8
HALO = SUBLANES
PACK_ROWS = 16
HEAD_PAD = LANES
MLA_VA = MLA_V_DIM + PACK_ROWS
DIFF_VA = DIFF_V_DIM + PACK_ROWS
TOKEN_TILE = 512
ATTN_Q_TILE = 256
ATTN_K_TILE = 512
LOG2E = 1.4426950408889634
NEG_BIG = -1e30
VMEM_LIMIT = 56 * 1024 * 1024


def _cparams(sems, vmem=VMEM_LIMIT):
    return pltpu.CompilerParams(dimension_semantics=sems, vmem_limit_bytes=vmem)


def _full(shape):
    return pl.BlockSpec(shape, lambda *_: (0,) * len(shape))


def _nt(a, b):
    return lax.dot_general(a, b, (((1,), (1,)), ((), ())), preferred_element_type=F32)


def _dot(a, b):
    return jnp.dot(a, b, preferred_element_type=F32)


def _rms(x, w, eps):
    return x * lax.rsqrt(jnp.mean(x * x, axis=-1, keepdims=True) + eps) * w


def _sigmoid(x):
    return 1.0 / (1.0 + jnp.exp(-x))


def _silu(x):
    return x * _sigmoid(x)


def _rope(x, c, s1, s2, half):
    return x * c + pltpu.roll(x, LANES - half, 1) * s1 + pltpu.roll(x, half, 1) * s2


def _rope_tables(seq, dim, block, off):
    inv_freq = 1.0 / (ROPE_THETA ** (jnp.arange(0, dim, 2, dtype=F32) / dim))
    ang = jnp.arange(seq, dtype=F32)[:, None] * inv_freq[None, :]
    cos, sin = jnp.cos(ang), jnp.sin(ang)
    half = dim // 2
    zero = jnp.zeros_like(sin)
    pre = jnp.zeros((seq, off), F32)
    post = jnp.zeros((seq, block - off - dim), F32)
    c = jnp.concatenate([pre + 1.0, cos, cos, post + 1.0], axis=1)
    s1 = jnp.concatenate([pre, -sin, zero, post], axis=1)
    s2 = jnp.concatenate([pre, zero, sin, post], axis=1)
    reps = LANES // block
    return tuple(jnp.tile(t, (1, reps)) for t in (c, s1, s2))


def _ret_in_kernel(x_ref, nw_ref, w_ref, c_ref, s1_ref, s2_ref, q_ref, k_ref, v_ref, g_ref):
    h = _rms(x_ref[0], nw_ref[...], NORM_EPS).astype(BF16)
    p = _dot(h, w_ref[...])
    c, s1, s2 = c_ref[...], s1_ref[...], s2_ref[...]
    half = RET_QK_DIM // 2
    for i in range(2):
        sl = slice(i * LANES, (i + 1) * LANES)
        q_ref[0, :, sl] = _rope(p[:, sl], c, s1, s2, half).astype(BF16)
        kk = _rope(p[:, 256 + i * LANES:256 + (i + 1) * LANES], c, s1, s2, half)
        k_ref[0, :, sl] = (kk * (RET_QK_DIM ** -0.5)).astype(BF16)
    v_ref[0] = p[:, 512:1024].astype(BF16)
    g_ref[0] = _silu(p[:, 1024:1536]).astype(BF16)


def _ret_in(x, nw, w, tabs):
    b, s, _ = x.shape
    tm = TOKEN_TILE
    tok = lambda width: pl.BlockSpec((1, tm, width), lambda bi, i: (bi, i, 0))
    tab = pl.BlockSpec((tm, LANES), lambda bi, i: (i, 0))
    return pl.pallas_call(
        _ret_in_kernel,
        grid=(b, s // tm),
        in_specs=[tok(D_MODEL), _full((1, D_MODEL)), _full(w.shape), tab, tab, tab],
        out_specs=[tok(256), tok(256), tok(512), tok(512)],
        out_shape=[jax.ShapeDtypeStruct((b, s, n), BF16) for n in (256, 256, 512, 512)],
        compiler_params=_cparams(("parallel", "parallel")),
        name="ret_in",
    )(x, nw, w, *tabs)


def _ones_rows(rows, cols, period, at):
    r = lax.broadcasted_iota(jnp.int32, (rows, cols), 0)
    return jnp.where(r % period == at, 1.0, 0.0).astype(F32)


def _mla_in_kernel(x_ref, nw_ref, w1_ref, qnw_ref, kvnw_ref, wq_ref, wk_ref, wvt_ref,
                   c_ref, s1_ref, s2_ref, q_ref, k_ref, vt_ref):
    h = _rms(x_ref[0], nw_ref[...], NORM_EPS).astype(BF16)
    p = _dot(h, w1_ref[...])
    cqn = _rms(p[:, 0:256], qnw_ref[...], NORM_EPS).astype(BF16)
    ckvn = _rms(p[:, 256:384], kvnw_ref[...], NORM_EPS).astype(BF16)
    c, s1, s2 = c_ref[...], s1_ref[...], s2_ref[...]
    half = MLA_ROPE_DIM // 2
    krr = _rope(p[:, 384:512], c, s1, s2, half)
    qp = _dot(cqn, wq_ref[...])
    kp = _dot(ckvn, wk_ref[...])
    qscale = ((MLA_NOPE_DIM + MLA_ROPE_DIM) ** -0.5) * LOG2E
    for hd in range(MLA_HEADS):
        sl = slice(hd * HEAD_PAD, (hd + 1) * HEAD_PAD)
        q_ref[0, :, sl] = (_rope(qp[:, sl], c, s1, s2, half) * qscale).astype(BF16)
        k_ref[0, :, sl] = (kp[:, sl] + krr).astype(BF16)
    vt = _nt(wvt_ref[...], ckvn)
    vt_ref[0] = (vt + _ones_rows(vt.shape[0], vt.shape[1], MLA_VA, MLA_V_DIM)).astype(BF16)


def _mla_in(x, nw, w1, qnw, kvnw, wq, wk, wvt, tabs):
    b, s, _ = x.shape
    tm = TOKEN_TILE
    tok = lambda width: pl.BlockSpec((1, tm, width), lambda bi, i: (bi, i, 0))
    tab = pl.BlockSpec((tm, LANES), lambda bi, i: (i, 0))
    rows = MLA_HEADS * MLA_VA
    return pl.pallas_call(
        _mla_in_kernel,
        grid=(b, s // tm),
        in_specs=[tok(D_MODEL), _full((1, D_MODEL)), _full(w1.shape), _full(qnw.shape), _full(kvnw.shape),
                  _full(wq.shape), _full(wk.shape), _full(wvt.shape), tab, tab, tab],
        out_specs=[tok(1024), tok(1024), pl.BlockSpec((1, rows, tm), lambda bi, i: (bi, 0, i))],
        out_shape=[jax.ShapeDtypeStruct((b, s, 1024), BF16), jax.ShapeDtypeStruct((b, s, 1024), BF16),
                   jax.ShapeDtypeStruct((b, rows, s), BF16)],
        compiler_params=_cparams(("parallel", "parallel")),
        name="mla_in",
    )(x, nw, w1, qnw, kvnw, wq, wk, wvt, *tabs)


def _diff_in_kernel(x_ref, nw_ref, wqk_ref, wvt_ref, c_ref, s1_ref, s2_ref, q_ref, k_ref, vt_ref):
    h = _rms(x_ref[0], nw_ref[...], NORM_EPS).astype(BF16)
    p = _dot(h, wqk_ref[...])
    c, s1, s2 = c_ref[...], s1_ref[...], s2_ref[...]
    half = DIFF_HEAD_DIM // 2
    qscale = (DIFF_HEAD_DIM ** -0.5) * LOG2E
    for hd in range(DIFF_HEADS):
        sl = slice(hd * HEAD_PAD, (hd + 1) * HEAD_PAD)
        q_ref[0, :, sl] = (_rope(p[:, sl], c, s1, s2, half) * qscale).astype(BF16)
        k_ref[0, :, sl] = _rope(p[:, 512 + hd * HEAD_PAD:512 + (hd + 1) * HEAD_PAD], c, s1, s2, half).astype(BF16)
    vt = _nt(wvt_ref[...], h)
    vt_ref[0] = (vt + _ones_rows(vt.shape[0], vt.shape[1], DIFF_VA, DIFF_V_DIM)).astype(BF16)


def _diff_in(x, nw, wqk, wvt, tabs):
    b, s, _ = x.shape
    tm = TOKEN_TILE
    tok = lambda width: pl.BlockSpec((1, tm, width), lambda bi, i: (bi, i, 0))
    tab = pl.BlockSpec((tm, LANES), lambda bi, i: (i, 0))
    rows = DIFF_HEADS * DIFF_VA
    return pl.pallas_call(
        _diff_in_kernel,
        grid=(b, s // tm),
        in_specs=[tok(D_MODEL), _full((1, D_MODEL)), _full(wqk.shape), _full(wvt.shape), tab, tab, tab],
        out_specs=[tok(512), tok(512), pl.BlockSpec((1, rows, tm), lambda bi, i: (bi, 0, i))],
        out_shape=[jax.ShapeDtypeStruct((b, s, 512), BF16), jax.ShapeDtypeStruct((b, s, 512), BF16),
                   jax.ShapeDtypeStruct((b, rows, s), BF16)],
        compiler_params=_cparams(("parallel", "parallel")),
        name="diff_in",
    )(x, nw, wqk, wvt, *tabs)


def _halo_specs(tm, s, width):
    per = tm // HALO
    last = s // HALO - 1
    main = pl.BlockSpec((1, tm, width), lambda bi, i: (bi, i, 0))
    prev = pl.BlockSpec((1, HALO, width), lambda bi, i: (bi, jnp.maximum(i * per - 1, 0), 0))
    nxt = pl.BlockSpec((1, HALO, width), lambda bi, i: (bi, jnp.minimum((i + 1) * per, last), 0))
    return prev, main, nxt


def _conv3(g_ref, tm, w_ref, b_ref):
    u = w_ref[0:1, :] * g_ref[pl.ds(HALO - 1, tm), :]
    u = u + w_ref[1:2, :] * g_ref[pl.ds(HALO, tm), :]
    u = u + w_ref[2:3, :] * g_ref[pl.ds(HALO + 1, tm), :]
    return u + b_ref[...]


def _zero_halo_at_sequence_ends(g_ref, tm):
    i = pl.program_id(1)

    @pl.when(i == 0)
    def _():
        g_ref[0:HALO, :] = jnp.zeros((HALO, g_ref.shape[1]), F32)

    @pl.when(i == pl.num_programs(1) - 1)
    def _():
        g_ref[HALO + tm:2 * HALO + tm, :] = jnp.zeros((HALO, g_ref.shape[1]), F32)


def _ssm_in_kernel(xp_ref, x_ref, xn_ref, nw_ref, wx_ref, wzd_ref, cw_ref, cb_ref, dtb_ref,
                   xbc_ref, z_ref, dt_ref, g_ref):
    tm = x_ref.shape[1]
    nw = nw_ref[...]
    hm = _rms(x_ref[0], nw, NORM_EPS)
    hcat = jnp.concatenate([_rms(xp_ref[0], nw, NORM_EPS), hm, _rms(xn_ref[0], nw, NORM_EPS)], axis=0)
    g_ref[...] = _dot(hcat.astype(BF16), wx_ref[...])
    _zero_halo_at_sequence_ends(g_ref, tm)
    xbc_ref[0] = _silu(_conv3(g_ref, tm, cw_ref, cb_ref))
    zd = _dot(hm.astype(BF16), wzd_ref[...])
    z_ref[0] = _silu(zd[:, 0:512])
    t = zd[:, 512:640] + dtb_ref[...]
    sp = jnp.maximum(t, 0.0) + jnp.log1p(jnp.exp(-jnp.abs(t)))
    lane = lax.broadcasted_iota(jnp.int32, sp.shape, 1)
    dt_ref[0] = jnp.where(lane < 2 * SSM_HEADS, sp, 0.0)


def _ssm_in(x, nw, wx, wzd, cw, cb, dtb):
    b, s, _ = x.shape
    tm = TOKEN_TILE
    tok = lambda width: pl.BlockSpec((1, tm, width), lambda bi, i: (bi, i, 0))
    return pl.pallas_call(
        _ssm_in_kernel,
        grid=(b, s // tm),
        in_specs=[*_halo_specs(tm, s, D_MODEL), _full((1, D_MODEL)), _full(wx.shape), _full(wzd.shape),
                  _full(cw.shape), _full(cb.shape), _full(dtb.shape)],
        out_specs=[tok(1024), tok(512), tok(LANES)],
        out_shape=[jax.ShapeDtypeStruct((b, s, 1024), F32), jax.ShapeDtypeStruct((b, s, 512), F32),
                   jax.ShapeDtypeStruct((b, s, LANES), F32)],
        scratch_shapes=[pltpu.VMEM((tm + 2 * HALO, SSM_CONV_CH), F32)],
        compiler_params=_cparams(("parallel", "parallel")),
        name="ssm_in",
    )(x, x, x, nw, wx, wzd, cw, cb, dtb)


def _ret_state_kernel(kf_ref, vf_ref, kb_ref, vb_ref, kdf_ref, kdb_ref, cdf_ref, cdb_ref,
                      sf_ref, sb_ref, st_ref):
    @pl.when(pl.program_id(1) == 0)
    def _():
        st_ref[...] = jnp.zeros(st_ref.shape, F32)

    def scan_step(d, k_ref, v_ref, kd_ref, cd_ref, out_ref):
        st = st_ref[d]
        out_ref[0, 0] = st
        kdt = (k_ref[0].astype(F32) * kd_ref[...]).T.astype(BF16)
        v = v_ref[0]
        inc = [_dot(kdt[hd * RET_QK_DIM:(hd + 1) * RET_QK_DIM], v[:, hd * RET_V_DIM:(hd + 1) * RET_V_DIM])
               for hd in range(RET_HEADS)]
        st_ref[d] = cd_ref[...] * st + jnp.concatenate(inc, axis=0)

    scan_step(0, kf_ref, vf_ref, kdf_ref, cdf_ref, sf_ref)
    scan_step(1, kb_ref, vb_ref, kdb_ref, cdb_ref, sb_ref)


def _ret_state(k, v, kdf, kdb, cdf, cdb):
    b, s, _ = k.shape
    n = s // CHUNK
    fwd = lambda width: pl.BlockSpec((1, CHUNK, width), lambda bi, c: (bi, c, 0))
    bwd = lambda width: pl.BlockSpec((1, CHUNK, width), lambda bi, c: (bi, n - 1 - c, 0))
    rows = RET_HEADS * RET_QK_DIM
    st_shape = jax.ShapeDtypeStruct((b, n, rows, RET_V_DIM), F32)
    return pl.pallas_call(
        _ret_state_kernel,
        grid=(b, n),
        in_specs=[fwd(256), fwd(512), bwd(256), bwd(512), _full(kdf.shape), _full(kdb.shape),
                  _full(cdf.shape), _full(cdb.shape)],
        out_specs=[pl.BlockSpec((1, 1, rows, RET_V_DIM), lambda bi, c: (bi, c, 0, 0)),
                   pl.BlockSpec((1, 1, rows, RET_V_DIM), lambda bi, c: (bi, n - 1 - c, 0, 0))],
        out_shape=[st_shape, st_shape],
        scratch_shapes=[pltpu.VMEM((2, rows, RET_V_DIM), F32)],
        compiler_params=_cparams(("parallel", "arbitrary")),
        name="ret_state",
    )(k, v, k, v, kdf, kdb, cdf, cdb)


def _ret_out_kernel(q_ref, k_ref, v_ref, g_ref, sf_ref, sb_ref, d_ref, qdf_ref, qdb_ref, nw_ref, o_ref):
    q = q_ref[0].astype(F32)
    qf = q * qdf_ref[...]
    qb = q * qdb_ref[...]
    k = k_ref[0]
    v = v_ref[0]
    sf = sf_ref[0, 0].astype(BF16)
    sb = sb_ref[0, 0].astype(BF16)
    lane = lax.broadcasted_iota(jnp.int32, (CHUNK, LANES), 1)
    for hd in range(RET_HEADS):
        pair = slice((hd // 2) * LANES, (hd // 2 + 1) * LANES)
        own = (lane < RET_QK_DIM) if hd % 2 == 0 else (lane >= RET_QK_DIM)
        pick = lambda t: jnp.where(own, t[:, pair], 0.0).astype(BF16)
        sc = _nt(pick(q), k[:, pair]) * d_ref[hd]
        vsl = slice(hd * RET_V_DIM, (hd + 1) * RET_V_DIM)
        y = _dot(sc.astype(BF16), v[:, vsl]) + _dot(pick(qf), sf[pair, :]) + _dot(pick(qb), sb[pair, :])
        y = _rms(y, nw_ref[:, vsl], HEAD_NORM_EPS)
        o_ref[0, :, vsl] = (g_ref[0, :, vsl].astype(F32) * y).astype(BF16)


def _ret_out(q, k, v, g, sf, sb, dmat, qdf, qdb, nw):
    b, s, _ = q.shape
    n = s // CHUNK
    tok = lambda width: pl.BlockSpec((1, CHUNK, width), lambda bi, c: (bi, c, 0))
    st = pl.BlockSpec((1, 1) + sf.shape[2:], lambda bi, c: (bi, c, 0, 0))
    return pl.pallas_call(
        _ret_out_kernel,
        grid=(b, n),
        in_specs=[tok(256), tok(256), tok(512), tok(512), st, st, _full(dmat.shape), _full(qdf.shape),
                  _full(qdb.shape), _full(nw.shape)],
        out_specs=tok(512),
        out_shape=jax.ShapeDtypeStruct((b, s, 512), BF16),
        compiler_params=_cparams(("parallel", "parallel")),
        name="ret_out",
    )(q, k, v, g, sf, sb, dmat, qdf, qdb, nw)


def _ret_tables(log_decay):
    lg_f = log_decay[0].astype(F32)
    lg_b = log_decay[1].astype(F32)
    pos = jnp.arange(CHUNK, dtype=F32)
    rel = pos[:, None] - pos[None, :]
    low = jnp.where(rel >= 0, jnp.exp(jnp.maximum(rel, 0.0)[None] * lg_f[:, None, None]), 0.0)
    upp = jnp.where(rel <= 0, jnp.exp(jnp.maximum(-rel, 0.0)[None] * lg_b[:, None, None]), 0.0)
    dmat = low + upp
    wide = lambda t: jnp.repeat(t, RET_QK_DIM, axis=1)
    qdf = wide(jnp.exp((pos + 1.0)[:, None] * lg_f))
    qdb = wide(jnp.exp((CHUNK - pos)[:, None] * lg_b))
    kdf = wide(jnp.exp((CHUNK - 1.0 - pos)[:, None] * lg_f))
    kdb = wide(jnp.exp(pos[:, None] * lg_b))
    tall = lambda lg: jnp.broadcast_to(jnp.repeat(jnp.exp(CHUNK * lg), RET_QK_DIM)[:, None],
                                       (RET_HEADS * RET_QK_DIM, RET_V_DIM))
    return dmat, qdf, qdb, kdf, kdb, tall(lg_f), tall(lg_b)


def _cumsum_lanes(x, reverse):
    lane = lax.broadcasted_iota(jnp.int32, x.shape, 1)
    for step in range(7):
        sh = 1 << step
        if reverse:
            x = x + jnp.where(lane < LANES - sh, pltpu.roll(x, LANES - sh, 1), 0.0)
        else:
            x = x + jnp.where(lane >= sh, pltpu.roll(x, sh, 1), 0.0)
    return x


def _ssd_state_kernel(xf_ref, dtf_ref, xb_ref, dtb_ref, alog_ref, sf_ref, sb_ref, st_ref):
    @pl.when(pl.program_id(1) == 0)
    def _():
        st_ref[...] = jnp.zeros(st_ref.shape, F32)

    head_of_lane = lax.broadcasted_iota(jnp.int32, (1, SSM_INNER), 1) // SSM_HEAD_DIM

    def scan_step(d, x_ref, dt_ref, out_ref):
        st = st_ref[d]
        out_ref[0, 0] = st
        r0 = d * SSM_HEADS
        dtt = dt_ref[0].T[r0:r0 + SSM_HEADS]
        a = -jnp.exp(alog_ref[r0:r0 + SSM_HEADS, :])
        cum = _cumsum_lanes(dtt * a, reverse=(d == 1))
        end = cum[:, 0:1] if d == 1 else cum[:, CHUNK - 1:CHUNK]
        endb = jnp.broadcast_to(end, cum.shape)
        wt = jnp.exp(endb - cum) * dtt
        edec = jnp.exp(endb)
        xbc = x_ref[0]
        xt = xbc[:, 0:SSM_INNER].T
        xwt = jnp.concatenate(
            [xt[hd * SSM_HEAD_DIM:(hd + 1) * SSM_HEAD_DIM] * wt[hd:hd + 1] for hd in range(SSM_HEADS)],
            axis=0).astype(BF16)
        inc = []
        per_group = SSM_INNER // SSM_GROUPS
        for g in range(SSM_GROUPS):
            bt = xbc[:, SSM_INNER + g * SSM_STATE:SSM_INNER + (g + 1) * SSM_STATE].T.astype(BF16)
            inc.append(_nt(bt, xwt[g * per_group:(g + 1) * per_group]))
        dec = jnp.zeros((1, SSM_INNER), F32)
        for hd in range(SSM_HEADS):
            row = jnp.concatenate([edec[hd:hd + 1]] * (SSM_INNER // LANES), axis=1)
            dec = jnp.where(head_of_lane == hd, row, dec)
        st_ref[d] = st * dec + jnp.concatenate(inc, axis=1)

    scan_step(0, xf_ref, dtf_ref, sf_ref)
    scan_step(1, xb_ref, dtb_ref, sb_ref)


def _ssd_state(xbc, dt, alog):
    b, s, _ = xbc.shape
    n = s // CHUNK
    fwd = lambda width: pl.BlockSpec((1, CHUNK, width), lambda bi, c: (bi, c, 0))
    bwd = lambda width: pl.BlockSpec((1, CHUNK, width), lambda bi, c: (bi, n - 1 - c, 0))
    st_shape = jax.ShapeDtypeStruct((b, n, SSM_STATE, SSM_INNER), F32)
    return pl.pallas_call(
        _ssd_state_kernel,
        grid=(b, n),
        in_specs=[fwd(768), fwd(LANES), bwd(768), bwd(LANES), _full(alog.shape)],
        out_specs=[pl.BlockSpec((1, 1, SSM_STATE, SSM_INNER), lambda bi, c: (bi, c, 0, 0)),
                   pl.BlockSpec((1, 1, SSM_STATE, SSM_INNER), lambda bi, c: (bi, n - 1 - c, 0, 0))],
        out_shape=[st_shape, st_shape],
        scratch_shapes=[pltpu.VMEM((2, SSM_STATE, SSM_INNER), F32)],
        compiler_params=_cparams(("parallel", "arbitrary")),
        name="ssd_state",
    )(xbc, dt, xbc, dt, alog)


def _ssd_out_kernel(x_ref, dt_ref, z_ref, sf_ref, sb_ref, alog_ref, dskip_ref, nw_ref, o_ref):
    xbc = x_ref[0]
    nh = SSM_HEADS
    dtt = dt_ref[0].T[0:2 * nh]
    dat = dtt * (-jnp.exp(alog_ref[...]))
    cumt = jnp.concatenate([_cumsum_lanes(dat[0:nh], False), _cumsum_lanes(dat[nh:2 * nh], True)], axis=0)
    cum = jnp.concatenate([cumt, jnp.zeros((LANES - 2 * nh, CHUNK), F32)], axis=0).T
    x = xbc[:, 0:SSM_INNER]
    xb = x.astype(BF16)
    sf = sf_ref[0, 0].astype(BF16)
    sb = sb_ref[0, 0].astype(BF16)
    ii = lax.broadcasted_iota(jnp.int32, (CHUNK, CHUNK), 0)
    jj = lax.broadcasted_iota(jnp.int32, (CHUNK, CHUNK), 1)
    low = ii >= jj
    upp = jj >= ii
    first = jj < SSM_HEAD_DIM
    heads_per_group = nh // SSM_GROUPS
    ys = []
    for pr in range(nh // 2):
        g = (2 * pr) // heads_per_group
        cg = xbc[:, SSM_INNER + SSM_GROUPS * SSM_STATE + g * SSM_STATE:
                 SSM_INNER + SSM_GROUPS * SSM_STATE + (g + 1) * SSM_STATE].astype(BF16)
        bg = xbc[:, SSM_INNER + g * SSM_STATE:SSM_INNER + (g + 1) * SSM_STATE].astype(BF16)
        cb = _nt(cg, bg)
        pair = slice(pr * LANES, (pr + 1) * LANES)
        diag = []
        ecol = []
        for hd in (2 * pr, 2 * pr + 1):
            colf = jnp.broadcast_to(cum[:, hd:hd + 1], (CHUNK, CHUNK))
            colb = jnp.broadcast_to(cum[:, nh + hd:nh + hd + 1], (CHUNK, CHUNK))
            segf = jnp.exp(jnp.where(low, colf - cumt[hd:hd + 1], NEG_BIG))
            segb = jnp.exp(jnp.where(upp, colb - cumt[nh + hd:nh + hd + 1], NEG_BIG))
            w = cb * (segf * dtt[hd:hd + 1] + segb * dtt[nh + hd:nh + hd + 1])
            diag.append(_dot(w.astype(BF16), xb[:, pair]))
            ecol.append((jnp.exp(colf), jnp.exp(colb)))
        yd = jnp.where(first, diag[0], diag[1])
        ef = jnp.where(first, ecol[0][0], ecol[1][0])
        eb = jnp.where(first, ecol[0][1], ecol[1][1])
        ys.append(yd + ef * _dot(cg, sf[:, pair]) + eb * _dot(cg, sb[:, pair]))
    y = jnp.concatenate(ys, axis=1) + dskip_ref[...] * x
    y = y * z_ref[0]
    gw = SSM_INNER // SSM_GROUPS
    for g in range(SSM_GROUPS):
        sl = slice(g * gw, (g + 1) * gw)
        o_ref[0, :, sl] = _rms(y[:, sl], nw_ref[:, sl], HEAD_NORM_EPS).astype(BF16)


def _ssd_out(xbc, dt, z, sf, sb, alog, dskip, nw):
    b, s, _ = xbc.shape
    n = s // CHUNK
    tok = lambda width: pl.BlockSpec((1, CHUNK, width), lambda bi, c: (bi, c, 0))
    st = pl.BlockSpec((1, 1, SSM_STATE, SSM_INNER), lambda bi, c: (bi, c, 0, 0))
    return pl.pallas_call(
        _ssd_out_kernel,
        grid=(b, n),
        in_specs=[tok(1024), tok(LANES), tok(512), st, st, _full(alog.shape), _full(dskip.shape), _full(nw.shape)],
        out_specs=tok(512),
        out_shape=jax.ShapeDtypeStruct((b, s, 512), BF16),
        compiler_params=_cparams(("parallel", "parallel")),
        name="ssd_out",
    )(xbc, dt, z, sf, sb, alog, dskip, nw)


def _attn_scores_t(q, k_ref, vt_ref, seq):
    tk = min(ATTN_K_TILE, seq)
    rows = vt_ref.shape[1]

    def body(c, carry):
        m, acc = carry
        off = pl.multiple_of(c * tk, tk)
        s = _nt(k_ref[0, pl.ds(off, tk), :], q)
        m_new = jnp.maximum(m, jnp.max(s, axis=0, keepdims=True))
        p = jnp.exp2(s - m_new).astype(BF16)
        acc = acc * jnp.exp2(m - m_new) + _dot(vt_ref[0, :, pl.ds(off, tk)], p)
        return m_new, acc

    init = (jnp.full((1, ATTN_Q_TILE), NEG_BIG, F32), jnp.zeros((rows, ATTN_Q_TILE), F32))
    return lax.fori_loop(0, seq // tk, body, init)[1]


def _mla_attn_kernel(q_ref, k_ref, vt_ref, o_ref):
    acc = _attn_scores_t(q_ref[0], k_ref, vt_ref, k_ref.shape[1])
    o_ref[0] = (acc[0:MLA_V_DIM] / acc[MLA_V_DIM:MLA_V_DIM + 1]).astype(BF16)


def _mla_attn(q, k, vt):
    b, s, _ = q.shape
    tq = ATTN_Q_TILE
    return pl.pallas_call(
        _mla_attn_kernel,
        grid=(b, MLA_HEADS, s // tq),
        in_specs=[pl.BlockSpec((1, tq, HEAD_PAD), lambda bi, h, i: (bi, i, h)),
                  pl.BlockSpec((1, s, HEAD_PAD), lambda bi, h, i: (bi, 0, h)),
                  pl.BlockSpec((1, MLA_VA, s), lambda bi, h, i: (bi, h, 0))],
        out_specs=pl.BlockSpec((1, MLA_V_DIM, tq), lambda bi, h, i: (bi, h, i)),
        out_shape=jax.ShapeDtypeStruct((b, MLA_HEADS * MLA_V_DIM, s), BF16),
        compiler_params=_cparams(("parallel", "parallel", "parallel")),
        name="mla_attn",
    )(q, k, vt)


def _diff_attn_kernel(q_ref, k_ref, vt_ref, lam_ref, nw_ref, o_ref, *, lam_init):
    tq = ATTN_Q_TILE // 2
    q = q_ref[0].astype(F32)
    lane = lax.broadcasted_iota(jnp.int32, q.shape, 1)
    first = lane < DIFF_HEAD_DIM
    both = jnp.concatenate([jnp.where(first, q, 0.0), jnp.where(first, 0.0, q)], axis=0).astype(BF16)
    acc = _attn_scores_t(both, k_ref, vt_ref, k_ref.shape[1])
    dv = DIFF_V_DIM
    o1 = acc[0:dv, 0:tq] / acc[dv:dv + 1, 0:tq]
    o2 = acc[0:dv, tq:2 * tq] / acc[dv:dv + 1, tq:2 * tq]
    lv = lam_ref[...]
    lam = (jnp.exp(jnp.sum(lv[0:1] * lv[1:2], axis=1, keepdims=True))
           - jnp.exp(jnp.sum(lv[2:3] * lv[3:4], axis=1, keepdims=True)) + lam_init)
    o = o1 - lam * o2
    o = o * lax.rsqrt(jnp.mean(o * o, axis=0, keepdims=True) + HEAD_NORM_EPS) * nw_ref[...]
    o_ref[0] = (o * (1.0 - lam_init)).astype(BF16)


def _diff_attn(q, k, vt, lam_vec, nw, lam_init):
    b, s, _ = q.shape
    tq = ATTN_Q_TILE // 2
    return pl.pallas_call(
        functools.partial(_diff_attn_kernel, lam_init=lam_init),
        grid=(b, DIFF_HEADS, s // tq),
        in_specs=[pl.BlockSpec((1, tq, HEAD_PAD), lambda bi, h, i: (bi, i, h)),
                  pl.BlockSpec((1, s, HEAD_PAD), lambda bi, h, i: (bi, 0, h)),
                  pl.BlockSpec((1, DIFF_VA, s), lambda bi, h, i: (bi, h, 0)),
                  _full(lam_vec.shape), _full(nw.shape)],
        out_specs=pl.BlockSpec((1, DIFF_V_DIM, tq), lambda bi, h, i: (bi, h, i)),
        out_shape=jax.ShapeDtypeStruct((b, DIFF_HEADS * DIFF_V_DIM, s), BF16),
        compiler_params=_cparams(("parallel", "parallel", "parallel")),
        name="diff_attn",
    )(q, k, vt, lam_vec, nw)


def _merge_kernel(x_ref, nw_ref, wg_ref, ret_ref, mlat_ref, difft_ref, ssm_ref, wb_ref, wo_ref, o_ref):
    x = x_ref[0]
    h = _rms(x, nw_ref[...], NORM_EPS).astype(BF16)
    rows = lambda t_ref: t_ref[0].astype(F32).T.astype(BF16)
    outs = (ret_ref[0], rows(mlat_ref), rows(difft_ref), ssm_ref[0])
    merged = None
    for i in range(N_BRANCH):
        gate = _sigmoid(_dot(h, wg_ref[:, i * D_MODEL:(i + 1) * D_MODEL]))
        term = gate * _dot(outs[i], wb_ref[i])
        merged = term if merged is None else merged + term
    o_ref[0] = x + _dot(merged.astype(BF16), wo_ref[...])


def _merge(x, nw, wg, ret, mlat, difft, ssm, wb, wo):
    b, s, _ = x.shape
    tm = TOKEN_TILE
    tok = lambda width: pl.BlockSpec((1, tm, width), lambda bi, i: (bi, i, 0))
    tr = pl.BlockSpec((1, BRANCH_WIDTH, tm), lambda bi, i: (bi, 0, i))
    return pl.pallas_call(
        _merge_kernel,
        grid=(b, s // tm),
        in_specs=[tok(D_MODEL), _full((1, D_MODEL)), _full(wg.shape), tok(512), tr, tr, tok(512),
                  _full(wb.shape), _full(wo.shape)],
        out_specs=tok(D_MODEL),
        out_shape=jax.ShapeDtypeStruct((b, s, D_MODEL), F32),
        compiler_params=_cparams(("parallel", "parallel")),
        name="merge",
    )(x, nw, wg, ret, mlat, difft, ssm, wb, wo)


FFN_COL_TILE = 1408


def _ffn_kernel(xp_ref, x_ref, xn_ref, nw_ref, wgate_ref, wup_ref, cw_ref, cb_ref, wdown_ref, fnw_ref,
                o_ref, g_ref, *, final_norm):
    tm = x_ref.shape[1]
    nw = nw_ref[...]
    x = x_ref[0]
    hm = _rms(x, nw, NORM_EPS)
    hcat = jnp.concatenate([_rms(xp_ref[0], nw, NORM_EPS), hm, _rms(xn_ref[0], nw, NORM_EPS)],
                           axis=0).astype(BF16)
    hb = hm.astype(BF16)
    acc = x
    for j in range(D_FF // FFN_COL_TILE):
        cols = slice(j * FFN_COL_TILE, (j + 1) * FFN_COL_TILE)
        g_ref[...] = _dot(hcat, wgate_ref[:, cols])
        _zero_halo_at_sequence_ends(g_ref, tm)
        u = _conv3(g_ref, tm, cw_ref.at[:, cols], cb_ref.at[:, cols])
        act = (_silu(u) * _dot(hb, wup_ref[:, cols])).astype(BF16)
        acc = acc + _dot(act, wdown_ref[cols, :])
    if final_norm:
        acc = _rms(acc, fnw_ref[...], NORM_EPS)
    o_ref[0] = acc


def _ffn(x, nw, wgate, wup, cw, cb, wdown, fnw, final_norm):
    b, s, _ = x.shape
    tm = TOKEN_TILE
    return pl.pallas_call(
        functools.partial(_ffn_kernel, final_norm=final_norm),
        grid=(b, s // tm),
        in_specs=[*_halo_specs(tm, s, D_MODEL), _full((1, D_MODEL)), _full(wgate.shape), _full(wup.shape),
                  _full(cw.shape), _full(cb.shape), _full(wdown.shape), _full((1, D_MODEL))],
        out_specs=pl.BlockSpec((1, tm, D_MODEL), lambda bi, i: (bi, i, 0)),
        out_shape=jax.ShapeDtypeStruct((b, s, D_MODEL), F32),
        scratch_shapes=[pltpu.VMEM((tm + 2 * HALO, FFN_COL_TILE), F32)],
        compiler_params=_cparams(("parallel", "parallel")),
        name="ffn",
    )(x, x, x, nw, wgate, wup, cw, cb, wdown, fnw)


def _pad_axis(t, axis, size):
    pad = [(0, 0)] * t.ndim
    pad[axis] = (0, size - t.shape[axis])
    return jnp.pad(t, pad)


def _layer_params(l, p):
    w_in = p["w_in"][l]
    row = lambda t: t.reshape(1, -1).astype(F32)
    d = {}
    d["norm_mix"] = row(p["norm_mix_w"][l])
    d["w_ret"] = w_in[:, _O_RET:_O_MLA_CQ].astype(BF16)
    kr = jnp.pad(w_in[:, _O_MLA_KR:_O_DIFF_QK], ((0, 0), (MLA_NOPE_DIM, HEAD_PAD - MLA_NOPE_DIM - MLA_ROPE_DIM)))
    d["w_mla1"] = jnp.concatenate([w_in[:, _O_MLA_CQ:_O_MLA_KR], kr], axis=1).astype(BF16)
    d["mla_qnw"] = row(p["mla_q_norm_w"][l])
    d["mla_kvnw"] = row(p["mla_kv_norm_w"][l])
    uq = p["mla_w_uq"][l].reshape(MLA_Q_RANK, MLA_HEADS, MLA_NOPE_DIM + MLA_ROPE_DIM)
    d["w_mla_q"] = _pad_axis(uq, 2, HEAD_PAD).reshape(MLA_Q_RANK, MLA_HEADS * HEAD_PAD).astype(BF16)
    ukv = p["mla_w_ukv"][l].reshape(MLA_KV_RANK, MLA_HEADS, MLA_NOPE_DIM + MLA_V_DIM)
    d["w_mla_k"] = _pad_axis(ukv[:, :, :MLA_NOPE_DIM], 2, HEAD_PAD).reshape(
        MLA_KV_RANK, MLA_HEADS * HEAD_PAD).astype(BF16)
    vt = jnp.transpose(ukv[:, :, MLA_NOPE_DIM:], (1, 2, 0))
    d["w_mla_vt"] = _pad_axis(vt, 1, MLA_VA).reshape(MLA_HEADS * MLA_VA, MLA_KV_RANK).astype(BF16)
    d["w_diff_qk"] = w_in[:, _O_DIFF_QK:_O_DIFF_V].astype(BF16)
    dvt = w_in[:, _O_DIFF_V:_O_SSM_Z].T.reshape(DIFF_HEADS, DIFF_V_DIM, D_MODEL)
    d["w_diff_vt"] = _pad_axis(dvt, 1, DIFF_VA).reshape(DIFF_HEADS * DIFF_VA, D_MODEL).astype(BF16)
    d["diff_lambda"] = p["diff_lambda"][l].astype(F32)
    d["diff_nw"] = jnp.broadcast_to(p["diff_norm_w"][l].astype(F32)[:, None], (DIFF_V_DIM, ATTN_Q_TILE // 2))
    d["w_ssm_x"] = w_in[:, _O_SSM_XBC:_O_SSM_DT].astype(BF16)
    zdt = jnp.concatenate([w_in[:, _O_SSM_Z:_O_SSM_XBC], w_in[:, _O_SSM_DT:_O_GATE]], axis=1)
    d["w_ssm_zdt"] = _pad_axis(zdt, 1, SSM_INNER + LANES).astype(BF16)
    d["ssm_conv_w"] = p["ssm_conv_w"][l].astype(F32)
    d["ssm_conv_b"] = row(p["ssm_conv_b"][l])
    d["ssm_dt_bias"] = _pad_axis(row(p["ssm_dt_bias"][l]), 1, LANES)
    d["ssm_alog"] = jnp.broadcast_to(p["ssm_a_log"][l].astype(F32).reshape(2 * SSM_HEADS, 1), (2 * SSM_HEADS, CHUNK))
    d["ssm_dskip"] = jnp.repeat(p["ssm_d"][l].astype(F32), SSM_HEAD_DIM).reshape(1, SSM_INNER)
    d["ssm_nw"] = row(p["ssm_norm_w"][l])
    d["ret_nw"] = row(p["ret_norm_w"][l])
    d["ret_log_decay"] = p["ret_log_decay"][l]
    d["w_gate"] = w_in[:, _O_GATE:_IN_COLS].astype(BF16)
    d["w_branch"] = p["w_branch"][l].astype(BF16)
    d["w_out"] = p["w_out"][l].astype(BF16)
    d["norm_ffn"] = row(p["norm_ffn_w"][l])
    d["ffn_w_gate"] = p["ffn_w_gate"][l].astype(BF16)
    d["ffn_w_up"] = p["ffn_w_up"][l].astype(BF16)
    d["ffn_conv_w"] = p["ffn_conv_w"][l].astype(F32)
    d["ffn_conv_b"] = row(p["ffn_conv_b"][l])
    d["ffn_w_down"] = p["ffn_w_down"][l].astype(BF16)
    d["final_nw"] = row(p["final_norm_w"])
    return d


def _encoder_layer(x, l, d, ropes):
    rope64, rope_mla = ropes
    rq, rk, rv, rg = _ret_in(x, d["norm_mix"], d["w_ret"], rope64)
    dmat, qdf, qdb, kdf, kdb, cdf, cdb = _ret_tables(d["ret_log_decay"])
    rsf, rsb = _ret_state(rk, rv, kdf, kdb, cdf, cdb)
    ret = _ret_out(rq, rk, rv, rg, rsf, rsb, dmat, qdf, qdb, d["ret_nw"])
    mq, mk, mvt = _mla_in(x, d["norm_mix"], d["w_mla1"], d["mla_qnw"], d["mla_kvnw"], d["w_mla_q"],
                          d["w_mla_k"], d["w_mla_vt"], rope_mla)
    mlat = _mla_attn(mq, mk, mvt)
    dq, dk, dvt = _diff_in(x, d["norm_mix"], d["w_diff_qk"], d["w_diff_vt"], rope64)
    lam_init = 0.8 - 0.6 * math.exp(-0.3 * l)
    difft = _diff_attn(dq, dk, dvt, d["diff_lambda"], d["diff_nw"], lam_init)
    xbc, sz, dt = _ssm_in(x, d["norm_mix"], d["w_ssm_x"], d["w_ssm_zdt"], d["ssm_conv_w"], d["ssm_conv_b"],
                          d["ssm_dt_bias"])
    ssf, ssb = _ssd_state(xbc, dt, d["ssm_alog"])
    ssm = _ssd_out(xbc, dt, sz, ssf, ssb, d["ssm_alog"], d["ssm_dskip"], d["ssm_nw"])
    x = _merge(x, d["norm_mix"], d["w_gate"], ret, mlat, difft, ssm, d["w_branch"], d["w_out"])
    return _ffn(x, d["norm_ffn"], d["ffn_w_gate"], d["ffn_w_up"], d["ffn_conv_w"], d["ffn_conv_b"],
                d["ffn_w_down"], d["final_nw"], final_norm=(l == DEPTH - 1))


def _trunk(x, layers):
    s = x.shape[1]
    ropes = (_rope_tables(s, RET_QK_DIM, RET_QK_DIM, 0),
             _rope_tables(s, MLA_ROPE_DIM, HEAD_PAD, MLA_NOPE_DIM))
    for l in range(DEPTH):
        x = _encoder_layer(x, l, layers[l], ropes)
    return x


def kernel(x_prompt, x_sample, norm_mix_w, w_in, ret_log_decay, ret_norm_w, mla_q_norm_w, mla_w_uq, mla_kv_norm_w, mla_w_ukv, diff_lambda, diff_norm_w, ssm_conv_w, ssm_conv_b, ssm_dt_bias, ssm_a_log, ssm_d, ssm_norm_w, w_branch, w_out, norm_ffn_w, ffn_w_gate, ffn_w_up, ffn_conv_w, ffn_conv_b, ffn_w_down, final_norm_w):
    p = {
        "norm_mix_w": norm_mix_w, "w_in": w_in, "ret_log_decay": ret_log_decay, "ret_norm_w": ret_norm_w,
        "mla_q_norm_w": mla_q_norm_w, "mla_w_uq": mla_w_uq, "mla_kv_norm_w": mla_kv_norm_w,
        "mla_w_ukv": mla_w_ukv, "diff_lambda": diff_lambda, "diff_norm_w": diff_norm_w,
        "ssm_conv_w": ssm_conv_w, "ssm_conv_b": ssm_conv_b, "ssm_dt_bias": ssm_dt_bias,
        "ssm_a_log": ssm_a_log, "ssm_d": ssm_d, "ssm_norm_w": ssm_norm_w, "w_branch": w_branch,
        "w_out": w_out, "norm_ffn_w": norm_ffn_w, "ffn_w_gate": ffn_w_gate, "ffn_w_up": ffn_w_up,
        "ffn_conv_w": ffn_conv_w, "ffn_conv_b": ffn_conv_b, "ffn_w_down": ffn_w_down,
        "final_norm_w": final_norm_w,
    }
    layers = [_layer_params(l, p) for l in range(DEPTH)]
    return _trunk(x_prompt, layers), _trunk(x_sample, layers)
```

```python
import functools
import math

import jax
import jax.numpy as jnp
from jax import lax
from jax.experimental import pallas as pl
from jax.experimental.pallas import tpu as pltpu

F32 = jnp.float32
BF16 = jnp.bfloat16

D_MODEL = 1024
DEPTH = 2
ROPE_THETA = 10000.0
NORM_EPS = 1e-6
HEAD_NORM_EPS = 1e-5
CHUNK = 128
N_BRANCH = 4
BRANCH_WIDTH = 512

RET_HEADS, RET_QK_DIM, RET_V_DIM = 4, 64, 128
MLA_HEADS, MLA_NOPE_DIM, MLA_ROPE_DIM, MLA_V_DIM = 8, 64, 32, 64
MLA_Q_RANK, MLA_KV_RANK = 256, 128
DIFF_HEADS, DIFF_HEAD_DIM = 4, 64
DIFF_V_DIM = 2 * DIFF_HEAD_DIM
SSM_HEADS, SSM_HEAD_DIM, SSM_GROUPS, SSM_STATE = 8, 64, 2, 128
SSM_INNER = SSM_HEADS * SSM_HEAD_DIM
SSM_CONV_CH = SSM_INNER + 2 * SSM_GROUPS * SSM_STATE
D_FF = 2816

_O_RET = 0
_O_MLA_CQ = 1536
_O_MLA_CKV = 1792
_O_MLA_KR = 1920
_O_DIFF_QK = 1952
_O_DIFF_V = 2976
_O_SSM_Z = 3488
_O_SSM_XBC = 4000
_O_SSM_DT = 5024
_O_GATE = 5040
_IN_COLS = 9136

LANES = 128
SUBLANES = 8
HALO = SUBLANES
PACK_ROWS = 16
HEAD_PAD = LANES
MLA_VA = MLA_V_DIM + PACK_ROWS
DIFF_VA = DIFF_V_DIM + PACK_ROWS
TOKEN_TILE = 512
ATTN_Q_TILE = 512
ATTN_K_TILE = 512
LOG2E = 1.4426950408889634
NEG_BIG = -1e30
VMEM_LIMIT = 56 * 1024 * 1024


def _cparams(sems, vmem=VMEM_LIMIT):
    return pltpu.CompilerParams(dimension_semantics=sems, vmem_limit_bytes=vmem)


def _full(shape):
    return pl.BlockSpec(shape, lambda *_: (0,) * len(shape))


def _nt(a, b):
    return lax.dot_general(a, b, (((1,), (1,)), ((), ())), preferred_element_type=F32)


def _dot(a, b):
    return jnp.dot(a, b, preferred_element_type=F32)


def _rms(x, w, eps):
    return x * lax.rsqrt(jnp.mean(x * x, axis=-1, keepdims=True) + eps) * w


def _sigmoid(x):
    return 1.0 / (1.0 + jnp.exp(-x))


def _silu(x):
    return x * _sigmoid(x)


def _rope(x, c, s1, s2, half):
    return x * c + pltpu.roll(x, LANES - half, 1) * s1 + pltpu.roll(x, half, 1) * s2


def _rope_tables(seq, dim, block, off):
    inv_freq = 1.0 / (ROPE_THETA ** (jnp.arange(0, dim, 2, dtype=F32) / dim))
    ang = jnp.arange(seq, dtype=F32)[:, None] * inv_freq[None, :]
    cos, sin = jnp.cos(ang), jnp.sin(ang)
    half = dim // 2
    zero = jnp.zeros_like(sin)
    pre = jnp.zeros((seq, off), F32)
    post = jnp.zeros((seq, block - off - dim), F32)
    c = jnp.concatenate([pre + 1.0, cos, cos, post + 1.0], axis=1)
    s1 = jnp.concatenate([pre, -sin, zero, post], axis=1)
    s2 = jnp.concatenate([pre, zero, sin, post], axis=1)
    reps = LANES // block
    return tuple(jnp.tile(t, (1, reps)) for t in (c, s1, s2))


def _ret_in_kernel(x_ref, nw_ref, w_ref, c_ref, s1_ref, s2_ref, q_ref, k_ref, v_ref, g_ref):
    h = _rms(x_ref[0], nw_ref[...], NORM_EPS).astype(BF16)
    p = _dot(h, w_ref[...])
    c, s1, s2 = c_ref[...], s1_ref[...], s2_ref[...]
    half = RET_QK_DIM // 2
    for i in range(2):
        sl = slice(i * LANES, (i + 1) * LANES)
        q_ref[0, :, sl] = _rope(p[:, sl], c, s1, s2, half).astype(BF16)
        kk = _rope(p[:, 256 + i * LANES:256 + (i + 1) * LANES], c, s1, s2, half)
        k_ref[0, :, sl] = (kk * (RET_QK_DIM ** -0.5)).astype(BF16)
    v_ref[0] = p[:, 512:1024].astype(BF16)
    g_ref[0] = _silu(p[:, 1024:1536]).astype(BF16)


def _ret_in(x, nw, w, tabs):
    b, s, _ = x.shape
    tm = TOKEN_TILE
    tok = lambda width: pl.BlockSpec((1, tm, width), lambda bi, i: (bi, i, 0))
    tab = pl.BlockSpec((tm, LANES), lambda bi, i: (i, 0))
    return pl.pallas_call(
        _ret_in_kernel,
        grid=(b, s // tm),
        in_specs=[tok(D_MODEL), _full((1, D_MODEL)), _full(w.shape), tab, tab, tab],
        out_specs=[tok(256), tok(256), tok(512), tok(512)],
        out_shape=[jax.ShapeDtypeStruct((b, s, n), BF16) for n in (256, 256, 512, 512)],
        compiler_params=_cparams(("parallel", "parallel")),
        name="ret_in",
    )(x, nw, w, *tabs)


def _ones_rows(rows, cols, period, at):
    r = lax.broadcasted_iota(jnp.int32, (rows, cols), 0)
    return jnp.where(r % period == at, 1.0, 0.0).astype(F32)


def _mla_in_kernel(x_ref, nw_ref, w1_ref, qnw_ref, kvnw_ref, wq_ref, wk_ref, wvt_ref,
                   c_ref, s1_ref, s2_ref, q_ref, k_ref, vt_ref):
    h = _rms(x_ref[0], nw_ref[...], NORM_EPS).astype(BF16)
    p = _dot(h, w1_ref[...])
    cqn = _rms(p[:, 0:256], qnw_ref[...], NORM_EPS).astype(BF16)
    ckvn = _rms(p[:, 256:384], kvnw_ref[...], NORM_EPS).astype(BF16)
    c, s1, s2 = c_ref[...], s1_ref[...], s2_ref[...]
    half = MLA_ROPE_DIM // 2
    krr = _rope(p[:, 384:512], c, s1, s2, half)
    qp = _dot(cqn, wq_ref[...])
    kp = _dot(ckvn, wk_ref[...])
    qscale = ((MLA_NOPE_DIM + MLA_ROPE_DIM) ** -0.5) * LOG2E
    for hd in range(MLA_HEADS):
        sl = slice(hd * HEAD_PAD, (hd + 1) * HEAD_PAD)
        q_ref[0, :, sl] = (_rope(qp[:, sl], c, s1, s2, half) * qscale).astype(BF16)
        k_ref[0, :, sl] = (kp[:, sl] + krr).astype(BF16)
    vt = _nt(wvt_ref[...], ckvn)
    vt_ref[0] = (vt + _ones_rows(vt.shape[0], vt.shape[1], MLA_VA, MLA_V_DIM)).astype(BF16)


def _mla_in(x, nw, w1, qnw, kvnw, wq, wk, wvt, tabs):
    b, s, _ = x.shape
    tm = TOKEN_TILE
    tok = lambda width: pl.BlockSpec((1, tm, width), lambda bi, i: (bi, i, 0))
    tab = pl.BlockSpec((tm, LANES), lambda bi, i: (i, 0))
    rows = MLA_HEADS * MLA_VA
    return pl.pallas_call(
        _mla_in_kernel,
        grid=(b, s // tm),
        in_specs=[tok(D_MODEL), _full((1, D_MODEL)), _full(w1.shape), _full(qnw.shape), _full(kvnw.shape),
                  _full(wq.shape), _full(wk.shape), _full(wvt.shape), tab, tab, tab],
        out_specs=[tok(1024), tok(1024), pl.BlockSpec((1, rows, tm), lambda bi, i: (bi, 0, i))],
        out_shape=[jax.ShapeDtypeStruct((b, s, 1024), BF16), jax.ShapeDtypeStruct((b, s, 1024), BF16),
                   jax.ShapeDtypeStruct((b, rows, s), BF16)],
        compiler_params=_cparams(("parallel", "parallel")),
        name="mla_in",
    )(x, nw, w1, qnw, kvnw, wq, wk, wvt, *tabs)


def _diff_in_kernel(x_ref, nw_ref, wqk_ref, wvt_ref, c_ref, s1_ref, s2_ref, q_ref, k_ref, vt_ref):
    h = _rms(x_ref[0], nw_ref[...], NORM_EPS).astype(BF16)
    p = _dot(h, wqk_ref[...])
    c, s1, s2 = c_ref[...], s1_ref[...], s2_ref[...]
    half = DIFF_HEAD_DIM // 2
    qscale = (DIFF_HEAD_DIM ** -0.5) * LOG2E
    for hd in range(DIFF_HEADS):
        sl = slice(hd * HEAD_PAD, (hd + 1) * HEAD_PAD)
        q_ref[0, :, sl] = (_rope(p[:, sl], c, s1, s2, half) * qscale).astype(BF16)
        k_ref[0, :, sl] = _rope(p[:, 512 + hd * HEAD_PAD:512 + (hd + 1) * HEAD_PAD], c, s1, s2, half).astype(BF16)
    vt = _nt(wvt_ref[...], h)
    vt_ref[0] = (vt + _ones_rows(vt.shape[0], vt.shape[1], DIFF_VA, DIFF_V_DIM)).astype(BF16)


def _diff_in(x, nw, wqk, wvt, tabs):
    b, s, _ = x.shape
    tm = TOKEN_TILE
    tok = lambda width: pl.BlockSpec((1, tm, width), lambda bi, i: (bi, i, 0))
    tab = pl.BlockSpec((tm, LANES), lambda bi, i: (i, 0))
    rows = DIFF_HEADS * DIFF_VA
    return pl.pallas_call(
        _diff_in_kernel,
        grid=(b, s // tm),
        in_specs=[tok(D_MODEL), _full((1, D_MODEL)), _full(wqk.shape), _full(wvt.shape), tab, tab, tab],
        out_specs=[tok(512), tok(512), pl.BlockSpec((1, rows, tm), lambda bi, i: (bi, 0, i))],
        out_shape=[jax.ShapeDtypeStruct((b, s, 512), BF16), jax.ShapeDtypeStruct((b, s, 512), BF16),
                   jax.ShapeDtypeStruct((b, rows, s), BF16)],
        compiler_params=_cparams(("parallel", "parallel")),
        name="diff_in",
    )(x, nw, wqk, wvt, *tabs)


def _halo_specs(tm, s, width):
    per = tm // HALO
    last = s // HALO - 1
    main = pl.BlockSpec((1, tm, width), lambda bi, i: (bi, i, 0))
    prev = pl.BlockSpec((1, HALO, width), lambda bi, i: (bi, jnp.maximum(i * per - 1, 0), 0))
    nxt = pl.BlockSpec((1, HALO, width), lambda bi, i: (bi, jnp.minimum((i + 1) * per, last), 0))
    return prev, main, nxt


def _conv3(g_ref, tm, w_ref, b_ref):
    u = w_ref[0:1, :] * g_ref[pl.ds(HALO - 1, tm), :]
    u = u + w_ref[1:2, :] * g_ref[pl.ds(HALO, tm), :]
    u = u + w_ref[2:3, :] * g_ref[pl.ds(HALO + 1, tm), :]
    return u + b_ref[...]


def _zero_halo_at_sequence_ends(g_ref, tm):
    i = pl.program_id(1)

    @pl.when(i == 0)
    def _():
        g_ref[0:HALO, :] = jnp.zeros((HALO, g_ref.shape[1]), F32)

    @pl.when(i == pl.num_programs(1) - 1)
    def _():
        g_ref[HALO + tm:2 * HALO + tm, :] = jnp.zeros((HALO, g_ref.shape[1]), F32)


def _ssm_in_kernel(xp_ref, x_ref, xn_ref, nw_ref, wx_ref, wzd_ref, cw_ref, cb_ref, dtb_ref,
                   xbc_ref, z_ref, dt_ref, g_ref):
    tm = x_ref.shape[1]
    nw = nw_ref[...]
    hm = _rms(x_ref[0], nw, NORM_EPS)
    hcat = jnp.concatenate([_rms(xp_ref[0], nw, NORM_EPS), hm, _rms(xn_ref[0], nw, NORM_EPS)], axis=0)
    g_ref[...] = _dot(hcat.astype(BF16), wx_ref[...])
    _zero_halo_at_sequence_ends(g_ref, tm)
    xbc_ref[0] = _silu(_conv3(g_ref, tm, cw_ref, cb_ref))
    zd = _dot(hm.astype(BF16), wzd_ref[...])
    z_ref[0] = _silu(zd[:, 0:512])
    t = zd[:, 512:640] + dtb_ref[...]
    sp = jnp.maximum(t, 0.0) + jnp.log1p(jnp.exp(-jnp.abs(t)))
    lane = lax.broadcasted_iota(jnp.int32, sp.shape, 1)
    dt_ref[0] = jnp.where(lane < 2 * SSM_HEADS, sp, 0.0)


def _ssm_in(x, nw, wx, wzd, cw, cb, dtb):
    b, s, _ = x.shape
    tm = TOKEN_TILE
    tok = lambda width: pl.BlockSpec((1, tm, width), lambda bi, i: (bi, i, 0))
    return pl.pallas_call(
        _ssm_in_kernel,
        grid=(b, s // tm),
        in_specs=[*_halo_specs(tm, s, D_MODEL), _full((1, D_MODEL)), _full(wx.shape), _full(wzd.shape),
                  _full(cw.shape), _full(cb.shape), _full(dtb.shape)],
        out_specs=[tok(1024), tok(512), tok(LANES)],
        out_shape=[jax.ShapeDtypeStruct((b, s, 1024), F32), jax.ShapeDtypeStruct((b, s, 512), F32),
                   jax.ShapeDtypeStruct((b, s, LANES), F32)],
        scratch_shapes=[pltpu.VMEM((tm + 2 * HALO, SSM_CONV_CH), F32)],
        compiler_params=_cparams(("parallel", "parallel")),
        name="ssm_in",
    )(x, x, x, nw, wx, wzd, cw, cb, dtb)


def _ret_state_kernel(kf_ref, vf_ref, kb_ref, vb_ref, kdf_ref, kdb_ref, cdf_ref, cdb_ref,
                      sf_ref, sb_ref, st_ref):
    @pl.when(pl.program_id(1) == 0)
    def _():
        st_ref[...] = jnp.zeros(st_ref.shape, F32)

    def scan_step(d, k_ref, v_ref, kd_ref, cd_ref, out_ref):
        st = st_ref[d]
        out_ref[0, 0] = st
        kdt = (k_ref[0].astype(F32) * kd_ref[...]).T.astype(BF16)
        v = v_ref[0]
        inc = [_dot(kdt[hd * RET_QK_DIM:(hd + 1) * RET_QK_DIM], v[:, hd * RET_V_DIM:(hd + 1) * RET_V_DIM])
               for hd in range(RET_HEADS)]
        st_ref[d] = cd_ref[...] * st + jnp.concatenate(inc, axis=0)

    scan_step(0, kf_ref, vf_ref, kdf_ref, cdf_ref, sf_ref)
    scan_step(1, kb_ref, vb_ref, kdb_ref, cdb_ref, sb_ref)


def _ret_state(k, v, kdf, kdb, cdf, cdb):
    b, s, _ = k.shape
    n = s // CHUNK
    fwd = lambda width: pl.BlockSpec((1, CHUNK, width), lambda bi, c: (bi, c, 0))
    bwd = lambda width: pl.BlockSpec((1, CHUNK, width), lambda bi, c: (bi, n - 1 - c, 0))
    rows = RET_HEADS * RET_QK_DIM
    st_shape = jax.ShapeDtypeStruct((b, n, rows, RET_V_DIM), F32)
    return pl.pallas_call(
        _ret_state_kernel,
        grid=(b, n),
        in_specs=[fwd(256), fwd(512), bwd(256), bwd(512), _full(kdf.shape), _full(kdb.shape),
                  _full(cdf.shape), _full(cdb.shape)],
        out_specs=[pl.BlockSpec((1, 1, rows, RET_V_DIM), lambda bi, c: (bi, c, 0, 0)),
                   pl.BlockSpec((1, 1, rows, RET_V_DIM), lambda bi, c: (bi, n - 1 - c, 0, 0))],
        out_shape=[st_shape, st_shape],
        scratch_shapes=[pltpu.VMEM((2, rows, RET_V_DIM), F32)],
        compiler_params=_cparams(("parallel", "arbitrary")),
        name="ret_state",
    )(k, v, k, v, kdf, kdb, cdf, cdb)


def _ret_out_kernel(q_ref, k_ref, v_ref, g_ref, sf_ref, sb_ref, d_ref, qdf_ref, qdb_ref, nw_ref, o_ref):
    q = q_ref[0].astype(F32)
    qf = q * qdf_ref[...]
    qb = q * qdb_ref[...]
    k = k_ref[0]
    v = v_ref[0]
    sf = sf_ref[0, 0].astype(BF16)
    sb = sb_ref[0, 0].astype(BF16)
    lane = lax.broadcasted_iota(jnp.int32, (CHUNK, LANES), 1)
    for hd in range(RET_HEADS):
        pair = slice((hd // 2) * LANES, (hd // 2 + 1) * LANES)
        own = (lane < RET_QK_DIM) if hd % 2 == 0 else (lane >= RET_QK_DIM)
        pick = lambda t: jnp.where(own, t[:, pair], 0.0).astype(BF16)
        sc = _nt(pick(q), k[:, pair]) * d_ref[hd]
        vsl = slice(hd * RET_V_DIM, (hd + 1) * RET_V_DIM)
        y = _dot(sc.astype(BF16), v[:, vsl]) + _dot(pick(qf), sf[pair, :]) + _dot(pick(qb), sb[pair, :])
        y = _rms(y, nw_ref[:, vsl], HEAD_NORM_EPS)
        o_ref[0, :, vsl] = (g_ref[0, :, vsl].astype(F32) * y).astype(BF16)


def _ret_out(q, k, v, g, sf, sb, dmat, qdf, qdb, nw):
    b, s, _ = q.shape
    n = s // CHUNK
    tok = lambda width: pl.BlockSpec((1, CHUNK, width), lambda bi, c: (bi, c, 0))
    st = pl.BlockSpec((1, 1) + sf.shape[2:], lambda bi, c: (bi, c, 0, 0))
    return pl.pallas_call(
        _ret_out_kernel,
        grid=(b, n),
        in_specs=[tok(256), tok(256), tok(512), tok(512), st, st, _full(dmat.shape), _full(qdf.shape),
                  _full(qdb.shape), _full(nw.shape)],
        out_specs=tok(512),
        out_shape=jax.ShapeDtypeStruct((b, s, 512), BF16),
        compiler_params=_cparams(("parallel", "parallel")),
        name="ret_out",
    )(q, k, v, g, sf, sb, dmat, qdf, qdb, nw)


def _ret_tables(log_decay):
    lg_f = log_decay[0].astype(F32)
    lg_b = log_decay[1].astype(F32)
    pos = jnp.arange(CHUNK, dtype=F32)
    rel = pos[:, None] - pos[None, :]
    low = jnp.where(rel >= 0, jnp.exp(jnp.maximum(rel, 0.0)[None] * lg_f[:, None, None]), 0.0)
    upp = jnp.where(rel <= 0, jnp.exp(jnp.maximum(-rel, 0.0)[None] * lg_b[:, None, None]), 0.0)
    dmat = low + upp
    wide = lambda t: jnp.repeat(t, RET_QK_DIM, axis=1)
    qdf = wide(jnp.exp((pos + 1.0)[:, None] * lg_f))
    qdb = wide(jnp.exp((CHUNK - pos)[:, None] * lg_b))
    kdf = wide(jnp.exp((CHUNK - 1.0 - pos)[:, None] * lg_f))
    kdb = wide(jnp.exp(pos[:, None] * lg_b))
    tall = lambda lg: jnp.broadcast_to(jnp.repeat(jnp.exp(CHUNK * lg), RET_QK_DIM)[:, None],
                                       (RET_HEADS * RET_QK_DIM, RET_V_DIM))
    return dmat, qdf, qdb, kdf, kdb, tall(lg_f), tall(lg_b)


def _cumsum_lanes(x, reverse):
    lane = lax.broadcasted_iota(jnp.int32, x.shape, 1)
    for step in range(7):
        sh = 1 << step
        if reverse:
            x = x + jnp.where(lane < LANES - sh, pltpu.roll(x, LANES - sh, 1), 0.0)
        else:
            x = x + jnp.where(lane >= sh, pltpu.roll(x, sh, 1), 0.0)
    return x


def _ssd_state_kernel(xf_ref, dtf_ref, xb_ref, dtb_ref, alog_ref, sf_ref, sb_ref, st_ref):
    @pl.when(pl.program_id(1) == 0)
    def _():
        st_ref[...] = jnp.zeros(st_ref.shape, F32)

    head_of_lane = lax.broadcasted_iota(jnp.int32, (1, SSM_INNER), 1) // SSM_HEAD_DIM

    def scan_step(d, x_ref, dt_ref, out_ref):
        st = st_ref[d]
        out_ref[0, 0] = st
        r0 = d * SSM_HEADS
        dtt = dt_ref[0].T[r0:r0 + SSM_HEADS]
        a = -jnp.exp(alog_ref[r0:r0 + SSM_HEADS, :])
        cum = _cumsum_lanes(dtt * a, reverse=(d == 1))
        end = cum[:, 0:1] if d == 1 else cum[:, CHUNK - 1:CHUNK]
        endb = jnp.broadcast_to(end, cum.shape)
        wt = jnp.exp(endb - cum) * dtt
        edec = jnp.exp(endb)
        xbc = x_ref[0]
        xt = xbc[:, 0:SSM_INNER].T
        xwt = jnp.concatenate(
            [xt[hd * SSM_HEAD_DIM:(hd + 1) * SSM_HEAD_DIM] * wt[hd:hd + 1] for hd in range(SSM_HEADS)],
            axis=0).astype(BF16)
        inc = []
        per_group = SSM_INNER // SSM_GROUPS
        for g in range(SSM_GROUPS):
            bt = xbc[:, SSM_INNER + g * SSM_STATE:SSM_INNER + (g + 1) * SSM_STATE].T.astype(BF16)
            inc.append(_nt(bt, xwt[g * per_group:(g + 1) * per_group]))
        dec = jnp.zeros((1, SSM_INNER), F32)
        for hd in range(SSM_HEADS):
            row = jnp.concatenate([edec[hd:hd + 1]] * (SSM_INNER // LANES), axis=1)
            dec = jnp.where(head_of_lane == hd, row, dec)
        st_ref[d] = st * dec + jnp.concatenate(inc, axis=1)

    scan_step(0, xf_ref, dtf_ref, sf_ref)
    scan_step(1, xb_ref, dtb_ref, sb_ref)


def _ssd_state(xbc, dt, alog):
    b, s, _ = xbc.shape
    n = s // CHUNK
    fwd = lambda width: pl.BlockSpec((1, CHUNK, width), lambda bi, c: (bi, c, 0))
    bwd = lambda width: pl.BlockSpec((1, CHUNK, width), lambda bi, c: (bi, n - 1 - c, 0))
    st_shape = jax.ShapeDtypeStruct((b, n, SSM_STATE, SSM_INNER), F32)
    return pl.pallas_call(
        _ssd_state_kernel,
        grid=(b, n),
        in_specs=[fwd(768), fwd(LANES), bwd(768), bwd(LANES), _full(alog.shape)],
        out_specs=[pl.BlockSpec((1, 1, SSM_STATE, SSM_INNER), lambda bi, c: (bi, c, 0, 0)),
                   pl.BlockSpec((1, 1, SSM_STATE, SSM_INNER), lambda bi, c: (bi, n - 1 - c, 0, 0))],
        out_shape=[st_shape, st_shape],
        scratch_shapes=[pltpu.VMEM((2, SSM_STATE, SSM_INNER), F32)],
        compiler_params=_cparams(("parallel", "arbitrary")),
        name="ssd_state",
    )(xbc, dt, xbc, dt, alog)


def _ssd_out_kernel(x_ref, dt_ref, z_ref, sf_ref, sb_ref, alog_ref, dskip_ref, nw_ref, o_ref):
    xbc = x_ref[0]
    nh = SSM_HEADS
    dtt = dt_ref[0].T[0:2 * nh]
    dat = dtt * (-jnp.exp(alog_ref[...]))
    cumt = jnp.concatenate([_cumsum_lanes(dat[0:nh], False), _cumsum_lanes(dat[nh:2 * nh], True)], axis=0)
    cum = jnp.concatenate([cumt, jnp.zeros((LANES - 2 * nh, CHUNK), F32)], axis=0).T
    x = xbc[:, 0:SSM_INNER]
    xb = x.astype(BF16)
    sf = sf_ref[0, 0].astype(BF16)
    sb = sb_ref[0, 0].astype(BF16)
    ii = lax.broadcasted_iota(jnp.int32, (CHUNK, CHUNK), 0)
    jj = lax.broadcasted_iota(jnp.int32, (CHUNK, CHUNK), 1)
    low = ii >= jj
    upp = jj >= ii
    first = jj < SSM_HEAD_DIM
    heads_per_group = nh // SSM_GROUPS
    ys = []
    for pr in range(nh // 2):
        g = (2 * pr) // heads_per_group
        cg = xbc[:, SSM_INNER + SSM_GROUPS * SSM_STATE + g * SSM_STATE:
                 SSM_INNER + SSM_GROUPS * SSM_STATE + (g + 1) * SSM_STATE].astype(BF16)
        bg = xbc[:, SSM_INNER + g * SSM_STATE:SSM_INNER + (g + 1) * SSM_STATE].astype(BF16)
        cb = _nt(cg, bg)
        pair = slice(pr * LANES, (pr + 1) * LANES)
        diag = []
        ecol = []
        for hd in (2 * pr, 2 * pr + 1):
            colf = jnp.broadcast_to(cum[:, hd:hd + 1], (CHUNK, CHUNK))
            colb = jnp.broadcast_to(cum[:, nh + hd:nh + hd + 1], (CHUNK, CHUNK))
            segf = jnp.exp(jnp.where(low, colf - cumt[hd:hd + 1], NEG_BIG))
            segb = jnp.exp(jnp.where(upp, colb - cumt[nh + hd:nh + hd + 1], NEG_BIG))
            w = cb * (segf * dtt[hd:hd + 1] + segb * dtt[nh + hd:nh + hd + 1])
            diag.append(_dot(w.astype(BF16), xb[:, pair]))
            ecol.append((jnp.exp(colf), jnp.exp(colb)))
        yd = jnp.where(first, diag[0], diag[1])
        ef = jnp.where(first, ecol[0][0], ecol[1][0])
        eb = jnp.where(first, ecol[0][1], ecol[1][1])
        ys.append(yd + ef * _dot(cg, sf[:, pair]) + eb * _dot(cg, sb[:, pair]))
    y = jnp.concatenate(ys, axis=1) + dskip_ref[...] * x
    y = y * z_ref[0]
    gw = SSM_INNER // SSM_GROUPS
    for g in range(SSM_GROUPS):
        sl = slice(g * gw, (g + 1) * gw)
        o_ref[0, :, sl] = _rms(y[:, sl], nw_ref[:, sl], HEAD_NORM_EPS).astype(BF16)


def _ssd_out(xbc, dt, z, sf, sb, alog, dskip, nw):
    b, s, _ = xbc.shape
    n = s // CHUNK
    tok = lambda width: pl.BlockSpec((1, CHUNK, width), lambda bi, c: (bi, c, 0))
    st = pl.BlockSpec((1, 1, SSM_STATE, SSM_INNER), lambda bi, c: (bi, c, 0, 0))
    return pl.pallas_call(
        _ssd_out_kernel,
        grid=(b, n),
        in_specs=[tok(1024), tok(LANES), tok(512), st, st, _full(alog.shape), _full(dskip.shape), _full(nw.shape)],
        out_specs=tok(512),
        out_shape=jax.ShapeDtypeStruct((b, s, 512), BF16),
        compiler_params=_cparams(("parallel", "parallel")),
        name="ssd_out",
    )(xbc, dt, z, sf, sb, alog, dskip, nw)


def _attn_scores_t(q, k_ref, vt_ref, seq):
    tk = min(ATTN_K_TILE, seq)
    rows = vt_ref.shape[1]

    m = jnp.full((1, ATTN_Q_TILE), NEG_BIG, F32)
    acc = jnp.zeros((rows, ATTN_Q_TILE), F32)
    n = seq // tk
    scores = lambda c: _nt(k_ref[0, c * tk:(c + 1) * tk, :], q)
    s_next = scores(0)
    for c in range(n):
        s = s_next
        if c + 1 < n:
            s_next = scores(c + 1)
        m_new = jnp.maximum(m, jnp.max(s, axis=0, keepdims=True))
        p = jnp.exp2(s - m_new).astype(BF16)
        acc = acc * jnp.exp2(m - m_new) + _dot(vt_ref[0, :, c * tk:(c + 1) * tk], p)
        m = m_new
    return acc


def _mla_attn_kernel(q_ref, k_ref, vt_ref, o_ref):
    acc = _attn_scores_t(q_ref[0], k_ref, vt_ref, k_ref.shape[1])
    o_ref[0] = (acc[0:MLA_V_DIM] / acc[MLA_V_DIM:MLA_V_DIM + 1]).astype(BF16)


def _mla_attn(q, k, vt):
    b, s, _ = q.shape
    tq = ATTN_Q_TILE
    return pl.pallas_call(
        _mla_attn_kernel,
        grid=(b, MLA_HEADS, s // tq),
        in_specs=[pl.BlockSpec((1, tq, HEAD_PAD), lambda bi, h, i: (bi, i, h)),
                  pl.BlockSpec((1, s, HEAD_PAD), lambda bi, h, i: (bi, 0, h)),
                  pl.BlockSpec((1, MLA_VA, s), lambda bi, h, i: (bi, h, 0))],
        out_specs=pl.BlockSpec((1, MLA_V_DIM, tq), lambda bi, h, i: (bi, h, i)),
        out_shape=jax.ShapeDtypeStruct((b, MLA_HEADS * MLA_V_DIM, s), BF16),
        compiler_params=_cparams(("parallel", "parallel", "parallel")),
        name="mla_attn",
    )(q, k, vt)


def _diff_attn_kernel(q_ref, k_ref, vt_ref, lam_ref, nw_ref, o_ref, *, lam_init):
    tq = ATTN_Q_TILE // 2
    q = q_ref[0].astype(F32)
    lane = lax.broadcasted_iota(jnp.int32, q.shape, 1)
    first = lane < DIFF_HEAD_DIM
    both = jnp.concatenate([jnp.where(first, q, 0.0), jnp.where(first, 0.0, q)], axis=0).astype(BF16)
    acc = _attn_scores_t(both, k_ref, vt_ref, k_ref.shape[1])
    dv = DIFF_V_DIM
    o1 = acc[0:dv, 0:tq] / acc[dv:dv + 1, 0:tq]
    o2 = acc[0:dv, tq:2 * tq] / acc[dv:dv + 1, tq:2 * tq]
    lv = lam_ref[...]
    lam = (jnp.exp(jnp.sum(lv[0:1] * lv[1:2], axis=1, keepdims=True))
           - jnp.exp(jnp.sum(lv[2:3] * lv[3:4], axis=1, keepdims=True)) + lam_init)
    o = o1 - lam * o2
    o = o * lax.rsqrt(jnp.mean(o * o, axis=0, keepdims=True) + HEAD_NORM_EPS) * nw_ref[...]
    o_ref[0] = (o * (1.0 - lam_init)).astype(BF16)


def _diff_attn(q, k, vt, lam_vec, nw, lam_init):
    b, s, _ = q.shape
    tq = ATTN_Q_TILE // 2
    return pl.pallas_call(
        functools.partial(_diff_attn_kernel, lam_init=lam_init),
        grid=(b, DIFF_HEADS, s // tq),
        in_specs=[pl.BlockSpec((1, tq, HEAD_PAD), lambda bi, h, i: (bi, i, h)),
                  pl.BlockSpec((1, s, HEAD_PAD), lambda bi, h, i: (bi, 0, h)),
                  pl.BlockSpec((1, DIFF_VA, s), lambda bi, h, i: (bi, h, 0)),
                  _full(lam_vec.shape), _full(nw.shape)],
        out_specs=pl.BlockSpec((1, DIFF_V_DIM, tq), lambda bi, h, i: (bi, h, i)),
        out_shape=jax.ShapeDtypeStruct((b, DIFF_HEADS * DIFF_V_DIM, s), BF16),
        compiler_params=_cparams(("parallel", "parallel", "parallel")),
        name="diff_attn",
    )(q, k, vt, lam_vec, nw)


def _merge_kernel(x_ref, nw_ref, wg_ref, ret_ref, mlat_ref, difft_ref, ssm_ref, wb_ref, wo_ref, o_ref):
    x = x_ref[0]
    h = _rms(x, nw_ref[...], NORM_EPS).astype(BF16)
    rows = lambda t_ref: t_ref[0].astype(F32).T.astype(BF16)
    outs = (ret_ref[0], rows(mlat_ref), rows(difft_ref), ssm_ref[0])
    merged = None
    for i in range(N_BRANCH):
        gate = _sigmoid(_dot(h, wg_ref[:, i * D_MODEL:(i + 1) * D_MODEL]))
        term = gate * _dot(outs[i], wb_ref[i])
        merged = term if merged is None else merged + term
    o_ref[0] = x + _dot(merged.astype(BF16), wo_ref[...])


def _merge(x, nw, wg, ret, mlat, difft, ssm, wb, wo):
    b, s, _ = x.shape
    tm = TOKEN_TILE
    tok = lambda width: pl.BlockSpec((1, tm, width), lambda bi, i: (bi, i, 0))
    tr = pl.BlockSpec((1, BRANCH_WIDTH, tm), lambda bi, i: (bi, 0, i))
    return pl.pallas_call(
        _merge_kernel,
        grid=(b, s // tm),
        in_specs=[tok(D_MODEL), _full((1, D_MODEL)), _full(wg.shape), tok(512), tr, tr, tok(512),
                  _full(wb.shape), _full(wo.shape)],
        out_specs=tok(D_MODEL),
        out_shape=jax.ShapeDtypeStruct((b, s, D_MODEL), F32),
        compiler_params=_cparams(("parallel", "parallel")),
        name="merge",
    )(x, nw, wg, ret, mlat, difft, ssm, wb, wo)


FFN_COL_TILE = 1408


def _ffn_kernel(xp_ref, x_ref, xn_ref, nw_ref, wgate_ref, wup_ref, cw_ref, cb_ref, wdown_ref, fnw_ref,
                o_ref, g_ref, *, final_norm):
    tm = x_ref.shape[1]
    nw = nw_ref[...]
    x = x_ref[0]
    hm = _rms(x, nw, NORM_EPS)
    hcat = jnp.concatenate([_rms(xp_ref[0], nw, NORM_EPS), hm, _rms(xn_ref[0], nw, NORM_EPS)],
                           axis=0).astype(BF16)
    hb = hm.astype(BF16)
    acc = x
    for j in range(D_FF // FFN_COL_TILE):
        cols = slice(j * FFN_COL_TILE, (j + 1) * FFN_COL_TILE)
        g_ref[...] = _dot(hcat, wgate_ref[:, cols])
        _zero_halo_at_sequence_ends(g_ref, tm)
        u = _conv3(g_ref, tm, cw_ref.at[:, cols], cb_ref.at[:, cols])
        act = (_silu(u) * _dot(hb, wup_ref[:, cols])).astype(BF16)
        acc = acc + _dot(act, wdown_ref[cols, :])
    if final_norm:
        acc = _rms(acc, fnw_ref[...], NORM_EPS)
    o_ref[0] = acc


def _ffn(x, nw, wgate, wup, cw, cb, wdown, fnw, final_norm):
    b, s, _ = x.shape
    tm = TOKEN_TILE
    return pl.pallas_call(
        functools.partial(_ffn_kernel, final_norm=final_norm),
        grid=(b, s // tm),
        in_specs=[*_halo_specs(tm, s, D_MODEL), _full((1, D_MODEL)), _full(wgate.shape), _full(wup.shape),
                  _full(cw.shape), _full(cb.shape), _full(wdown.shape), _full((1, D_MODEL))],
        out_specs=pl.BlockSpec((1, tm, D_MODEL), lambda bi, i: (bi, i, 0)),
        out_shape=jax.ShapeDtypeStruct((b, s, D_MODEL), F32),
        scratch_shapes=[pltpu.VMEM((tm + 2 * HALO, FFN_COL_TILE), F32)],
        compiler_params=_cparams(("parallel", "parallel")),
        name="ffn",
    )(x, x, x, nw, wgate, wup, cw, cb, wdown, fnw)


def _pad_axis(t, axis, size):
    pad = [(0, 0)] * t.ndim
    pad[axis] = (0, size - t.shape[axis])
    return jnp.pad(t, pad)


def _layer_params(l, p):
    w_in = p["w_in"][l]
    row = lambda t: t.reshape(1, -1).astype(F32)
    d = {}
    d["norm_mix"] = row(p["norm_mix_w"][l])
    d["w_ret"] = w_in[:, _O_RET:_O_MLA_CQ].astype(BF16)
    kr = jnp.pad(w_in[:, _O_MLA_KR:_O_DIFF_QK], ((0, 0), (MLA_NOPE_DIM, HEAD_PAD - MLA_NOPE_DIM - MLA_ROPE_DIM)))
    d["w_mla1"] = jnp.concatenate([w_in[:, _O_MLA_CQ:_O_MLA_KR], kr], axis=1).astype(BF16)
    d["mla_qnw"] = row(p["mla_q_norm_w"][l])
    d["mla_kvnw"] = row(p["mla_kv_norm_w"][l])
    uq = p["mla_w_uq"][l].reshape(MLA_Q_RANK, MLA_HEADS, MLA_NOPE_DIM + MLA_ROPE_DIM)
    d["w_mla_q"] = _pad_axis(uq, 2, HEAD_PAD).reshape(MLA_Q_RANK, MLA_HEADS * HEAD_PAD).astype(BF16)
    ukv = p["mla_w_ukv"][l].reshape(MLA_KV_RANK, MLA_HEADS, MLA_NOPE_DIM + MLA_V_DIM)
    d["w_mla_k"] = _pad_axis(ukv[:, :, :MLA_NOPE_DIM], 2, HEAD_PAD).reshape(
        MLA_KV_RANK, MLA_HEADS * HEAD_PAD).astype(BF16)
    vt = jnp.transpose(ukv[:, :, MLA_NOPE_DIM:], (1, 2, 0))
    d["w_mla_vt"] = _pad_axis(vt, 1, MLA_VA).reshape(MLA_HEADS * MLA_VA, MLA_KV_RANK).astype(BF16)
    d["w_diff_qk"] = w_in[:, _O_DIFF_QK:_O_DIFF_V].astype(BF16)
    dvt = w_in[:, _O_DIFF_V:_O_SSM_Z].T.reshape(DIFF_HEADS, DIFF_V_DIM, D_MODEL)
    d["w_diff_vt"] = _pad_axis(dvt, 1, DIFF_VA).reshape(DIFF_HEADS * DIFF_VA, D_MODEL).astype(BF16)
    d["diff_lambda"] = p["diff_lambda"][l].astype(F32)
    d["diff_nw"] = jnp.broadcast_to(p["diff_norm_w"][l].astype(F32)[:, None], (DIFF_V_DIM, ATTN_Q_TILE // 2))
    d["w_ssm_x"] = w_in[:, _O_SSM_XBC:_O_SSM_DT].astype(BF16)
    zdt = jnp.concatenate([w_in[:, _O_SSM_Z:_O_SSM_XBC], w_in[:, _O_SSM_DT:_O_GATE]], axis=1)
    d["w_ssm_zdt"] = _pad_axis(zdt, 1, SSM_INNER + LANES).astype(BF16)
    d["ssm_conv_w"] = p["ssm_conv_w"][l].astype(F32)
    d["ssm_conv_b"] = row(p["ssm_conv_b"][l])
    d["ssm_dt_bias"] = _pad_axis(row(p["ssm_dt_bias"][l]), 1, LANES)
    d["ssm_alog"] = jnp.broadcast_to(p["ssm_a_log"][l].astype(F32).reshape(2 * SSM_HEADS, 1), (2 * SSM_HEADS, CHUNK))
    d["ssm_dskip"] = jnp.repeat(p["ssm_d"][l].astype(F32), SSM_HEAD_DIM).reshape(1, SSM_INNER)
    d["ssm_nw"] = row(p["ssm_norm_w"][l])
    d["ret_nw"] = row(p["ret_norm_w"][l])
    d["ret_log_decay"] = p["ret_log_decay"][l]
    d["w_gate"] = w_in[:, _O_GATE:_IN_COLS].astype(BF16)
    d["w_branch"] = p["w_branch"][l].astype(BF16)
    d["w_out"] = p["w_out"][l].astype(BF16)
    d["norm_ffn"] = row(p["norm_ffn_w"][l])
    d["ffn_w_gate"] = p["ffn_w_gate"][l].astype(BF16)
    d["ffn_w_up"] = p["ffn_w_up"][l].astype(BF16)
    d["ffn_conv_w"] = p["ffn_conv_w"][l].astype(F32)
    d["ffn_conv_b"] = row(p["ffn_conv_b"][l])
    d["ffn_w_down"] = p["ffn_w_down"][l].astype(BF16)
    d["final_nw"] = row(p["final_norm_w"])
    return d


def _encoder_layer(x, l, d, ropes):
    rope64, rope_mla = ropes
    rq, rk, rv, rg = _ret_in(x, d["norm_mix"], d["w_ret"], rope64)
    dmat, qdf, qdb, kdf, kdb, cdf, cdb = _ret_tables(d["ret_log_decay"])
    rsf, rsb = _ret_state(rk, rv, kdf, kdb, cdf, cdb)
    ret = _ret_out(rq, rk, rv, rg, rsf, rsb, dmat, qdf, qdb, d["ret_nw"])
    mq, mk, mvt = _mla_in(x, d["norm_mix"], d["w_mla1"], d["mla_qnw"], d["mla_kvnw"], d["w_mla_q"],
                          d["w_mla_k"], d["w_mla_vt"], rope_mla)
    mlat = _mla_attn(mq, mk, mvt)
    dq, dk, dvt = _diff_in(x, d["norm_mix"], d["w_diff_qk"], d["w_diff_vt"], rope64)
    lam_init = 0.8 - 0.6 * math.exp(-0.3 * l)
    difft = _diff_attn(dq, dk, dvt, d["diff_lambda"], d["diff_nw"], lam_init)
    xbc, sz, dt = _ssm_in(x, d["norm_mix"], d["w_ssm_x"], d["w_ssm_zdt"], d["ssm_conv_w"], d["ssm_conv_b"],
                          d["ssm_dt_bias"])
    ssf, ssb = _ssd_state(xbc, dt, d["ssm_alog"])
    ssm = _ssd_out(xbc, dt, sz, ssf, ssb, d["ssm_alog"], d["ssm_dskip"], d["ssm_nw"])
    x = _merge(x, d["norm_mix"], d["w_gate"], ret, mlat, difft, ssm, d["w_branch"], d["w_out"])
    return _ffn(x, d["norm_ffn"], d["ffn_w_gate"], d["ffn_w_up"], d["ffn_conv_w"], d["ffn_conv_b"],
                d["ffn_w_down"], d["final_nw"], final_norm=(l == DEPTH - 1))


def _trunk(x, layers):
    s = x.shape[1]
    ropes = (_rope_tables(s, RET_QK_DIM, RET_QK_DIM, 0),
             _rope_tables(s, MLA_ROPE_DIM, HEAD_PAD, MLA_NOPE_DIM))
    for l in range(DEPTH):
        x = _encoder_layer(x, l, layers[l], ropes)
    return x


def kernel(x_prompt, x_sample, norm_mix_w, w_in, ret_log_decay, ret_norm_w, mla_q_norm_w, mla_w_uq, mla_kv_norm_w, mla_w_ukv, diff_lambda, diff_norm_w, ssm_conv_w, ssm_conv_b, ssm_dt_bias, ssm_a_log, ssm_d, ssm_norm_w, w_branch, w_out, norm_ffn_w, ffn_w_gate, ffn_w_up, ffn_conv_w, ffn_conv_b, ffn_w_down, final_norm_w):
    p = {
        "norm_mix_w": norm_mix_w, "w_in": w_in, "ret_log_decay": ret_log_decay, "ret_norm_w": ret_norm_w,
        "mla_q_norm_w": mla_q_norm_w, "mla_w_uq": mla_w_uq, "mla_kv_norm_w": mla_kv_norm_w,
        "mla_w_ukv": mla_w_ukv, "diff_lambda": diff_lambda, "diff_norm_w": diff_norm_w,
        "ssm_conv_w": ssm_conv_w, "ssm_conv_b": ssm_conv_b, "ssm_dt_bias": ssm_dt_bias,
        "ssm_a_log": ssm_a_log, "ssm_d": ssm_d, "ssm_norm_w": ssm_norm_w, "w_branch": w_branch,
        "w_out": w_out, "norm_ffn_w": norm_ffn_w, "ffn_w_gate": ffn_w_gate, "ffn_w_up": ffn_w_up,
        "ffn_conv_w": ffn_conv_w, "ffn_conv_b": ffn_conv_b, "ffn_w_down": ffn_w_down,
        "final_norm_w": final_norm_w,
    }
    layers = [_layer_params(l, p) for l in range(DEPTH)]
    return _trunk(x_prompt, layers), _trunk(x_sample, layers)
```

```python
import functools
import math

import jax
import jax.numpy as jnp
from jax import lax
from jax.experimental import pallas as pl
from jax.experimental.pallas import tpu as pltpu

F32 = jnp.float32
BF16 = jnp.bfloat16

D_MODEL = 1024
DEPTH = 2
ROPE_THETA = 10000.0
NORM_EPS = 1e-6
HEAD_NORM_EPS = 1e-5
CHUNK = 128
N_BRANCH = 4
BRANCH_WIDTH = 512

RET_HEADS, RET_QK_DIM, RET_V_DIM = 4, 64, 128
MLA_HEADS, MLA_NOPE_DIM, MLA_ROPE_DIM, MLA_V_DIM = 8, 64, 32, 64
MLA_Q_RANK, MLA_KV_RANK = 256, 128
DIFF_HEADS, DIFF_HEAD_DIM = 4, 64
DIFF_V_DIM = 2 * DIFF_HEAD_DIM
SSM_HEADS, SSM_HEAD_DIM, SSM_GROUPS, SSM_STATE = 8, 64, 2, 128
SSM_INNER = SSM_HEADS * SSM_HEAD_DIM
SSM_CONV_CH = SSM_INNER + 2 * SSM_GROUPS * SSM_STATE
D_FF = 2816

_O_RET = 0
_O_MLA_CQ = 1536
_O_MLA_CKV = 1792
_O_MLA_KR = 1920
_O_DIFF_QK = 1952
_O_DIFF_V = 2976
_O_SSM_Z = 3488
_O_SSM_XBC = 4000
_O_SSM_DT = 5024
_O_GATE = 5040
_IN_COLS = 9136

LANES = 128
SUBLANES = 8
HALO = SUBLANES
PACK_ROWS = 16
HEAD_PAD = LANES
MLA_VA = MLA_V_DIM + PACK_ROWS
DIFF_VA = DIFF_V_DIM + PACK_ROWS
TOKEN_TILE = 512
CHUNKS_PER_STEP = 4
ATTN_Q_TILE = 512
ATTN_K_TILE = 256
ATTN_LOOKAHEAD = 3
ATTN_TILES_PER_STEP = 2
LOG2E = 1.4426950408889634
NEG_BIG = -1e30
VMEM_LIMIT = 56 * 1024 * 1024


def _cparams(sems, vmem=VMEM_LIMIT):
    return pltpu.CompilerParams(dimension_semantics=sems, vmem_limit_bytes=vmem)


def _full(shape):
    return pl.BlockSpec(shape, lambda *_: (0,) * len(shape))


def _nt(a, b):
    return lax.dot_general(a, b, (((1,), (1,)), ((), ())), preferred_element_type=F32)


def _dot(a, b):
    return jnp.dot(a, b, preferred_element_type=F32)


def _rms(x, w, eps):
    return x * lax.rsqrt(jnp.mean(x * x, axis=-1, keepdims=True) + eps) * w


def _sigmoid(x):
    return 1.0 / (1.0 + jnp.exp(-x))


def _silu(x):
    return x * _sigmoid(x)


def _rope(x, c, s1, s2, half):
    return x * c + pltpu.roll(x, LANES - half, 1) * s1 + pltpu.roll(x, half, 1) * s2


def _rope_tables(seq, dim, block, off):
    inv_freq = 1.0 / (ROPE_THETA ** (jnp.arange(0, dim, 2, dtype=F32) / dim))
    ang = jnp.arange(seq, dtype=F32)[:, None] * inv_freq[None, :]
    cos, sin = jnp.cos(ang), jnp.sin(ang)
    half = dim // 2
    zero = jnp.zeros_like(sin)
    pre = jnp.zeros((seq, off), F32)
    post = jnp.zeros((seq, block - off - dim), F32)
    c = jnp.concatenate([pre + 1.0, cos, cos, post + 1.0], axis=1)
    s1 = jnp.concatenate([pre, -sin, zero, post], axis=1)
    s2 = jnp.concatenate([pre, zero, sin, post], axis=1)
    reps = LANES // block
    return tuple(jnp.tile(t, (1, reps)) for t in (c, s1, s2))


def _ret_in_kernel(x_ref, nw_ref, w_ref, c_ref, s1_ref, s2_ref, q_ref, k_ref, v_ref, g_ref):
    h = _rms(x_ref[0], nw_ref[...], NORM_EPS).astype(BF16)
    p = _dot(h, w_ref[...])
    c, s1, s2 = c_ref[...], s1_ref[...], s2_ref[...]
    half = RET_QK_DIM // 2
    for i in range(2):
        sl = slice(i * LANES, (i + 1) * LANES)
        q_ref[0, :, sl] = _rope(p[:, sl], c, s1, s2, half).astype(BF16)
        kk = _rope(p[:, 256 + i * LANES:256 + (i + 1) * LANES], c, s1, s2, half)
        k_ref[0, :, sl] = (kk * (RET_QK_DIM ** -0.5)).astype(BF16)
    v_ref[0] = p[:, 512:1024].astype(BF16)
    g_ref[0] = _silu(p[:, 1024:1536]).astype(BF16)


def _ret_in(x, nw, w, tabs):
    b, s, _ = x.shape
    tm = TOKEN_TILE
    tok = lambda width: pl.BlockSpec((1, tm, width), lambda bi, i: (bi, i, 0))
    tab = pl.BlockSpec((tm, LANES), lambda bi, i: (i, 0))
    return pl.pallas_call(
        _ret_in_kernel,
        grid=(b, s // tm),
        in_specs=[tok(D_MODEL), _full((1, D_MODEL)), _full(w.shape), tab, tab, tab],
        out_specs=[tok(256), tok(256), tok(512), tok(512)],
        out_shape=[jax.ShapeDtypeStruct((b, s, n), BF16) for n in (256, 256, 512, 512)],
        compiler_params=_cparams(("parallel", "parallel")),
        name="ret_in",
    )(x, nw, w, *tabs)


def _ones_rows(rows, cols, period, at):
    r = lax.broadcasted_iota(jnp.int32, (rows, cols), 0)
    return jnp.where(r % period == at, 1.0, 0.0).astype(F32)


def _mla_in_kernel(x_ref, nw_ref, w1_ref, qnw_ref, kvnw_ref, wq_ref, wk_ref, wvt_ref,
                   c_ref, s1_ref, s2_ref, q_ref, k_ref, vt_ref):
    h = _rms(x_ref[0], nw_ref[...], NORM_EPS).astype(BF16)
    p = _dot(h, w1_ref[...])
    cqn = _rms(p[:, 0:256], qnw_ref[...], NORM_EPS).astype(BF16)
    ckvn = _rms(p[:, 256:384], kvnw_ref[...], NORM_EPS).astype(BF16)
    c, s1, s2 = c_ref[...], s1_ref[...], s2_ref[...]
    half = MLA_ROPE_DIM // 2
    krr = _rope(p[:, 384:512], c, s1, s2, half)
    qp = _dot(cqn, wq_ref[...])
    kp = _dot(ckvn, wk_ref[...])
    qscale = ((MLA_NOPE_DIM + MLA_ROPE_DIM) ** -0.5) * LOG2E
    for hd in range(MLA_HEADS):
        sl = slice(hd * HEAD_PAD, (hd + 1) * HEAD_PAD)
        q_ref[0, :, sl] = (_rope(qp[:, sl], c, s1, s2, half) * qscale).astype(BF16)
        k_ref[0, :, sl] = (kp[:, sl] + krr).astype(BF16)
    vt = _nt(wvt_ref[...], ckvn)
    vt_ref[0] = (vt + _ones_rows(vt.shape[0], vt.shape[1], MLA_VA, MLA_V_DIM)).astype(BF16)


def _mla_in(x, nw, w1, qnw, kvnw, wq, wk, wvt, tabs):
    b, s, _ = x.shape
    tm = TOKEN_TILE
    tok = lambda width: pl.BlockSpec((1, tm, width), lambda bi, i: (bi, i, 0))
    tab = pl.BlockSpec((tm, LANES), lambda bi, i: (i, 0))
    rows = MLA_HEADS * MLA_VA
    return pl.pallas_call(
        _mla_in_kernel,
        grid=(b, s // tm),
        in_specs=[tok(D_MODEL), _full((1, D_MODEL)), _full(w1.shape), _full(qnw.shape), _full(kvnw.shape),
                  _full(wq.shape), _full(wk.shape), _full(wvt.shape), tab, tab, tab],
        out_specs=[tok(1024), tok(1024), pl.BlockSpec((1, rows, tm), lambda bi, i: (bi, 0, i))],
        out_shape=[jax.ShapeDtypeStruct((b, s, 1024), BF16), jax.ShapeDtypeStruct((b, s, 1024), BF16),
                   jax.ShapeDtypeStruct((b, rows, s), BF16)],
        compiler_params=_cparams(("parallel", "parallel")),
        name="mla_in",
    )(x, nw, w1, qnw, kvnw, wq, wk, wvt, *tabs)


def _diff_in_kernel(x_ref, nw_ref, wqk_ref, wvt_ref, c_ref, s1_ref, s2_ref, q_ref, k_ref, vt_ref):
    h = _rms(x_ref[0], nw_ref[...], NORM_EPS).astype(BF16)
    p = _dot(h, wqk_ref[...])
    c, s1, s2 = c_ref[...], s1_ref[...], s2_ref[...]
    half = DIFF_HEAD_DIM // 2
    qscale = (DIFF_HEAD_DIM ** -0.5) * LOG2E
    for hd in range(DIFF_HEADS):
        sl = slice(hd * HEAD_PAD, (hd + 1) * HEAD_PAD)
        q_ref[0, :, sl] = (_rope(p[:, sl], c, s1, s2, half) * qscale).astype(BF16)
        k_ref[0, :, sl] = _rope(p[:, 512 + hd * HEAD_PAD:512 + (hd + 1) * HEAD_PAD], c, s1, s2, half).astype(BF16)
    vt = _nt(wvt_ref[...], h)
    vt_ref[0] = (vt + _ones_rows(vt.shape[0], vt.shape[1], DIFF_VA, DIFF_V_DIM)).astype(BF16)


def _diff_in(x, nw, wqk, wvt, tabs):
    b, s, _ = x.shape
    tm = TOKEN_TILE
    tok = lambda width: pl.BlockSpec((1, tm, width), lambda bi, i: (bi, i, 0))
    tab = pl.BlockSpec((tm, LANES), lambda bi, i: (i, 0))
    rows = DIFF_HEADS * DIFF_VA
    return pl.pallas_call(
        _diff_in_kernel,
        grid=(b, s // tm),
        in_specs=[tok(D_MODEL), _full((1, D_MODEL)), _full(wqk.shape), _full(wvt.shape), tab, tab, tab],
        out_specs=[tok(512), tok(512), pl.BlockSpec((1, rows, tm), lambda bi, i: (bi, 0, i))],
        out_shape=[jax.ShapeDtypeStruct((b, s, 512), BF16), jax.ShapeDtypeStruct((b, s, 512), BF16),
                   jax.ShapeDtypeStruct((b, rows, s), BF16)],
        compiler_params=_cparams(("parallel", "parallel")),
        name="diff_in",
    )(x, nw, wqk, wvt, *tabs)


def _halo_specs(tm, s, width):
    per = tm // HALO
    last = s // HALO - 1
    main = pl.BlockSpec((1, tm, width), lambda bi, i: (bi, i, 0))
    prev = pl.BlockSpec((1, HALO, width), lambda bi, i: (bi, jnp.maximum(i * per - 1, 0), 0))
    nxt = pl.BlockSpec((1, HALO, width), lambda bi, i: (bi, jnp.minimum((i + 1) * per, last), 0))
    return prev, main, nxt


def _conv3(g_ref, tm, w_ref, b_ref):
    u = w_ref[0:1, :] * g_ref[pl.ds(HALO - 1, tm), :]
    u = u + w_ref[1:2, :] * g_ref[pl.ds(HALO, tm), :]
    u = u + w_ref[2:3, :] * g_ref[pl.ds(HALO + 1, tm), :]
    return u + b_ref[...]


def _zero_halo_at_sequence_ends(g_ref, tm):
    i = pl.program_id(1)

    @pl.when(i == 0)
    def _():
        g_ref[0:HALO, :] = jnp.zeros((HALO, g_ref.shape[1]), F32)

    @pl.when(i == pl.num_programs(1) - 1)
    def _():
        g_ref[HALO + tm:2 * HALO + tm, :] = jnp.zeros((HALO, g_ref.shape[1]), F32)


def _softplus(t):
    return jnp.maximum(t, 0.0) + jnp.log1p(jnp.exp(-jnp.abs(t)))


def _ssm_in_kernel(xp_ref, x_ref, xn_ref, nw_ref, wx_ref, wzd_ref, wdtt_ref, cw_ref, cb_ref, dtb_ref, dtbc_ref,
                   xbc_ref, z_ref, dt_ref, dtt_ref, g_ref):
    tm = x_ref.shape[1]
    nw = nw_ref[...]
    hm = _rms(x_ref[0], nw, NORM_EPS)
    hcat = jnp.concatenate([_rms(xp_ref[0], nw, NORM_EPS), hm, _rms(xn_ref[0], nw, NORM_EPS)], axis=0)
    g_ref[...] = _dot(hcat.astype(BF16), wx_ref[...])
    _zero_halo_at_sequence_ends(g_ref, tm)
    xbc_ref[0] = _silu(_conv3(g_ref, tm, cw_ref, cb_ref))
    hb = hm.astype(BF16)
    zd = _dot(hb, wzd_ref[...])
    z_ref[0] = _silu(zd[:, 0:512])
    sp = _softplus(zd[:, 512:640] + dtb_ref[...])
    lane = lax.broadcasted_iota(jnp.int32, sp.shape, 1)
    dt_ref[0] = jnp.where(lane < 2 * SSM_HEADS, sp, 0.0)
    dtt_ref[0] = _softplus(_nt(wdtt_ref[...], hb) + dtbc_ref[...])


def _ssm_in(x, nw, wx, wzd, wdtt, cw, cb, dtb, dtbc):
    b, s, _ = x.shape
    tm = TOKEN_TILE
    tok = lambda width: pl.BlockSpec((1, tm, width), lambda bi, i: (bi, i, 0))
    return pl.pallas_call(
        _ssm_in_kernel,
        grid=(b, s // tm),
        in_specs=[*_halo_specs(tm, s, D_MODEL), _full((1, D_MODEL)), _full(wx.shape), _full(wzd.shape),
                  _full(wdtt.shape), _full(cw.shape), _full(cb.shape), _full(dtb.shape), _full(dtbc.shape)],
        out_specs=[tok(1024), tok(512), tok(LANES), pl.BlockSpec((1, 2 * SSM_HEADS, tm), lambda bi, i: (bi, 0, i))],
        out_shape=[jax.ShapeDtypeStruct((b, s, 1024), F32), jax.ShapeDtypeStruct((b, s, 512), F32),
                   jax.ShapeDtypeStruct((b, s, LANES), F32), jax.ShapeDtypeStruct((b, 2 * SSM_HEADS, s), F32)],
        scratch_shapes=[pltpu.VMEM((tm + 2 * HALO, SSM_CONV_CH), F32)],
        compiler_params=_cparams(("parallel", "parallel")),
        name="ssm_in",
    )(x, x, x, nw, wx, wzd, wdtt, cw, cb, dtb, dtbc)


def _ret_state_kernel(kf_ref, vf_ref, kb_ref, vb_ref, kdf_ref, kdb_ref, cdf_ref, cdb_ref,
                      sf_ref, sb_ref, st_ref):
    @pl.when(pl.program_id(1) == 0)
    def _():
        st_ref[...] = jnp.zeros(st_ref.shape, F32)

    def scan(d, k_ref, v_ref, kd_ref, cd_ref, out_ref):
        incs = []
        for g in range(CHUNKS_PER_STEP):
            rows = slice(g * CHUNK, (g + 1) * CHUNK)
            kdt = (k_ref[0, rows, :].astype(F32) * kd_ref[...]).T.astype(BF16)
            v = v_ref[0, rows, :]
            incs.append(jnp.concatenate(
                [_dot(kdt[hd * RET_QK_DIM:(hd + 1) * RET_QK_DIM], v[:, hd * RET_V_DIM:(hd + 1) * RET_V_DIM])
                 for hd in range(RET_HEADS)], axis=0))
        st = st_ref[d]
        order = range(CHUNKS_PER_STEP) if d == 0 else range(CHUNKS_PER_STEP - 1, -1, -1)
        for g in order:
            out_ref[0, g] = st
            st = cd_ref[...] * st + incs[g]
        st_ref[d] = st

    scan(0, kf_ref, vf_ref, kdf_ref, cdf_ref, sf_ref)
    scan(1, kb_ref, vb_ref, kdb_ref, cdb_ref, sb_ref)


def _ret_state(k, v, kdf, kdb, cdf, cdb):
    b, s, _ = k.shape
    g = CHUNKS_PER_STEP
    n = s // (g * CHUNK)
    fwd = lambda width: pl.BlockSpec((1, g * CHUNK, width), lambda bi, c: (bi, c, 0))
    bwd = lambda width: pl.BlockSpec((1, g * CHUNK, width), lambda bi, c: (bi, n - 1 - c, 0))
    rows = RET_HEADS * RET_QK_DIM
    st_shape = jax.ShapeDtypeStruct((b, n * g, rows, RET_V_DIM), F32)
    return pl.pallas_call(
        _ret_state_kernel,
        grid=(b, n),
        in_specs=[fwd(256), fwd(512), bwd(256), bwd(512), _full(kdf.shape), _full(kdb.shape),
                  _full(cdf.shape), _full(cdb.shape)],
        out_specs=[pl.BlockSpec((1, g, rows, RET_V_DIM), lambda bi, c: (bi, c, 0, 0)),
                   pl.BlockSpec((1, g, rows, RET_V_DIM), lambda bi, c: (bi, n - 1 - c, 0, 0))],
        out_shape=[st_shape, st_shape],
        scratch_shapes=[pltpu.VMEM((2, rows, RET_V_DIM), F32)],
        compiler_params=_cparams(("parallel", "arbitrary")),
        name="ret_state",
    )(k, v, k, v, kdf, kdb, cdf, cdb)


def _ret_out_kernel(q_ref, k_ref, v_ref, g_ref, sf_ref, sb_ref, d_ref, qdf_ref, qdb_ref, nw_ref, o_ref):
    gs = range(CHUNKS_PER_STEP)
    rows = [slice(g * CHUNK, (g + 1) * CHUNK) for g in gs]
    lane = lax.broadcasted_iota(jnp.int32, (CHUNK, LANES), 1)
    q = [q_ref[0, rows[g], :].astype(F32) for g in gs]
    qf = [q[g] * qdf_ref[...] for g in gs]
    qb = [q[g] * qdb_ref[...] for g in gs]
    for hd in range(RET_HEADS):
        pair = slice((hd // 2) * LANES, (hd // 2 + 1) * LANES)
        own = (lane < RET_QK_DIM) if hd % 2 == 0 else (lane >= RET_QK_DIM)
        pick = lambda t: jnp.where(own, t[:, pair], 0.0).astype(BF16)
        vsl = slice(hd * RET_V_DIM, (hd + 1) * RET_V_DIM)
        sc = [_nt(pick(q[g]), k_ref[0, rows[g], pair]) * d_ref[hd] for g in gs]
        cross = [_dot(pick(qf[g]), sf_ref[0, g, pair, :].astype(BF16))
                 + _dot(pick(qb[g]), sb_ref[0, g, pair, :].astype(BF16)) for g in gs]
        y = [_dot(sc[g].astype(BF16), v_ref[0, rows[g], vsl]) + cross[g] for g in gs]
        for g in gs:
            yn = _rms(y[g], nw_ref[:, vsl], HEAD_NORM_EPS)
            o_ref[0, rows[g], vsl] = (g_ref[0, rows[g], vsl].astype(F32) * yn).astype(BF16)


def _ret_out(q, k, v, g, sf, sb, dmat, qdf, qdb, nw):
    b, s, _ = q.shape
    gc = CHUNKS_PER_STEP
    n = s // (gc * CHUNK)
    tok = lambda width: pl.BlockSpec((1, gc * CHUNK, width), lambda bi, c: (bi, c, 0))
    st = pl.BlockSpec((1, gc) + sf.shape[2:], lambda bi, c: (bi, c, 0, 0))
    return pl.pallas_call(
        _ret_out_kernel,
        grid=(b, n),
        in_specs=[tok(256), tok(256), tok(512), tok(512), st, st, _full(dmat.shape), _full(qdf.shape),
                  _full(qdb.shape), _full(nw.shape)],
        out_specs=tok(512),
        out_shape=jax.ShapeDtypeStruct((b, s, 512), BF16),
        compiler_params=_cparams(("parallel", "parallel")),
        name="ret_out",
    )(q, k, v, g, sf, sb, dmat, qdf, qdb, nw)


def _ret_tables(log_decay):
    lg_f = log_decay[0].astype(F32)
    lg_b = log_decay[1].astype(F32)
    pos = jnp.arange(CHUNK, dtype=F32)
    rel = pos[:, None] - pos[None, :]
    low = jnp.where(rel >= 0, jnp.exp(jnp.maximum(rel, 0.0)[None] * lg_f[:, None, None]), 0.0)
    upp = jnp.where(rel <= 0, jnp.exp(jnp.maximum(-rel, 0.0)[None] * lg_b[:, None, None]), 0.0)
    dmat = low + upp
    wide = lambda t: jnp.repeat(t, RET_QK_DIM, axis=1)
    qdf = wide(jnp.exp((pos + 1.0)[:, None] * lg_f))
    qdb = wide(jnp.exp((CHUNK - pos)[:, None] * lg_b))
    kdf = wide(jnp.exp((CHUNK - 1.0 - pos)[:, None] * lg_f))
    kdb = wide(jnp.exp(pos[:, None] * lg_b))
    tall = lambda lg: jnp.broadcast_to(jnp.repeat(jnp.exp(CHUNK * lg), RET_QK_DIM)[:, None],
                                       (RET_HEADS * RET_QK_DIM, RET_V_DIM))
    return dmat, qdf, qdb, kdf, kdb, tall(lg_f), tall(lg_b)


def _split3(x):
    hi = x.astype(BF16)
    r1 = x - hi.astype(F32)
    mid = r1.astype(BF16)
    lo = (r1 - mid.astype(F32)).astype(BF16)
    return hi, mid, lo


def _tri(lower):
    ii = lax.broadcasted_iota(jnp.int32, (CHUNK, CHUNK), 0)
    jj = lax.broadcasted_iota(jnp.int32, (CHUNK, CHUNK), 1)
    return jnp.where((ii >= jj) if lower else (jj >= ii), 1.0, 0.0).astype(BF16)


def _cumsum_time_on_lanes(xt, tri):
    hi, mid, lo = _split3(xt)
    return _nt(hi, tri) + _nt(mid, tri) + _nt(lo, tri)


def _cumsum_time_on_rows(x, tri):
    hi, mid, lo = _split3(x)
    return _dot(tri, hi) + _dot(tri, mid) + _dot(tri, lo)


def _ssd_state_kernel(xf_ref, dtf_ref, xb_ref, dtb_ref, alog_ref, sf_ref, sb_ref, st_ref):
    @pl.when(pl.program_id(1) == 0)
    def _():
        st_ref[...] = jnp.zeros(st_ref.shape, F32)

    head_of_lane = lax.broadcasted_iota(jnp.int32, (1, SSM_INNER), 1) // SSM_HEAD_DIM

    per_group = SSM_INNER // SSM_GROUPS

    def scan(d, x_ref, dt_ref, out_ref):
        gs = range(CHUNKS_PER_STEP)
        rows = [slice(g * CHUNK, (g + 1) * CHUNK) for g in gs]
        r0 = d * SSM_HEADS
        a = -jnp.exp(alog_ref[r0:r0 + SSM_HEADS, :])
        tri = _tri(lower=(d == 0))
        dtt = [dt_ref[0, r0:r0 + SSM_HEADS, rows[g]] for g in gs]
        cum = [_cumsum_time_on_lanes(dtt[g] * a, tri) for g in gs]
        endb = [jnp.broadcast_to(cum[g][:, 0:1] if d == 1 else cum[g][:, CHUNK - 1:CHUNK], cum[g].shape)
                for g in gs]
        wt = [jnp.exp(endb[g] - cum[g]) * dtt[g] for g in gs]
        edec = [jnp.exp(endb[g]) for g in gs]
        xt = [x_ref[0, rows[g], 0:SSM_INNER].T for g in gs]
        bt = [[x_ref[0, rows[g], SSM_INNER + k * SSM_STATE:SSM_INNER + (k + 1) * SSM_STATE].T.astype(BF16)
               for k in range(SSM_GROUPS)] for g in gs]
        xwt = [jnp.concatenate([xt[g][hd * SSM_HEAD_DIM:(hd + 1) * SSM_HEAD_DIM] * wt[g][hd:hd + 1]
                                for hd in range(SSM_HEADS)], axis=0).astype(BF16) for g in gs]
        inc = [jnp.concatenate([_nt(bt[g][k], xwt[g][k * per_group:(k + 1) * per_group])
                                for k in range(SSM_GROUPS)], axis=1) for g in gs]
        dec = []
        for g in gs:
            row_dec = jnp.zeros((1, SSM_INNER), F32)
            for hd in range(SSM_HEADS):
                row = jnp.concatenate([edec[g][hd:hd + 1]] * (SSM_INNER // LANES), axis=1)
                row_dec = jnp.where(head_of_lane == hd, row, row_dec)
            dec.append(row_dec)
        st = st_ref[d]
        for g in (gs if d == 0 else reversed(gs)):
            out_ref[0, g] = st
            st = st * dec[g] + inc[g]
        st_ref[d] = st

    scan(0, xf_ref, dtf_ref, sf_ref)
    scan(1, xb_ref, dtb_ref, sb_ref)


def _ssd_state(xbc, dtt, alog):
    b, s, _ = xbc.shape
    g = CHUNKS_PER_STEP
    n = s // (g * CHUNK)
    fwd = lambda width: pl.BlockSpec((1, g * CHUNK, width), lambda bi, c: (bi, c, 0))
    bwd = lambda width: pl.BlockSpec((1, g * CHUNK, width), lambda bi, c: (bi, n - 1 - c, 0))
    dt_fwd = pl.BlockSpec((1, 2 * SSM_HEADS, g * CHUNK), lambda bi, c: (bi, 0, c))
    dt_bwd = pl.BlockSpec((1, 2 * SSM_HEADS, g * CHUNK), lambda bi, c: (bi, 0, n - 1 - c))
    st_shape = jax.ShapeDtypeStruct((b, n * g, SSM_STATE, SSM_INNER), F32)
    return pl.pallas_call(
        _ssd_state_kernel,
        grid=(b, n),
        in_specs=[fwd(768), dt_fwd, bwd(768), dt_bwd, _full(alog.shape)],
        out_specs=[pl.BlockSpec((1, g, SSM_STATE, SSM_INNER), lambda bi, c: (bi, c, 0, 0)),
                   pl.BlockSpec((1, g, SSM_STATE, SSM_INNER), lambda bi, c: (bi, n - 1 - c, 0, 0))],
        out_shape=[st_shape, st_shape],
        scratch_shapes=[pltpu.VMEM((2, SSM_STATE, SSM_INNER), F32)],
        compiler_params=_cparams(("parallel", "arbitrary")),
        name="ssd_state",
    )(xbc, dtt, xbc, dtt, alog)


def _ssd_out_kernel(x_ref, dt_ref, dtt_ref, z_ref, sf_ref, sb_ref, alog_ref, arow_ref, dskip_ref, nw_ref, o_ref):
    gs = range(CHUNKS_PER_STEP)
    rows = [slice(g * CHUNK, (g + 1) * CHUNK) for g in gs]
    nh = SSM_HEADS
    low_t, upp_t = _tri(True), _tri(False)
    a_t = -jnp.exp(alog_ref[...])
    a_row = -jnp.exp(arow_ref[...])
    dtt = [dtt_ref[0, :, rows[g]] for g in gs]
    cumt = [jnp.concatenate([_cumsum_time_on_lanes(dtt[g][0:nh] * a_t[0:nh], low_t),
                             _cumsum_time_on_lanes(dtt[g][nh:2 * nh] * a_t[nh:2 * nh], upp_t)], axis=0) for g in gs]
    da = [dt_ref[0, rows[g], :] * a_row for g in gs]
    head_lane = lax.broadcasted_iota(jnp.int32, (CHUNK, LANES), 1)
    cum = [jnp.where(head_lane < nh, _cumsum_time_on_rows(da[g], low_t), _cumsum_time_on_rows(da[g], upp_t))
           for g in gs]
    ii = lax.broadcasted_iota(jnp.int32, (CHUNK, CHUNK), 0)
    jj = lax.broadcasted_iota(jnp.int32, (CHUNK, CHUNK), 1)
    low = ii >= jj
    upp = jj >= ii
    first = jj < SSM_HEAD_DIM
    heads_per_group = nh // SSM_GROUPS
    c0 = SSM_INNER + SSM_GROUPS * SSM_STATE
    ys = [[] for _ in gs]
    for pr in range(nh // 2):
        k = (2 * pr) // heads_per_group
        pair = slice(pr * LANES, (pr + 1) * LANES)
        cg = [x_ref[0, rows[g], c0 + k * SSM_STATE:c0 + (k + 1) * SSM_STATE].astype(BF16) for g in gs]
        bg = [x_ref[0, rows[g], SSM_INNER + k * SSM_STATE:SSM_INNER + (k + 1) * SSM_STATE].astype(BF16) for g in gs]
        cb = [_nt(cg[g], bg[g]) for g in gs]
        off = [_dot(cg[g], sf_ref[0, g, :, pair].astype(BF16)) for g in gs]
        offb = [_dot(cg[g], sb_ref[0, g, :, pair].astype(BF16)) for g in gs]
        diag = [[], []]
        ecol = [[], []]
        for e, hd in enumerate((2 * pr, 2 * pr + 1)):
            colf = [jnp.broadcast_to(cum[g][:, hd:hd + 1], (CHUNK, CHUNK)) for g in gs]
            colb = [jnp.broadcast_to(cum[g][:, nh + hd:nh + hd + 1], (CHUNK, CHUNK)) for g in gs]
            segf = [jnp.exp(jnp.where(low, colf[g] - cumt[g][hd:hd + 1], NEG_BIG)) for g in gs]
            segb = [jnp.exp(jnp.where(upp, colb[g] - cumt[g][nh + hd:nh + hd + 1], NEG_BIG)) for g in gs]
            w = [cb[g] * (segf[g] * dtt[g][hd:hd + 1] + segb[g] * dtt[g][nh + hd:nh + hd + 1]) for g in gs]
            diag[e] = [_dot(w[g].astype(BF16), x_ref[0, rows[g], pair].astype(BF16)) for g in gs]
            ecol[e] = [(jnp.exp(colf[g]), jnp.exp(colb[g])) for g in gs]
        for g in gs:
            yd = jnp.where(first, diag[0][g], diag[1][g])
            ef = jnp.where(first, ecol[0][g][0], ecol[1][g][0])
            eb = jnp.where(first, ecol[0][g][1], ecol[1][g][1])
            ys[g].append(yd + ef * off[g] + eb * offb[g])
    gw = SSM_INNER // SSM_GROUPS
    for g in gs:
        y = jnp.concatenate(ys[g], axis=1) + dskip_ref[...] * x_ref[0, rows[g], 0:SSM_INNER]
        y = y * z_ref[0, rows[g], :]
        for k in range(SSM_GROUPS):
            sl = slice(k * gw, (k + 1) * gw)
            o_ref[0, rows[g], sl] = _rms(y[:, sl], nw_ref[:, sl], HEAD_NORM_EPS).astype(BF16)


def _ssd_out(xbc, dt, dtt, z, sf, sb, alog, arow, dskip, nw):
    b, s, _ = xbc.shape
    gc = CHUNKS_PER_STEP
    n = s // (gc * CHUNK)
    tok = lambda width: pl.BlockSpec((1, gc * CHUNK, width), lambda bi, c: (bi, c, 0))
    st = pl.BlockSpec((1, gc, SSM_STATE, SSM_INNER), lambda bi, c: (bi, c, 0, 0))
    return pl.pallas_call(
        _ssd_out_kernel,
        grid=(b, n),
        in_specs=[tok(1024), tok(LANES), pl.BlockSpec((1, 2 * SSM_HEADS, gc * CHUNK), lambda bi, c: (bi, 0, c)),
                  tok(512), st, st, _full(alog.shape), _full(arow.shape), _full(dskip.shape), _full(nw.shape)],
        out_specs=tok(512),
        out_shape=jax.ShapeDtypeStruct((b, s, 512), BF16),
        compiler_params=_cparams(("parallel", "parallel")),
        name="ssd_out",
    )(xbc, dt, dtt, z, sf, sb, alog, arow, dskip, nw)


def _attn_scores_t(q, k_ref, vt_ref, seq):
    tk = min(ATTN_K_TILE, seq)
    rows = vt_ref.shape[1]

    m = jnp.full((1, ATTN_Q_TILE), NEG_BIG, F32)
    acc = jnp.zeros((rows, ATTN_Q_TILE), F32)
    n = seq // tk
    scores = lambda c: _nt(k_ref[0, c * tk:(c + 1) * tk, :], q)
    pending = [scores(c) for c in range(min(ATTN_LOOKAHEAD, n))]
    for c in range(n):
        s = pending.pop(0)
        if c + ATTN_LOOKAHEAD < n:
            pending.append(scores(c + ATTN_LOOKAHEAD))
        m_new = jnp.maximum(m, jnp.max(s, axis=0, keepdims=True))
        p = jnp.exp2(s - m_new).astype(BF16)
        acc = acc * jnp.exp2(m - m_new) + _dot(vt_ref[0, :, c * tk:(c + 1) * tk], p)
        m = m_new
    return acc


def _mla_attn_kernel(q_ref, k_ref, vt_ref, o_ref):
    tq = ATTN_Q_TILE
    for t in range(ATTN_TILES_PER_STEP):
        cols = slice(t * tq, (t + 1) * tq)
        acc = _attn_scores_t(q_ref[0, cols, :], k_ref, vt_ref, k_ref.shape[1])
        o_ref[0, :, cols] = (acc[0:MLA_V_DIM] / acc[MLA_V_DIM:MLA_V_DIM + 1]).astype(BF16)


def _mla_attn(q, k, vt):
    b, s, _ = q.shape
    tq = ATTN_Q_TILE * ATTN_TILES_PER_STEP
    return pl.pallas_call(
        _mla_attn_kernel,
        grid=(b, MLA_HEADS, s // tq),
        in_specs=[pl.BlockSpec((1, tq, HEAD_PAD), lambda bi, h, i: (bi, i, h)),
                  pl.BlockSpec((1, s, HEAD_PAD), lambda bi, h, i: (bi, 0, h)),
                  pl.BlockSpec((1, MLA_VA, s), lambda bi, h, i: (bi, h, 0))],
        out_specs=pl.BlockSpec((1, MLA_V_DIM, tq), lambda bi, h, i: (bi, h, i)),
        out_shape=jax.ShapeDtypeStruct((b, MLA_HEADS * MLA_V_DIM, s), BF16),
        compiler_params=_cparams(("parallel", "parallel", "parallel")),
        name="mla_attn",
    )(q, k, vt)


def _diff_attn_kernel(q_ref, k_ref, vt_ref, lam_ref, nw_ref, o_ref, *, lam_init):
    tq = ATTN_Q_TILE // 2
    dv = DIFF_V_DIM
    lv = lam_ref[...]
    lam = (jnp.exp(jnp.sum(lv[0:1] * lv[1:2], axis=1, keepdims=True))
           - jnp.exp(jnp.sum(lv[2:3] * lv[3:4], axis=1, keepdims=True)) + lam_init)
    for t in range(ATTN_TILES_PER_STEP):
        cols = slice(t * tq, (t + 1) * tq)
        q = q_ref[0, cols, :].astype(F32)
        lane = lax.broadcasted_iota(jnp.int32, q.shape, 1)
        first = lane < DIFF_HEAD_DIM
        both = jnp.concatenate([jnp.where(first, q, 0.0), jnp.where(first, 0.0, q)], axis=0).astype(BF16)
        acc = _attn_scores_t(both, k_ref, vt_ref, k_ref.shape[1])
        o1 = acc[0:dv, 0:tq] / acc[dv:dv + 1, 0:tq]
        o2 = acc[0:dv, tq:2 * tq] / acc[dv:dv + 1, tq:2 * tq]
        o = o1 - lam * o2
        o = o * lax.rsqrt(jnp.mean(o * o, axis=0, keepdims=True) + HEAD_NORM_EPS) * nw_ref[...]
        o_ref[0, :, cols] = (o * (1.0 - lam_init)).astype(BF16)


def _diff_attn(q, k, vt, lam_vec, nw, lam_init):
    b, s, _ = q.shape
    tq = (ATTN_Q_TILE // 2) * ATTN_TILES_PER_STEP
    return pl.pallas_call(
        functools.partial(_diff_attn_kernel, lam_init=lam_init),
        grid=(b, DIFF_HEADS, s // tq),
        in_specs=[pl.BlockSpec((1, tq, HEAD_PAD), lambda bi, h, i: (bi, i, h)),
                  pl.BlockSpec((1, s, HEAD_PAD), lambda bi, h, i: (bi, 0, h)),
                  pl.BlockSpec((1, DIFF_VA, s), lambda bi, h, i: (bi, h, 0)),
                  _full(lam_vec.shape), _full(nw.shape)],
        out_specs=pl.BlockSpec((1, DIFF_V_DIM, tq), lambda bi, h, i: (bi, h, i)),
        out_shape=jax.ShapeDtypeStruct((b, DIFF_HEADS * DIFF_V_DIM, s), BF16),
        compiler_params=_cparams(("parallel", "parallel", "parallel")),
        name="diff_attn",
    )(q, k, vt, lam_vec, nw)


def _merge_kernel(x_ref, nw_ref, wg_ref, ret_ref, mlat_ref, difft_ref, ssm_ref, wb_ref, wo_ref, o_ref):
    x = x_ref[0]
    h = _rms(x, nw_ref[...], NORM_EPS).astype(BF16)
    rows = lambda t_ref: t_ref[0].astype(F32).T.astype(BF16)
    outs = (ret_ref[0], rows(mlat_ref), rows(difft_ref), ssm_ref[0])
    merged = None
    for i in range(N_BRANCH):
        gate = _sigmoid(_dot(h, wg_ref[:, i * D_MODEL:(i + 1) * D_MODEL]))
        term = gate * _dot(outs[i], wb_ref[i])
        merged = term if merged is None else merged + term
    o_ref[0] = x + _dot(merged.astype(BF16), wo_ref[...])


def _merge(x, nw, wg, ret, mlat, difft, ssm, wb, wo):
    b, s, _ = x.shape
    tm = TOKEN_TILE
    tok = lambda width: pl.BlockSpec((1, tm, width), lambda bi, i: (bi, i, 0))
    tr = pl.BlockSpec((1, BRANCH_WIDTH, tm), lambda bi, i: (bi, 0, i))
    return pl.pallas_call(
        _merge_kernel,
        grid=(b, s // tm),
        in_specs=[tok(D_MODEL), _full((1, D_MODEL)), _full(wg.shape), tok(512), tr, tr, tok(512),
                  _full(wb.shape), _full(wo.shape)],
        out_specs=tok(D_MODEL),
        out_shape=jax.ShapeDtypeStruct((b, s, D_MODEL), F32),
        compiler_params=_cparams(("parallel", "parallel")),
        name="merge",
    )(x, nw, wg, ret, mlat, difft, ssm, wb, wo)


FFN_COL_TILE = 1408


def _ffn_kernel(xp_ref, x_ref, xn_ref, nw_ref, wgate_ref, wup_ref, cw_ref, cb_ref, wdown_ref, fnw_ref,
                o_ref, g_ref, *, final_norm):
    tm = x_ref.shape[1]
    nw = nw_ref[...]
    x = x_ref[0]
    hm = _rms(x, nw, NORM_EPS)
    hcat = jnp.concatenate([_rms(xp_ref[0], nw, NORM_EPS), hm, _rms(xn_ref[0], nw, NORM_EPS)],
                           axis=0).astype(BF16)
    hb = hm.astype(BF16)
    acc = x
    for j in range(D_FF // FFN_COL_TILE):
        cols = slice(j * FFN_COL_TILE, (j + 1) * FFN_COL_TILE)
        g_ref[...] = _dot(hcat, wgate_ref[:, cols])
        _zero_halo_at_sequence_ends(g_ref, tm)
        u = _conv3(g_ref, tm, cw_ref.at[:, cols], cb_ref.at[:, cols])
        act = (_silu(u) * _dot(hb, wup_ref[:, cols])).astype(BF16)
        acc = acc + _dot(act, wdown_ref[cols, :])
    if final_norm:
        acc = _rms(acc, fnw_ref[...], NORM_EPS)
    o_ref[0] = acc


def _ffn(x, nw, wgate, wup, cw, cb, wdown, fnw, final_norm):
    b, s, _ = x.shape
    tm = TOKEN_TILE
    return pl.pallas_call(
        functools.partial(_ffn_kernel, final_norm=final_norm),
        grid=(b, s // tm),
        in_specs=[*_halo_specs(tm, s, D_MODEL), _full((1, D_MODEL)), _full(wgate.shape), _full(wup.shape),
                  _full(cw.shape), _full(cb.shape), _full(wdown.shape), _full((1, D_MODEL))],
        out_specs=pl.BlockSpec((1, tm, D_MODEL), lambda bi, i: (bi, i, 0)),
        out_shape=jax.ShapeDtypeStruct((b, s, D_MODEL), F32),
        scratch_shapes=[pltpu.VMEM((tm + 2 * HALO, FFN_COL_TILE), F32)],
        compiler_params=_cparams(("parallel", "parallel")),
        name="ffn",
    )(x, x, x, nw, wgate, wup, cw, cb, wdown, fnw)


def _pad_axis(t, axis, size):
    pad = [(0, 0)] * t.ndim
    pad[axis] = (0, size - t.shape[axis])
    return jnp.pad(t, pad)


def _layer_params(l, p):
    w_in = p["w_in"][l]
    row = lambda t: t.reshape(1, -1).astype(F32)
    d = {}
    d["norm_mix"] = row(p["norm_mix_w"][l])
    d["w_ret"] = w_in[:, _O_RET:_O_MLA_CQ].astype(BF16)
    kr = jnp.pad(w_in[:, _O_MLA_KR:_O_DIFF_QK], ((0, 0), (MLA_NOPE_DIM, HEAD_PAD - MLA_NOPE_DIM - MLA_ROPE_DIM)))
    d["w_mla1"] = jnp.concatenate([w_in[:, _O_MLA_CQ:_O_MLA_KR], kr], axis=1).astype(BF16)
    d["mla_qnw"] = row(p["mla_q_norm_w"][l])
    d["mla_kvnw"] = row(p["mla_kv_norm_w"][l])
    uq = p["mla_w_uq"][l].reshape(MLA_Q_RANK, MLA_HEADS, MLA_NOPE_DIM + MLA_ROPE_DIM)
    d["w_mla_q"] = _pad_axis(uq, 2, HEAD_PAD).reshape(MLA_Q_RANK, MLA_HEADS * HEAD_PAD).astype(BF16)
    ukv = p["mla_w_ukv"][l].reshape(MLA_KV_RANK, MLA_HEADS, MLA_NOPE_DIM + MLA_V_DIM)
    d["w_mla_k"] = _pad_axis(ukv[:, :, :MLA_NOPE_DIM], 2, HEAD_PAD).reshape(
        MLA_KV_RANK, MLA_HEADS * HEAD_PAD).astype(BF16)
    vt = jnp.transpose(ukv[:, :, MLA_NOPE_DIM:], (1, 2, 0))
    d["w_mla_vt"] = _pad_axis(vt, 1, MLA_VA).reshape(MLA_HEADS * MLA_VA, MLA_KV_RANK).astype(BF16)
    d["w_diff_qk"] = w_in[:, _O_DIFF_QK:_O_DIFF_V].astype(BF16)
    dvt = w_in[:, _O_DIFF_V:_O_SSM_Z].T.reshape(DIFF_HEADS, DIFF_V_DIM, D_MODEL)
    d["w_diff_vt"] = _pad_axis(dvt, 1, DIFF_VA).reshape(DIFF_HEADS * DIFF_VA, D_MODEL).astype(BF16)
    d["diff_lambda"] = p["diff_lambda"][l].astype(F32)
    d["diff_nw"] = jnp.broadcast_to(p["diff_norm_w"][l].astype(F32)[:, None], (DIFF_V_DIM, ATTN_Q_TILE // 2))
    d["w_ssm_x"] = w_in[:, _O_SSM_XBC:_O_SSM_DT].astype(BF16)
    zdt = jnp.concatenate([w_in[:, _O_SSM_Z:_O_SSM_XBC], w_in[:, _O_SSM_DT:_O_GATE]], axis=1)
    d["w_ssm_zdt"] = _pad_axis(zdt, 1, SSM_INNER + LANES).astype(BF16)
    d["ssm_conv_w"] = p["ssm_conv_w"][l].astype(F32)
    d["ssm_conv_b"] = row(p["ssm_conv_b"][l])
    d["ssm_dt_bias"] = _pad_axis(row(p["ssm_dt_bias"][l]), 1, LANES)
    d["ssm_dt_bias_col"] = p["ssm_dt_bias"][l].astype(F32).reshape(2 * SSM_HEADS, 1)
    d["w_ssm_dtt"] = w_in[:, _O_SSM_DT:_O_GATE].T.astype(BF16)
    d["ssm_alog_row"] = _pad_axis(row(p["ssm_a_log"][l]), 1, LANES)
    d["ssm_alog"] = jnp.broadcast_to(p["ssm_a_log"][l].astype(F32).reshape(2 * SSM_HEADS, 1), (2 * SSM_HEADS, CHUNK))
    d["ssm_dskip"] = jnp.repeat(p["ssm_d"][l].astype(F32), SSM_HEAD_DIM).reshape(1, SSM_INNER)
    d["ssm_nw"] = row(p["ssm_norm_w"][l])
    d["ret_nw"] = row(p["ret_norm_w"][l])
    d["ret_log_decay"] = p["ret_log_decay"][l]
    d["w_gate"] = w_in[:, _O_GATE:_IN_COLS].astype(BF16)
    d["w_branch"] = p["w_branch"][l].astype(BF16)
    d["w_out"] = p["w_out"][l].astype(BF16)
    d["norm_ffn"] = row(p["norm_ffn_w"][l])
    d["ffn_w_gate"] = p["ffn_w_gate"][l].astype(BF16)
    d["ffn_w_up"] = p["ffn_w_up"][l].astype(BF16)
    d["ffn_conv_w"] = p["ffn_conv_w"][l].astype(F32)
    d["ffn_conv_b"] = row(p["ffn_conv_b"][l])
    d["ffn_w_down"] = p["ffn_w_down"][l].astype(BF16)
    d["final_nw"] = row(p["final_norm_w"])
    return d


def _encoder_layer(x, l, d, ropes):
    rope64, rope_mla = ropes
    rq, rk, rv, rg = _ret_in(x, d["norm_mix"], d["w_ret"], rope64)
    dmat, qdf, qdb, kdf, kdb, cdf, cdb = _ret_tables(d["ret_log_decay"])
    rsf, rsb = _ret_state(rk, rv, kdf, kdb, cdf, cdb)
    ret = _ret_out(rq, rk, rv, rg, rsf, rsb, dmat, qdf, qdb, d["ret_nw"])
    mq, mk, mvt = _mla_in(x, d["norm_mix"], d["w_mla1"], d["mla_qnw"], d["mla_kvnw"], d["w_mla_q"],
                          d["w_mla_k"], d["w_mla_vt"], rope_mla)
    mlat = _mla_attn(mq, mk, mvt)
    dq, dk, dvt = _diff_in(x, d["norm_mix"], d["w_diff_qk"], d["w_diff_vt"], rope64)
    lam_init = 0.8 - 0.6 * math.exp(-0.3 * l)
    difft = _diff_attn(dq, dk, dvt, d["diff_lambda"], d["diff_nw"], lam_init)
    xbc, sz, dt, dtt = _ssm_in(x, d["norm_mix"], d["w_ssm_x"], d["w_ssm_zdt"], d["w_ssm_dtt"], d["ssm_conv_w"],
                               d["ssm_conv_b"], d["ssm_dt_bias"], d["ssm_dt_bias_col"])
    ssf, ssb = _ssd_state(xbc, dtt, d["ssm_alog"])
    ssm = _ssd_out(xbc, dt, dtt, sz, ssf, ssb, d["ssm_alog"], d["ssm_alog_row"], d["ssm_dskip"], d["ssm_nw"])
    x = _merge(x, d["norm_mix"], d["w_gate"], ret, mlat, difft, ssm, d["w_branch"], d["w_out"])
    return _ffn(x, d["norm_ffn"], d["ffn_w_gate"], d["ffn_w_up"], d["ffn_conv_w"], d["ffn_conv_b"],
                d["ffn_w_down"], d["final_nw"], final_norm=(l == DEPTH - 1))


def _trunk(x, layers):
    s = x.shape[1]
    ropes = (_rope_tables(s, RET_QK_DIM, RET_QK_DIM, 0),
             _rope_tables(s, MLA_ROPE_DIM, HEAD_PAD, MLA_NOPE_DIM))
    for l in range(DEPTH):
        x = _encoder_layer(x, l, layers[l], ropes)
    return x


def kernel(x_prompt, x_sample, norm_mix_w, w_in, ret_log_decay, ret_norm_w, mla_q_norm_w, mla_w_uq, mla_kv_norm_w, mla_w_ukv, diff_lambda, diff_norm_w, ssm_conv_w, ssm_conv_b, ssm_dt_bias, ssm_a_log, ssm_d, ssm_norm_w, w_branch, w_out, norm_ffn_w, ffn_w_gate, ffn_w_up, ffn_conv_w, ffn_conv_b, ffn_w_down, final_norm_w):
    p = {
        "norm_mix_w": norm_mix_w, "w_in": w_in, "ret_log_decay": ret_log_decay, "ret_norm_w": ret_norm_w,
        "mla_q_norm_w": mla_q_norm_w, "mla_w_uq": mla_w_uq, "mla_kv_norm_w": mla_kv_norm_w,
        "mla_w_ukv": mla_w_ukv, "diff_lambda": diff_lambda, "diff_norm_w": diff_norm_w,
        "ssm_conv_w": ssm_conv_w, "ssm_conv_b": ssm_conv_b, "ssm_dt_bias": ssm_dt_bias,
        "ssm_a_log": ssm_a_log, "ssm_d": ssm_d, "ssm_norm_w": ssm_norm_w, "w_branch": w_branch,
        "w_out": w_out, "norm_ffn_w": norm_ffn_w, "ffn_w_gate": ffn_w_gate, "ffn_w_up": ffn_w_up,
        "ffn_conv_w": ffn_conv_w, "ffn_conv_b": ffn_conv_b, "ffn_w_down": ffn_w_down,
        "final_norm_w": final_norm_w,
    }
    layers = [_layer_params(l, p) for l in range(DEPTH)]
    return _trunk(x_prompt, layers), _trunk(x_sample, layers)
```

```python
import functools
import math

import jax
import jax.numpy as jnp
from jax import lax
from jax.experimental import pallas as pl
from jax.experimental.pallas import tpu as pltpu

F32 = jnp.float32
BF16 = jnp.bfloat16

D_MODEL = 1024
DEPTH = 2
ROPE_THETA = 10000.0
NORM_EPS = 1e-6
HEAD_NORM_EPS = 1e-5
CHUNK = 128
N_BRANCH = 4
BRANCH_WIDTH = 512

RET_HEADS, RET_QK_DIM, RET_V_DIM = 4, 64, 128
MLA_HEADS, MLA_NOPE_DIM, MLA_ROPE_DIM, MLA_V_DIM = 8, 64, 32, 64
MLA_Q_RANK, MLA_KV_RANK = 256, 128
DIFF_HEADS, DIFF_HEAD_DIM = 4, 64
DIFF_V_DIM = 2 * DIFF_HEAD_DIM
SSM_HEADS, SSM_HEAD_DIM, SSM_GROUPS, SSM_STATE = 8, 64, 2, 128
SSM_INNER = SSM_HEADS * SSM_HEAD_DIM
SSM_CONV_CH = SSM_INNER + 2 * SSM_GROUPS * SSM_STATE
D_FF = 2816

_O_RET = 0
_O_MLA_CQ = 1536
_O_MLA_CKV = 1792
_O_MLA_KR = 1920
_O_DIFF_QK = 1952
_O_DIFF_V = 2976
_O_SSM_Z = 3488
_O_SSM_XBC = 4000
_O_SSM_DT = 5024
_O_GATE = 5040
_IN_COLS = 9136

LANES = 128
SUBLANES = 8
HALO = SUBLANES
PACK_ROWS = 16
HEAD_PAD = LANES
MLA_VA = MLA_V_DIM + PACK_ROWS
DIFF_VA = DIFF_V_DIM + PACK_ROWS
TOKEN_TILE = 512
CHUNKS_PER_STEP = 4
ATTN_Q_TILE = 512
ATTN_K_TILE = 256
ATTN_LOOKAHEAD = 3
ATTN_TILES_PER_STEP = 4
LOG2E = 1.4426950408889634
NEG_BIG = -1e30
VMEM_LIMIT = 56 * 1024 * 1024


def _cparams(sems, vmem=VMEM_LIMIT):
    return pltpu.CompilerParams(dimension_semantics=sems, vmem_limit_bytes=vmem)


def _full(shape):
    return pl.BlockSpec(shape, lambda *_: (0,) * len(shape))


def _nt(a, b):
    return lax.dot_general(a, b, (((1,), (1,)), ((), ())), preferred_element_type=F32)


def _dot(a, b):
    return jnp.dot(a, b, preferred_element_type=F32)


def _rms(x, w, eps):
    return x * lax.rsqrt(jnp.mean(x * x, axis=-1, keepdims=True) + eps) * w


def _sigmoid(x):
    return 0.5 * jnp.tanh(0.5 * x) + 0.5


def _silu(x):
    return x * _sigmoid(x)


def _rope(x, c, s1, s2, half):
    return x * c + pltpu.roll(x, LANES - half, 1) * s1 + pltpu.roll(x, half, 1) * s2


def _rope_tables(seq, dim, block, off):
    inv_freq = 1.0 / (ROPE_THETA ** (jnp.arange(0, dim, 2, dtype=F32) / dim))
    ang = jnp.arange(seq, dtype=F32)[:, None] * inv_freq[None, :]
    cos, sin = jnp.cos(ang), jnp.sin(ang)
    half = dim // 2
    zero = jnp.zeros_like(sin)
    pre = jnp.zeros((seq, off), F32)
    post = jnp.zeros((seq, block - off - dim), F32)
    c = jnp.concatenate([pre + 1.0, cos, cos, post + 1.0], axis=1)
    s1 = jnp.concatenate([pre, -sin, zero, post], axis=1)
    s2 = jnp.concatenate([pre, zero, sin, post], axis=1)
    reps = LANES // block
    return tuple(jnp.tile(t, (1, reps)) for t in (c, s1, s2))


def _ret_in_kernel(x_ref, nw_ref, w_ref, c_ref, s1_ref, s2_ref, q_ref, k_ref, v_ref, g_ref):
    h = _rms(x_ref[0], nw_ref[...], NORM_EPS).astype(BF16)
    p = _dot(h, w_ref[...])
    c, s1, s2 = c_ref[...], s1_ref[...], s2_ref[...]
    half = RET_QK_DIM // 2
    for i in range(2):
        sl = slice(i * LANES, (i + 1) * LANES)
        q_ref[0, :, sl] = _rope(p[:, sl], c, s1, s2, half).astype(BF16)
        kk = _rope(p[:, 256 + i * LANES:256 + (i + 1) * LANES], c, s1, s2, half)
        k_ref[0, :, sl] = (kk * (RET_QK_DIM ** -0.5)).astype(BF16)
    v_ref[0] = p[:, 512:1024].astype(BF16)
    g_ref[0] = _silu(p[:, 1024:1536]).astype(BF16)


def _ret_in(x, nw, w, tabs):
    b, s, _ = x.shape
    tm = TOKEN_TILE
    tok = lambda width: pl.BlockSpec((1, tm, width), lambda bi, i: (bi, i, 0))
    tab = pl.BlockSpec((tm, LANES), lambda bi, i: (i, 0))
    return pl.pallas_call(
        _ret_in_kernel,
        grid=(b, s // tm),
        in_specs=[tok(D_MODEL), _full((1, D_MODEL)), _full(w.shape), tab, tab, tab],
        out_specs=[tok(256), tok(256), tok(512), tok(512)],
        out_shape=[jax.ShapeDtypeStruct((b, s, n), BF16) for n in (256, 256, 512, 512)],
        compiler_params=_cparams(("parallel", "parallel")),
        name="ret_in",
    )(x, nw, w, *tabs)


def _ones_rows(rows, cols, period, at):
    r = lax.broadcasted_iota(jnp.int32, (rows, cols), 0)
    return jnp.where(r % period == at, 1.0, 0.0).astype(F32)


def _mla_in_kernel(x_ref, nw_ref, w1_ref, qnw_ref, kvnw_ref, wq_ref, wqsw_ref, wk_ref, wvt_ref,
                   c_ref, s1_ref, s2_ref, q_ref, k_ref, vt_ref):
    h = _rms(x_ref[0], nw_ref[...], NORM_EPS).astype(BF16)
    p = _dot(h, w1_ref[...])
    cqn = _rms(p[:, 0:256], qnw_ref[...], NORM_EPS).astype(BF16)
    ckvn = _rms(p[:, 256:384], kvnw_ref[...], NORM_EPS).astype(BF16)
    c, s1, s2 = c_ref[...], s1_ref[...], s2_ref[...]
    half = MLA_ROPE_DIM // 2
    krr = _rope(p[:, 384:512], c, s1, s2, half)
    qp = _dot(cqn, wq_ref[...])
    qsw = _dot(cqn, wqsw_ref[...])
    sin = s2 - s1
    kp = _dot(ckvn, wk_ref[...])
    qscale = ((MLA_NOPE_DIM + MLA_ROPE_DIM) ** -0.5) * LOG2E
    for hd in range(MLA_HEADS):
        sl = slice(hd * HEAD_PAD, (hd + 1) * HEAD_PAD)
        q_ref[0, :, sl] = ((qp[:, sl] * c + qsw[:, sl] * sin) * qscale).astype(BF16)
        k_ref[0, :, sl] = (kp[:, sl] + krr).astype(BF16)
    vt = _nt(wvt_ref[...], ckvn)
    vt_ref[0] = (vt + _ones_rows(vt.shape[0], vt.shape[1], MLA_VA, MLA_V_DIM)).astype(BF16)


def _mla_in(x, nw, w1, qnw, kvnw, wq, wqsw, wk, wvt, tabs):
    b, s, _ = x.shape
    tm = TOKEN_TILE
    tok = lambda width: pl.BlockSpec((1, tm, width), lambda bi, i: (bi, i, 0))
    tab = pl.BlockSpec((tm, LANES), lambda bi, i: (i, 0))
    rows = MLA_HEADS * MLA_VA
    return pl.pallas_call(
        _mla_in_kernel,
        grid=(b, s // tm),
        in_specs=[tok(D_MODEL), _full((1, D_MODEL)), _full(w1.shape), _full(qnw.shape), _full(kvnw.shape),
                  _full(wq.shape), _full(wqsw.shape), _full(wk.shape), _full(wvt.shape), tab, tab, tab],
        out_specs=[tok(1024), tok(1024), pl.BlockSpec((1, rows, tm), lambda bi, i: (bi, 0, i))],
        out_shape=[jax.ShapeDtypeStruct((b, s, 1024), BF16), jax.ShapeDtypeStruct((b, s, 1024), BF16),
                   jax.ShapeDtypeStruct((b, rows, s), BF16)],
        compiler_params=_cparams(("parallel", "parallel")),
        name="mla_in",
    )(x, nw, w1, qnw, kvnw, wq, wqsw, wk, wvt, *tabs)


def _diff_in_kernel(x_ref, nw_ref, wqk_ref, wvt_ref, c_ref, s1_ref, s2_ref, q_ref, k_ref, vt_ref):
    h = _rms(x_ref[0], nw_ref[...], NORM_EPS).astype(BF16)
    p = _dot(h, wqk_ref[...])
    c, s1, s2 = c_ref[...], s1_ref[...], s2_ref[...]
    half = DIFF_HEAD_DIM // 2
    qscale = (DIFF_HEAD_DIM ** -0.5) * LOG2E
    for hd in range(DIFF_HEADS):
        sl = slice(hd * HEAD_PAD, (hd + 1) * HEAD_PAD)
        q_ref[0, :, sl] = (_rope(p[:, sl], c, s1, s2, half) * qscale).astype(BF16)
        k_ref[0, :, sl] = _rope(p[:, 512 + hd * HEAD_PAD:512 + (hd + 1) * HEAD_PAD], c, s1, s2, half).astype(BF16)
    vt = _nt(wvt_ref[...], h)
    vt_ref[0] = (vt + _ones_rows(vt.shape[0], vt.shape[1], DIFF_VA, DIFF_V_DIM)).astype(BF16)


def _diff_in(x, nw, wqk, wvt, tabs):
    b, s, _ = x.shape
    tm = TOKEN_TILE
    tok = lambda width: pl.BlockSpec((1, tm, width), lambda bi, i: (bi, i, 0))
    tab = pl.BlockSpec((tm, LANES), lambda bi, i: (i, 0))
    rows = DIFF_HEADS * DIFF_VA
    return pl.pallas_call(
        _diff_in_kernel,
        grid=(b, s // tm),
        in_specs=[tok(D_MODEL), _full((1, D_MODEL)), _full(wqk.shape), _full(wvt.shape), tab, tab, tab],
        out_specs=[tok(512), tok(512), pl.BlockSpec((1, rows, tm), lambda bi, i: (bi, 0, i))],
        out_shape=[jax.ShapeDtypeStruct((b, s, 512), BF16), jax.ShapeDtypeStruct((b, s, 512), BF16),
                   jax.ShapeDtypeStruct((b, rows, s), BF16)],
        compiler_params=_cparams(("parallel", "parallel")),
        name="diff_in",
    )(x, nw, wqk, wvt, *tabs)


def _halo_specs(tm, s, width):
    per = tm // HALO
    last = s // HALO - 1
    main = pl.BlockSpec((1, tm, width), lambda bi, i: (bi, i, 0))
    prev = pl.BlockSpec((1, HALO, width), lambda bi, i: (bi, jnp.maximum(i * per - 1, 0), 0))
    nxt = pl.BlockSpec((1, HALO, width), lambda bi, i: (bi, jnp.minimum((i + 1) * per, last), 0))
    return prev, main, nxt


def _conv3(g_ref, tm, w_ref, b_ref):
    u = w_ref[0:1, :] * g_ref[pl.ds(HALO - 1, tm), :]
    u = u + w_ref[1:2, :] * g_ref[pl.ds(HALO, tm), :]
    u = u + w_ref[2:3, :] * g_ref[pl.ds(HALO + 1, tm), :]
    return u + b_ref[...]


CONV_ROW_STRIP = 64
CONV_COL_STRIP = 256


def _conv3_strips(g_ref, tm, w_ref, b_ref, emit):
    for r0 in range(0, tm, CONV_ROW_STRIP):
        for c0 in range(0, g_ref.shape[1], CONV_COL_STRIP):
            rows = slice(r0, r0 + CONV_ROW_STRIP)
            cols = slice(c0, c0 + CONV_COL_STRIP)
            u = w_ref[0:1, cols] * g_ref[HALO - 1 + r0:HALO - 1 + r0 + CONV_ROW_STRIP, cols]
            u = u + w_ref[1:2, cols] * g_ref[HALO + r0:HALO + r0 + CONV_ROW_STRIP, cols]
            u = u + w_ref[2:3, cols] * g_ref[HALO + 1 + r0:HALO + 1 + r0 + CONV_ROW_STRIP, cols]
            emit(rows, cols, u + b_ref[:, cols])


def _zero_halo_at_sequence_ends(g_ref, tm):
    i = pl.program_id(1)

    @pl.when(i == 0)
    def _():
        g_ref[0:HALO, :] = jnp.zeros((HALO, g_ref.shape[1]), F32)

    @pl.when(i == pl.num_programs(1) - 1)
    def _():
        g_ref[HALO + tm:2 * HALO + tm, :] = jnp.zeros((HALO, g_ref.shape[1]), F32)


def _softplus(t):
    return jnp.maximum(t, 0.0) + jnp.log1p(jnp.exp(-jnp.abs(t)))


def _ssm_in_kernel(xp_ref, x_ref, xn_ref, nw_ref, wx_ref, wzd_ref, wdtt_ref, cw_ref, cb_ref, dtb_ref, dtbc_ref,
                   xbc_ref, z_ref, dt_ref, dtt_ref, g_ref):
    tm = x_ref.shape[1]
    nw = nw_ref[...]
    hm = _rms(x_ref[0], nw, NORM_EPS)
    hcat = jnp.concatenate([_rms(xp_ref[0], nw, NORM_EPS), hm, _rms(xn_ref[0], nw, NORM_EPS)], axis=0)
    g_ref[...] = _dot(hcat.astype(BF16), wx_ref[...])
    _zero_halo_at_sequence_ends(g_ref, tm)
    def emit(rows, cols, u):
        xbc_ref[0, rows, cols] = _silu(u)

    _conv3_strips(g_ref, tm, cw_ref, cb_ref, emit)
    hb = hm.astype(BF16)
    zd = _dot(hb, wzd_ref[...])
    z_ref[0] = _silu(zd[:, 0:512])
    sp = _softplus(zd[:, 512:640] + dtb_ref[...])
    lane = lax.broadcasted_iota(jnp.int32, sp.shape, 1)
    dt_ref[0] = jnp.where(lane < 2 * SSM_HEADS, sp, 0.0)
    dtt_ref[0] = _softplus(_nt(wdtt_ref[...], hb) + dtbc_ref[...])


def _ssm_in(x, nw, wx, wzd, wdtt, cw, cb, dtb, dtbc):
    b, s, _ = x.shape
    tm = TOKEN_TILE
    tok = lambda width: pl.BlockSpec((1, tm, width), lambda bi, i: (bi, i, 0))
    return pl.pallas_call(
        _ssm_in_kernel,
        grid=(b, s // tm),
        in_specs=[*_halo_specs(tm, s, D_MODEL), _full((1, D_MODEL)), _full(wx.shape), _full(wzd.shape),
                  _full(wdtt.shape), _full(cw.shape), _full(cb.shape), _full(dtb.shape), _full(dtbc.shape)],
        out_specs=[tok(1024), tok(512), tok(LANES), pl.BlockSpec((1, 2 * SSM_HEADS, tm), lambda bi, i: (bi, 0, i))],
        out_shape=[jax.ShapeDtypeStruct((b, s, 1024), F32), jax.ShapeDtypeStruct((b, s, 512), F32),
                   jax.ShapeDtypeStruct((b, s, LANES), F32), jax.ShapeDtypeStruct((b, 2 * SSM_HEADS, s), F32)],
        scratch_shapes=[pltpu.VMEM((tm + 2 * HALO, SSM_CONV_CH), F32)],
        compiler_params=_cparams(("parallel", "parallel")),
        name="ssm_in",
    )(x, x, x, nw, wx, wzd, wdtt, cw, cb, dtb, dtbc)


def _ret_state_kernel(kf_ref, vf_ref, kb_ref, vb_ref, kdf_ref, kdb_ref, cdf_ref, cdb_ref,
                      sf_ref, sb_ref, st_ref):
    @pl.when(pl.program_id(1) == 0)
    def _():
        st_ref[...] = jnp.zeros(st_ref.shape, F32)

    def scan(d, k_ref, v_ref, kd_ref, cd_ref, out_ref):
        incs = []
        for g in range(CHUNKS_PER_STEP):
            rows = slice(g * CHUNK, (g + 1) * CHUNK)
            kdt = (k_ref[0, rows, :].astype(F32) * kd_ref[...]).T.astype(BF16)
            v = v_ref[0, rows, :]
            incs.append(jnp.concatenate(
                [_dot(kdt[hd * RET_QK_DIM:(hd + 1) * RET_QK_DIM], v[:, hd * RET_V_DIM:(hd + 1) * RET_V_DIM])
                 for hd in range(RET_HEADS)], axis=0))
        st = st_ref[d]
        order = range(CHUNKS_PER_STEP) if d == 0 else range(CHUNKS_PER_STEP - 1, -1, -1)
        for g in order:
            out_ref[0, g] = st
            st = cd_ref[...] * st + incs[g]
        st_ref[d] = st

    scan(0, kf_ref, vf_ref, kdf_ref, cdf_ref, sf_ref)
    scan(1, kb_ref, vb_ref, kdb_ref, cdb_ref, sb_ref)


def _ret_state(k, v, kdf, kdb, cdf, cdb):
    b, s, _ = k.shape
    g = CHUNKS_PER_STEP
    n = s // (g * CHUNK)
    fwd = lambda width: pl.BlockSpec((1, g * CHUNK, width), lambda bi, c: (bi, c, 0))
    bwd = lambda width: pl.BlockSpec((1, g * CHUNK, width), lambda bi, c: (bi, n - 1 - c, 0))
    rows = RET_HEADS * RET_QK_DIM
    st_shape = jax.ShapeDtypeStruct((b, n * g, rows, RET_V_DIM), F32)
    return pl.pallas_call(
        _ret_state_kernel,
        grid=(b, n),
        in_specs=[fwd(256), fwd(512), bwd(256), bwd(512), _full(kdf.shape), _full(kdb.shape),
                  _full(cdf.shape), _full(cdb.shape)],
        out_specs=[pl.BlockSpec((1, g, rows, RET_V_DIM), lambda bi, c: (bi, c, 0, 0)),
                   pl.BlockSpec((1, g, rows, RET_V_DIM), lambda bi, c: (bi, n - 1 - c, 0, 0))],
        out_shape=[st_shape, st_shape],
        scratch_shapes=[pltpu.VMEM((2, rows, RET_V_DIM), F32)],
        compiler_params=_cparams(("parallel", "arbitrary")),
        name="ret_state",
    )(k, v, k, v, kdf, kdb, cdf, cdb)


def _ret_out_kernel(q_ref, k_ref, v_ref, g_ref, sf_ref, sb_ref, d_ref, qdf_ref, qdb_ref, nw_ref, o_ref):
    gs = range(CHUNKS_PER_STEP)
    rows = [slice(g * CHUNK, (g + 1) * CHUNK) for g in gs]
    lane = lax.broadcasted_iota(jnp.int32, (CHUNK, LANES), 1)
    q = [q_ref[0, rows[g], :].astype(F32) for g in gs]
    qf = [q[g] * qdf_ref[...] for g in gs]
    qb = [q[g] * qdb_ref[...] for g in gs]
    for hd in range(RET_HEADS):
        pair = slice((hd // 2) * LANES, (hd // 2 + 1) * LANES)
        own = (lane < RET_QK_DIM) if hd % 2 == 0 else (lane >= RET_QK_DIM)
        pick = lambda t: jnp.where(own, t[:, pair], 0.0).astype(BF16)
        vsl = slice(hd * RET_V_DIM, (hd + 1) * RET_V_DIM)
        sc = [_nt(pick(q[g]), k_ref[0, rows[g], pair]) * d_ref[hd] for g in gs]
        cross = [_dot(pick(qf[g]), sf_ref[0, g, pair, :].astype(BF16))
                 + _dot(pick(qb[g]), sb_ref[0, g, pair, :].astype(BF16)) for g in gs]
        y = [_dot(sc[g].astype(BF16), v_ref[0, rows[g], vsl]) + cross[g] for g in gs]
        for g in gs:
            yn = _rms(y[g], nw_ref[:, vsl], HEAD_NORM_EPS)
            o_ref[0, rows[g], vsl] = (g_ref[0, rows[g], vsl].astype(F32) * yn).astype(BF16)


def _ret_out(q, k, v, g, sf, sb, dmat, qdf, qdb, nw):
    b, s, _ = q.shape
    gc = CHUNKS_PER_STEP
    n = s // (gc * CHUNK)
    tok = lambda width: pl.BlockSpec((1, gc * CHUNK, width), lambda bi, c: (bi, c, 0))
    st = pl.BlockSpec((1, gc) + sf.shape[2:], lambda bi, c: (bi, c, 0, 0))
    return pl.pallas_call(
        _ret_out_kernel,
        grid=(b, n),
        in_specs=[tok(256), tok(256), tok(512), tok(512), st, st, _full(dmat.shape), _full(qdf.shape),
                  _full(qdb.shape), _full(nw.shape)],
        out_specs=tok(512),
        out_shape=jax.ShapeDtypeStruct((b, s, 512), BF16),
        compiler_params=_cparams(("parallel", "parallel")),
        name="ret_out",
    )(q, k, v, g, sf, sb, dmat, qdf, qdb, nw)


def _ret_tables(log_decay):
    lg_f = log_decay[0].astype(F32)
    lg_b = log_decay[1].astype(F32)
    pos = jnp.arange(CHUNK, dtype=F32)
    rel = pos[:, None] - pos[None, :]
    low = jnp.where(rel >= 0, jnp.exp(jnp.maximum(rel, 0.0)[None] * lg_f[:, None, None]), 0.0)
    upp = jnp.where(rel <= 0, jnp.exp(jnp.maximum(-rel, 0.0)[None] * lg_b[:, None, None]), 0.0)
    dmat = low + upp
    wide = lambda t: jnp.repeat(t, RET_QK_DIM, axis=1)
    qdf = wide(jnp.exp((pos + 1.0)[:, None] * lg_f))
    qdb = wide(jnp.exp((CHUNK - pos)[:, None] * lg_b))
    kdf = wide(jnp.exp((CHUNK - 1.0 - pos)[:, None] * lg_f))
    kdb = wide(jnp.exp(pos[:, None] * lg_b))
    tall = lambda lg: jnp.broadcast_to(jnp.repeat(jnp.exp(CHUNK * lg), RET_QK_DIM)[:, None],
                                       (RET_HEADS * RET_QK_DIM, RET_V_DIM))
    return dmat, qdf, qdb, kdf, kdb, tall(lg_f), tall(lg_b)


def _split3(x):
    hi = x.astype(BF16)
    r1 = x - hi.astype(F32)
    mid = r1.astype(BF16)
    lo = (r1 - mid.astype(F32)).astype(BF16)
    return hi, mid, lo


def _tri(lower):
    ii = lax.broadcasted_iota(jnp.int32, (CHUNK, CHUNK), 0)
    jj = lax.broadcasted_iota(jnp.int32, (CHUNK, CHUNK), 1)
    return jnp.where((ii >= jj) if lower else (jj >= ii), 1.0, 0.0).astype(BF16)


def _cumsum_time_on_lanes(xt, tri):
    hi, mid, lo = _split3(xt)
    return _nt(hi, tri) + _nt(mid, tri) + _nt(lo, tri)


def _cumsum_time_on_rows(x, tri):
    hi, mid, lo = _split3(x)
    return _dot(tri, hi) + _dot(tri, mid) + _dot(tri, lo)


def _ssd_state_kernel(xf_ref, dtf_ref, xb_ref, dtb_ref, alog_ref, sf_ref, sb_ref, st_ref):
    @pl.when(pl.program_id(1) == 0)
    def _():
        st_ref[...] = jnp.zeros(st_ref.shape, F32)

    head_of_lane = lax.broadcasted_iota(jnp.int32, (1, SSM_INNER), 1) // SSM_HEAD_DIM

    per_group = SSM_INNER // SSM_GROUPS

    def scan(d, x_ref, dt_ref, out_ref):
        gs = range(CHUNKS_PER_STEP)
        rows = [slice(g * CHUNK, (g + 1) * CHUNK) for g in gs]
        r0 = d * SSM_HEADS
        a = -jnp.exp(alog_ref[r0:r0 + SSM_HEADS, :])
        tri = _tri(lower=(d == 0))
        dtt = [dt_ref[0, r0:r0 + SSM_HEADS, rows[g]] for g in gs]
        cum = [_cumsum_time_on_lanes(dtt[g] * a, tri) for g in gs]
        endb = [jnp.broadcast_to(cum[g][:, 0:1] if d == 1 else cum[g][:, CHUNK - 1:CHUNK], cum[g].shape)
                for g in gs]
        wt = [jnp.exp(endb[g] - cum[g]) * dtt[g] for g in gs]
        edec = [jnp.exp(endb[g]) for g in gs]
        xt = [x_ref[0, rows[g], 0:SSM_INNER].T for g in gs]
        bt = [[x_ref[0, rows[g], SSM_INNER + k * SSM_STATE:SSM_INNER + (k + 1) * SSM_STATE].T.astype(BF16)
               for k in range(SSM_GROUPS)] for g in gs]
        xwt = [jnp.concatenate([xt[g][hd * SSM_HEAD_DIM:(hd + 1) * SSM_HEAD_DIM] * wt[g][hd:hd + 1]
                                for hd in range(SSM_HEADS)], axis=0).astype(BF16) for g in gs]
        inc = [jnp.concatenate([_nt(bt[g][k], xwt[g][k * per_group:(k + 1) * per_group])
                                for k in range(SSM_GROUPS)], axis=1) for g in gs]
        dec = []
        for g in gs:
            row_dec = jnp.zeros((1, SSM_INNER), F32)
            for hd in range(SSM_HEADS):
                row = jnp.concatenate([edec[g][hd:hd + 1]] * (SSM_INNER // LANES), axis=1)
                row_dec = jnp.where(head_of_lane == hd, row, row_dec)
            dec.append(row_dec)
        st = st_ref[d]
        for g in (gs if d == 0 else reversed(gs)):
            out_ref[0, g] = st
            st = st * dec[g] + inc[g]
        st_ref[d] = st

    scan(0, xf_ref, dtf_ref, sf_ref)
    scan(1, xb_ref, dtb_ref, sb_ref)


def _ssd_state(xbc, dtt, alog):
    b, s, _ = xbc.shape
    g = CHUNKS_PER_STEP
    n = s // (g * CHUNK)
    fwd = lambda width: pl.BlockSpec((1, g * CHUNK, width), lambda bi, c: (bi, c, 0))
    bwd = lambda width: pl.BlockSpec((1, g * CHUNK, width), lambda bi, c: (bi, n - 1 - c, 0))
    dt_fwd = pl.BlockSpec((1, 2 * SSM_HEADS, g * CHUNK), lambda bi, c: (bi, 0, c))
    dt_bwd = pl.BlockSpec((1, 2 * SSM_HEADS, g * CHUNK), lambda bi, c: (bi, 0, n - 1 - c))
    st_shape = jax.ShapeDtypeStruct((b, n * g, SSM_STATE, SSM_INNER), F32)
    return pl.pallas_call(
        _ssd_state_kernel,
        grid=(b, n),
        in_specs=[fwd(768), dt_fwd, bwd(768), dt_bwd, _full(alog.shape)],
        out_specs=[pl.BlockSpec((1, g, SSM_STATE, SSM_INNER), lambda bi, c: (bi, c, 0, 0)),
                   pl.BlockSpec((1, g, SSM_STATE, SSM_INNER), lambda bi, c: (bi, n - 1 - c, 0, 0))],
        out_shape=[st_shape, st_shape],
        scratch_shapes=[pltpu.VMEM((2, SSM_STATE, SSM_INNER), F32)],
        compiler_params=_cparams(("parallel", "arbitrary")),
        name="ssd_state",
    )(xbc, dtt, xbc, dtt, alog)


def _ssd_out_kernel(x_ref, dt_ref, dtt_ref, z_ref, sf_ref, sb_ref, alog_ref, arow_ref, dskip_ref, nw_ref, o_ref):
    gs = range(CHUNKS_PER_STEP)
    rows = [slice(g * CHUNK, (g + 1) * CHUNK) for g in gs]
    nh = SSM_HEADS
    low_t, upp_t = _tri(True), _tri(False)
    a_t = -jnp.exp(alog_ref[...])
    a_row = -jnp.exp(arow_ref[...])
    dtt = [dtt_ref[0, :, rows[g]] for g in gs]
    cumt = [jnp.concatenate([_cumsum_time_on_lanes(dtt[g][0:nh] * a_t[0:nh], low_t),
                             _cumsum_time_on_lanes(dtt[g][nh:2 * nh] * a_t[nh:2 * nh], upp_t)], axis=0) for g in gs]
    da = [dt_ref[0, rows[g], :] * a_row for g in gs]
    head_lane = lax.broadcasted_iota(jnp.int32, (CHUNK, LANES), 1)
    cum = [jnp.where(head_lane < nh, _cumsum_time_on_rows(da[g], low_t), _cumsum_time_on_rows(da[g], upp_t))
           for g in gs]
    ii = lax.broadcasted_iota(jnp.int32, (CHUNK, CHUNK), 0)
    jj = lax.broadcasted_iota(jnp.int32, (CHUNK, CHUNK), 1)
    low = ii >= jj
    upp = jj >= ii
    first = jj < SSM_HEAD_DIM
    heads_per_group = nh // SSM_GROUPS
    c0 = SSM_INNER + SSM_GROUPS * SSM_STATE
    ys = [[] for _ in gs]
    for pr in range(nh // 2):
        k = (2 * pr) // heads_per_group
        pair = slice(pr * LANES, (pr + 1) * LANES)
        cg = [x_ref[0, rows[g], c0 + k * SSM_STATE:c0 + (k + 1) * SSM_STATE].astype(BF16) for g in gs]
        bg = [x_ref[0, rows[g], SSM_INNER + k * SSM_STATE:SSM_INNER + (k + 1) * SSM_STATE].astype(BF16) for g in gs]
        cb = [_nt(cg[g], bg[g]) for g in gs]
        off = [_dot(cg[g], sf_ref[0, g, :, pair].astype(BF16)) for g in gs]
        offb = [_dot(cg[g], sb_ref[0, g, :, pair].astype(BF16)) for g in gs]
        diag = [[], []]
        ecol = [[], []]
        for e, hd in enumerate((2 * pr, 2 * pr + 1)):
            colf = [jnp.broadcast_to(cum[g][:, hd:hd + 1], (CHUNK, CHUNK)) for g in gs]
            colb = [jnp.broadcast_to(cum[g][:, nh + hd:nh + hd + 1], (CHUNK, CHUNK)) for g in gs]
            segf = [jnp.exp(jnp.where(low, colf[g] - cumt[g][hd:hd + 1], NEG_BIG)) for g in gs]
            segb = [jnp.exp(jnp.where(upp, colb[g] - cumt[g][nh + hd:nh + hd + 1], NEG_BIG)) for g in gs]
            w = [cb[g] * (segf[g] * dtt[g][hd:hd + 1] + segb[g] * dtt[g][nh + hd:nh + hd + 1]) for g in gs]
            diag[e] = [_dot(w[g].astype(BF16), x_ref[0, rows[g], pair].astype(BF16)) for g in gs]
            ecol[e] = [(jnp.exp(colf[g]), jnp.exp(colb[g])) for g in gs]
        for g in gs:
            yd = jnp.where(first, diag[0][g], diag[1][g])
            ef = jnp.where(first, ecol[0][g][0], ecol[1][g][0])
            eb = jnp.where(first, ecol[0][g][1], ecol[1][g][1])
            ys[g].append(yd + ef * off[g] + eb * offb[g])
    gw = SSM_INNER // SSM_GROUPS
    for g in gs:
        y = jnp.concatenate(ys[g], axis=1) + dskip_ref[...] * x_ref[0, rows[g], 0:SSM_INNER]
        y = y * z_ref[0, rows[g], :]
        for k in range(SSM_GROUPS):
            sl = slice(k * gw, (k + 1) * gw)
            o_ref[0, rows[g], sl] = _rms(y[:, sl], nw_ref[:, sl], HEAD_NORM_EPS).astype(BF16)


def _ssd_out(xbc, dt, dtt, z, sf, sb, alog, arow, dskip, nw):
    b, s, _ = xbc.shape
    gc = CHUNKS_PER_STEP
    n = s // (gc * CHUNK)
    tok = lambda width: pl.BlockSpec((1, gc * CHUNK, width), lambda bi, c: (bi, c, 0))
    st = pl.BlockSpec((1, gc, SSM_STATE, SSM_INNER), lambda bi, c: (bi, c, 0, 0))
    return pl.pallas_call(
        _ssd_out_kernel,
        grid=(b, n),
        in_specs=[tok(1024), tok(LANES), pl.BlockSpec((1, 2 * SSM_HEADS, gc * CHUNK), lambda bi, c: (bi, 0, c)),
                  tok(512), st, st, _full(alog.shape), _full(arow.shape), _full(dskip.shape), _full(nw.shape)],
        out_specs=tok(512),
        out_shape=jax.ShapeDtypeStruct((b, s, 512), BF16),
        compiler_params=_cparams(("parallel", "parallel")),
        name="ssd_out",
    )(xbc, dt, dtt, z, sf, sb, alog, arow, dskip, nw)


def _attn_scores_t(q, k_ref, vt_ref, seq):
    tk = min(ATTN_K_TILE, seq)
    rows = vt_ref.shape[1]

    m = jnp.full((1, ATTN_Q_TILE), NEG_BIG, F32)
    acc = jnp.zeros((rows, ATTN_Q_TILE), F32)
    n = seq // tk
    scores = lambda c: _nt(k_ref[0, c * tk:(c + 1) * tk, :], q)
    pending = [scores(c) for c in range(min(ATTN_LOOKAHEAD, n))]
    for c in range(n):
        s = pending.pop(0)
        if c + ATTN_LOOKAHEAD < n:
            pending.append(scores(c + ATTN_LOOKAHEAD))
        m_new = jnp.maximum(m, jnp.max(s, axis=0, keepdims=True))
        p = jnp.exp2(s - m_new).astype(BF16)
        acc = acc * jnp.exp2(m - m_new) + _dot(vt_ref[0, :, c * tk:(c + 1) * tk], p)
        m = m_new
    return acc


def _mla_attn_kernel(q_ref, k_ref, vt_ref, o_ref):
    tq = ATTN_Q_TILE
    for t in range(ATTN_TILES_PER_STEP):
        cols = slice(t * tq, (t + 1) * tq)
        acc = _attn_scores_t(q_ref[0, cols, :], k_ref, vt_ref, k_ref.shape[1])
        o_ref[0, :, cols] = (acc[0:MLA_V_DIM] / acc[MLA_V_DIM:MLA_V_DIM + 1]).astype(BF16)


def _mla_attn(q, k, vt):
    b, s, _ = q.shape
    tq = ATTN_Q_TILE * ATTN_TILES_PER_STEP
    return pl.pallas_call(
        _mla_attn_kernel,
        grid=(b, MLA_HEADS, s // tq),
        in_specs=[pl.BlockSpec((1, tq, HEAD_PAD), lambda bi, h, i: (bi, i, h)),
                  pl.BlockSpec((1, s, HEAD_PAD), lambda bi, h, i: (bi, 0, h)),
                  pl.BlockSpec((1, MLA_VA, s), lambda bi, h, i: (bi, h, 0))],
        out_specs=pl.BlockSpec((1, MLA_V_DIM, tq), lambda bi, h, i: (bi, h, i)),
        out_shape=jax.ShapeDtypeStruct((b, MLA_HEADS * MLA_V_DIM, s), BF16),
        compiler_params=_cparams(("parallel", "parallel", "parallel")),
        name="mla_attn",
    )(q, k, vt)


def _diff_attn_kernel(q_ref, k_ref, vt_ref, lam_ref, nw_ref, o_ref, *, lam_init):
    tq = ATTN_Q_TILE // 2
    dv = DIFF_V_DIM
    lv = lam_ref[...]
    lam = (jnp.exp(jnp.sum(lv[0:1] * lv[1:2], axis=1, keepdims=True))
           - jnp.exp(jnp.sum(lv[2:3] * lv[3:4], axis=1, keepdims=True)) + lam_init)
    for t in range(ATTN_TILES_PER_STEP):
        cols = slice(t * tq, (t + 1) * tq)
        q = q_ref[0, cols, :].astype(F32)
        lane = lax.broadcasted_iota(jnp.int32, q.shape, 1)
        first = lane < DIFF_HEAD_DIM
        both = jnp.concatenate([jnp.where(first, q, 0.0), jnp.where(first, 0.0, q)], axis=0).astype(BF16)
        acc = _attn_scores_t(both, k_ref, vt_ref, k_ref.shape[1])
        o1 = acc[0:dv, 0:tq] / acc[dv:dv + 1, 0:tq]
        o2 = acc[0:dv, tq:2 * tq] / acc[dv:dv + 1, tq:2 * tq]
        o = o1 - lam * o2
        o = o * lax.rsqrt(jnp.mean(o * o, axis=0, keepdims=True) + HEAD_NORM_EPS) * nw_ref[...]
        o_ref[0, :, cols] = (o * (1.0 - lam_init)).astype(BF16)


def _diff_attn(q, k, vt, lam_vec, nw, lam_init):
    b, s, _ = q.shape
    tq = (ATTN_Q_TILE // 2) * ATTN_TILES_PER_STEP
    return pl.pallas_call(
        functools.partial(_diff_attn_kernel, lam_init=lam_init),
        grid=(b, DIFF_HEADS, s // tq),
        in_specs=[pl.BlockSpec((1, tq, HEAD_PAD), lambda bi, h, i: (bi, i, h)),
                  pl.BlockSpec((1, s, HEAD_PAD), lambda bi, h, i: (bi, 0, h)),
                  pl.BlockSpec((1, DIFF_VA, s), lambda bi, h, i: (bi, h, 0)),
                  _full(lam_vec.shape), _full(nw.shape)],
        out_specs=pl.BlockSpec((1, DIFF_V_DIM, tq), lambda bi, h, i: (bi, h, i)),
        out_shape=jax.ShapeDtypeStruct((b, DIFF_HEADS * DIFF_V_DIM, s), BF16),
        compiler_params=_cparams(("parallel", "parallel", "parallel")),
        name="diff_attn",
    )(q, k, vt, lam_vec, nw)


def _merge_kernel(x_ref, nw_ref, wg_ref, ret_ref, mlat_ref, difft_ref, ssm_ref, wb_ref, wo_ref, o_ref):
    x = x_ref[0]
    h = _rms(x, nw_ref[...], NORM_EPS).astype(BF16)
    rows = lambda t_ref: t_ref[0].astype(F32).T.astype(BF16)
    outs = (ret_ref[0], rows(mlat_ref), rows(difft_ref), ssm_ref[0])
    merged = None
    for i in range(N_BRANCH):
        gate = _sigmoid(_dot(h, wg_ref[:, i * D_MODEL:(i + 1) * D_MODEL]))
        term = gate * _dot(outs[i], wb_ref[i])
        merged = term if merged is None else merged + term
    o_ref[0] = x + _dot(merged.astype(BF16), wo_ref[...])


def _merge(x, nw, wg, ret, mlat, difft, ssm, wb, wo):
    b, s, _ = x.shape
    tm = TOKEN_TILE
    tok = lambda width: pl.BlockSpec((1, tm, width), lambda bi, i: (bi, i, 0))
    tr = pl.BlockSpec((1, BRANCH_WIDTH, tm), lambda bi, i: (bi, 0, i))
    return pl.pallas_call(
        _merge_kernel,
        grid=(b, s // tm),
        in_specs=[tok(D_MODEL), _full((1, D_MODEL)), _full(wg.shape), tok(512), tr, tr, tok(512),
                  _full(wb.shape), _full(wo.shape)],
        out_specs=tok(D_MODEL),
        out_shape=jax.ShapeDtypeStruct((b, s, D_MODEL), F32),
        compiler_params=_cparams(("parallel", "parallel")),
        name="merge",
    )(x, nw, wg, ret, mlat, difft, ssm, wb, wo)


FFN_COL_TILE = 1408


def _ffn_kernel(xp_ref, x_ref, xn_ref, nw_ref, wgate_ref, wup_ref, cw_ref, cb_ref, wdown_ref, fnw_ref,
                o_ref, g_ref, *, final_norm):
    tm = x_ref.shape[1]
    nw = nw_ref[...]
    x = x_ref[0]
    hm = _rms(x, nw, NORM_EPS)
    hcat = jnp.concatenate([_rms(xp_ref[0], nw, NORM_EPS), hm, _rms(xn_ref[0], nw, NORM_EPS)],
                           axis=0).astype(BF16)
    hb = hm.astype(BF16)
    acc = x
    for j in range(D_FF // FFN_COL_TILE):
        cols = slice(j * FFN_COL_TILE, (j + 1) * FFN_COL_TILE)
        g_ref[...] = _dot(hcat, wgate_ref[:, cols])
        _zero_halo_at_sequence_ends(g_ref, tm)
        u = _conv3(g_ref, tm, cw_ref.at[:, cols], cb_ref.at[:, cols])
        act = (_silu(u) * _dot(hb, wup_ref[:, cols])).astype(BF16)
        acc = acc + _dot(act, wdown_ref[cols, :])
    if final_norm:
        acc = _rms(acc, fnw_ref[...], NORM_EPS)
    o_ref[0] = acc


def _ffn(x, nw, wgate, wup, cw, cb, wdown, fnw, final_norm):
    b, s, _ = x.shape
    tm = TOKEN_TILE
    return pl.pallas_call(
        functools.partial(_ffn_kernel, final_norm=final_norm),
        grid=(b, s // tm),
        in_specs=[*_halo_specs(tm, s, D_MODEL), _full((1, D_MODEL)), _full(wgate.shape), _full(wup.shape),
                  _full(cw.shape), _full(cb.shape), _full(wdown.shape), _full((1, D_MODEL))],
        out_specs=pl.BlockSpec((1, tm, D_MODEL), lambda bi, i: (bi, i, 0)),
        out_shape=jax.ShapeDtypeStruct((b, s, D_MODEL), F32),
        scratch_shapes=[pltpu.VMEM((tm + 2 * HALO, FFN_COL_TILE), F32)],
        compiler_params=_cparams(("parallel", "parallel")),
        name="ffn",
    )(x, x, x, nw, wgate, wup, cw, cb, wdown, fnw)


def _pad_axis(t, axis, size):
    pad = [(0, 0)] * t.ndim
    pad[axis] = (0, size - t.shape[axis])
    return jnp.pad(t, pad)


def _layer_params(l, p):
    w_in = p["w_in"][l]
    row = lambda t: t.reshape(1, -1).astype(F32)
    d = {}
    d["norm_mix"] = row(p["norm_mix_w"][l])
    d["w_ret"] = w_in[:, _O_RET:_O_MLA_CQ].astype(BF16)
    kr = jnp.pad(w_in[:, _O_MLA_KR:_O_DIFF_QK], ((0, 0), (MLA_NOPE_DIM, HEAD_PAD - MLA_NOPE_DIM - MLA_ROPE_DIM)))
    d["w_mla1"] = jnp.concatenate([w_in[:, _O_MLA_CQ:_O_MLA_KR], kr], axis=1).astype(BF16)
    d["mla_qnw"] = row(p["mla_q_norm_w"][l])
    d["mla_kvnw"] = row(p["mla_kv_norm_w"][l])
    uq = p["mla_w_uq"][l].reshape(MLA_Q_RANK, MLA_HEADS, MLA_NOPE_DIM + MLA_ROPE_DIM)
    d["w_mla_q"] = _pad_axis(uq, 2, HEAD_PAD).reshape(MLA_Q_RANK, MLA_HEADS * HEAD_PAD).astype(BF16)
    r1, r2 = jnp.split(uq[:, :, MLA_NOPE_DIM:], 2, axis=2)
    swapped = jnp.concatenate([jnp.zeros_like(uq[:, :, :MLA_NOPE_DIM]), -r2, r1], axis=2)
    d["w_mla_q_swap"] = _pad_axis(swapped, 2, HEAD_PAD).reshape(MLA_Q_RANK, MLA_HEADS * HEAD_PAD).astype(BF16)
    ukv = p["mla_w_ukv"][l].reshape(MLA_KV_RANK, MLA_HEADS, MLA_NOPE_DIM + MLA_V_DIM)
    d["w_mla_k"] = _pad_axis(ukv[:, :, :MLA_NOPE_DIM], 2, HEAD_PAD).reshape(
        MLA_KV_RANK, MLA_HEADS * HEAD_PAD).astype(BF16)
    vt = jnp.transpose(ukv[:, :, MLA_NOPE_DIM:], (1, 2, 0))
    d["w_mla_vt"] = _pad_axis(vt, 1, MLA_VA).reshape(MLA_HEADS * MLA_VA, MLA_KV_RANK).astype(BF16)
    d["w_diff_qk"] = w_in[:, _O_DIFF_QK:_O_DIFF_V].astype(BF16)
    dvt = w_in[:, _O_DIFF_V:_O_SSM_Z].T.reshape(DIFF_HEADS, DIFF_V_DIM, D_MODEL)
    d["w_diff_vt"] = _pad_axis(dvt, 1, DIFF_VA).reshape(DIFF_HEADS * DIFF_VA, D_MODEL).astype(BF16)
    d["diff_lambda"] = p["diff_lambda"][l].astype(F32)
    d["diff_nw"] = jnp.broadcast_to(p["diff_norm_w"][l].astype(F32)[:, None], (DIFF_V_DIM, ATTN_Q_TILE // 2))
    d["w_ssm_x"] = w_in[:, _O_SSM_XBC:_O_SSM_DT].astype(BF16)
    zdt = jnp.concatenate([w_in[:, _O_SSM_Z:_O_SSM_XBC], w_in[:, _O_SSM_DT:_O_GATE]], axis=1)
    d["w_ssm_zdt"] = _pad_axis(zdt, 1, SSM_INNER + LANES).astype(BF16)
    d["ssm_conv_w"] = p["ssm_conv_w"][l].astype(F32)
    d["ssm_conv_b"] = row(p["ssm_conv_b"][l])
    d["ssm_dt_bias"] = _pad_axis(row(p["ssm_dt_bias"][l]), 1, LANES)
    d["ssm_dt_bias_col"] = p["ssm_dt_bias"][l].astype(F32).reshape(2 * SSM_HEADS, 1)
    d["w_ssm_dtt"] = w_in[:, _O_SSM_DT:_O_GATE].T.astype(BF16)
    d["ssm_alog_row"] = _pad_axis(row(p["ssm_a_log"][l]), 1, LANES)
    d["ssm_alog"] = jnp.broadcast_to(p["ssm_a_log"][l].astype(F32).reshape(2 * SSM_HEADS, 1), (2 * SSM_HEADS, CHUNK))
    d["ssm_dskip"] = jnp.repeat(p["ssm_d"][l].astype(F32), SSM_HEAD_DIM).reshape(1, SSM_INNER)
    d["ssm_nw"] = row(p["ssm_norm_w"][l])
    d["ret_nw"] = row(p["ret_norm_w"][l])
    d["ret_log_decay"] = p["ret_log_decay"][l]
    d["w_gate"] = w_in[:, _O_GATE:_IN_COLS].astype(BF16)
    d["w_branch"] = p["w_branch"][l].astype(BF16)
    d["w_out"] = p["w_out"][l].astype(BF16)
    d["norm_ffn"] = row(p["norm_ffn_w"][l])
    d["ffn_w_gate"] = p["ffn_w_gate"][l].astype(BF16)
    d["ffn_w_up"] = p["ffn_w_up"][l].astype(BF16)
    d["ffn_conv_w"] = p["ffn_conv_w"][l].astype(F32)
    d["ffn_conv_b"] = row(p["ffn_conv_b"][l])
    d["ffn_w_down"] = p["ffn_w_down"][l].astype(BF16)
    d["final_nw"] = row(p["final_norm_w"])
    return d


def _encoder_layer(x, l, d, ropes):
    rope64, rope_mla = ropes
    rq, rk, rv, rg = _ret_in(x, d["norm_mix"], d["w_ret"], rope64)
    dmat, qdf, qdb, kdf, kdb, cdf, cdb = _ret_tables(d["ret_log_decay"])
    rsf, rsb = _ret_state(rk, rv, kdf, kdb, cdf, cdb)
    ret = _ret_out(rq, rk, rv, rg, rsf, rsb, dmat, qdf, qdb, d["ret_nw"])
    mq, mk, mvt = _mla_in(x, d["norm_mix"], d["w_mla1"], d["mla_qnw"], d["mla_kvnw"], d["w_mla_q"],
                          d["w_mla_q_swap"], d["w_mla_k"], d["w_mla_vt"], rope_mla)
    mlat = _mla_attn(mq, mk, mvt)
    dq, dk, dvt = _diff_in(x, d["norm_mix"], d["w_diff_qk"], d["w_diff_vt"], rope64)
    lam_init = 0.8 - 0.6 * math.exp(-0.3 * l)
    difft = _diff_attn(dq, dk, dvt, d["diff_lambda"], d["diff_nw"], lam_init)
    xbc, sz, dt, dtt = _ssm_in(x, d["norm_mix"], d["w_ssm_x"], d["w_ssm_zdt"], d["w_ssm_dtt"], d["ssm_conv_w"],
                               d["ssm_conv_b"], d["ssm_dt_bias"], d["ssm_dt_bias_col"])
    ssf, ssb = _ssd_state(xbc, dtt, d["ssm_alog"])
    ssm = _ssd_out(xbc, dt, dtt, sz, ssf, ssb, d["ssm_alog"], d["ssm_alog_row"], d["ssm_dskip"], d["ssm_nw"])
    x = _merge(x, d["norm_mix"], d["w_gate"], ret, mlat, difft, ssm, d["w_branch"], d["w_out"])
    return _ffn(x, d["norm_ffn"], d["ffn_w_gate"], d["ffn_w_up"], d["ffn_conv_w"], d["ffn_conv_b"],
                d["ffn_w_down"], d["final_nw"], final_norm=(l == DEPTH - 1))


def _trunk(x, layers):
    s = x.shape[1]
    ropes = (_rope_tables(s, RET_QK_DIM, RET_QK_DIM, 0),
             _rope_tables(s, MLA_ROPE_DIM, HEAD_PAD, MLA_NOPE_DIM))
    for l in range(DEPTH):
        x = _encoder_layer(x, l, layers[l], ropes)
    return x


def kernel(x_prompt, x_sample, norm_mix_w, w_in, ret_log_decay, ret_norm_w, mla_q_norm_w, mla_w_uq, mla_kv_norm_w, mla_w_ukv, diff_lambda, diff_norm_w, ssm_conv_w, ssm_conv_b, ssm_dt_bias, ssm_a_log, ssm_d, ssm_norm_w, w_branch, w_out, norm_ffn_w, ffn_w_gate, ffn_w_up, ffn_conv_w, ffn_conv_b, ffn_w_down, final_norm_w):
    p = {
        "norm_mix_w": norm_mix_w, "w_in": w_in, "ret_log_decay": ret_log_decay, "ret_norm_w": ret_norm_w,
        "mla_q_norm_w": mla_q_norm_w, "mla_w_uq": mla_w_uq, "mla_kv_norm_w": mla_kv_norm_w,
        "mla_w_ukv": mla_w_ukv, "diff_lambda": diff_lambda, "diff_norm_w": diff_norm_w,
        "ssm_conv_w": ssm_conv_w, "ssm_conv_b": ssm_conv_b, "ssm_dt_bias": ssm_dt_bias,
        "ssm_a_log": ssm_a_log, "ssm_d": ssm_d, "ssm_norm_w": ssm_norm_w, "w_branch": w_branch,
        "w_out": w_out, "norm_ffn_w": norm_ffn_w, "ffn_w_gate": ffn_w_gate, "ffn_w_up": ffn_w_up,
        "ffn_conv_w": ffn_conv_w, "ffn_conv_b": ffn_conv_b, "ffn_w_down": ffn_w_down,
        "final_norm_w": final_norm_w,
    }
    layers = [_layer_params(l, p) for l in range(DEPTH)]
    return _trunk(x_prompt, layers), _trunk(x_sample, layers)
```

```python
import functools
import math

import jax
import jax.numpy as jnp
from jax import lax
from jax.experimental import pallas as pl
from jax.experimental.pallas import tpu as pltpu

F32 = jnp.float32
BF16 = jnp.bfloat16

D_MODEL = 1024
DEPTH = 2
ROPE_THETA = 10000.0
NORM_EPS = 1e-6
HEAD_NORM_EPS = 1e-5
CHUNK = 128
N_BRANCH = 4
BRANCH_WIDTH = 512

RET_HEADS, RET_QK_DIM, RET_V_DIM = 4, 64, 128
MLA_HEADS, MLA_NOPE_DIM, MLA_ROPE_DIM, MLA_V_DIM = 8, 64, 32, 64
MLA_Q_RANK, MLA_KV_RANK = 256, 128
DIFF_HEADS, DIFF_HEAD_DIM = 4, 64
DIFF_V_DIM = 2 * DIFF_HEAD_DIM
SSM_HEADS, SSM_HEAD_DIM, SSM_GROUPS, SSM_STATE = 8, 64, 2, 128
SSM_INNER = SSM_HEADS * SSM_HEAD_DIM
SSM_CONV_CH = SSM_INNER + 2 * SSM_GROUPS * SSM_STATE
D_FF = 2816

_O_RET = 0
_O_MLA_CQ = 1536
_O_MLA_CKV = 1792
_O_MLA_KR = 1920
_O_DIFF_QK = 1952
_O_DIFF_V = 2976
_O_SSM_Z = 3488
_O_SSM_XBC = 4000
_O_SSM_DT = 5024
_O_GATE = 5040
_IN_COLS = 9136

LANES = 128
SUBLANES = 8
HALO = SUBLANES
PACK_ROWS = 16
HEAD_PAD = LANES
MLA_VA = MLA_V_DIM + PACK_ROWS
DIFF_VA = DIFF_V_DIM + PACK_ROWS
TOKEN_TILE = 512
IN_TOKEN_TILE = 1024
CHUNKS_PER_STEP = 8
ATTN_Q_TILE = 512
ATTN_K_TILE = 256
ATTN_LOOKAHEAD = 3
ATTN_TILES_PER_STEP = 4
LOG2E = 1.4426950408889634
NEG_BIG = -1e30
VMEM_LIMIT = 56 * 1024 * 1024


def _cparams(sems, vmem=VMEM_LIMIT):
    return pltpu.CompilerParams(dimension_semantics=sems, vmem_limit_bytes=vmem)


def _full(shape):
    return pl.BlockSpec(shape, lambda *_: (0,) * len(shape))


def _nt(a, b):
    return lax.dot_general(a, b, (((1,), (1,)), ((), ())), preferred_element_type=F32)


def _dot(a, b):
    return jnp.dot(a, b, preferred_element_type=F32)


def _rms(x, w, eps):
    return x * lax.rsqrt(jnp.mean(x * x, axis=-1, keepdims=True) + eps) * w


def _sigmoid(x):
    return 0.5 * jnp.tanh(0.5 * x) + 0.5


def _silu(x):
    return x * _sigmoid(x)


def _rope(x, c, s1, s2, half):
    return x * c + pltpu.roll(x, LANES - half, 1) * s1 + pltpu.roll(x, half, 1) * s2


def _rope_tables(seq, dim, block, off):
    inv_freq = 1.0 / (ROPE_THETA ** (jnp.arange(0, dim, 2, dtype=F32) / dim))
    ang = jnp.arange(seq, dtype=F32)[:, None] * inv_freq[None, :]
    cos, sin = jnp.cos(ang), jnp.sin(ang)
    half = dim // 2
    zero = jnp.zeros_like(sin)
    pre = jnp.zeros((seq, off), F32)
    post = jnp.zeros((seq, block - off - dim), F32)
    c = jnp.concatenate([pre + 1.0, cos, cos, post + 1.0], axis=1)
    s1 = jnp.concatenate([pre, -sin, zero, post], axis=1)
    s2 = jnp.concatenate([pre, zero, sin, post], axis=1)
    reps = LANES // block
    return tuple(jnp.tile(t, (1, reps)) for t in (c, s1, s2))


def _ret_in_kernel(x_ref, nw_ref, w_ref, c_ref, s1_ref, s2_ref, q_ref, k_ref, v_ref, g_ref):
    h = _rms(x_ref[0], nw_ref[...], NORM_EPS).astype(BF16)
    p = _dot(h, w_ref[...])
    c, s1, s2 = c_ref[...], s1_ref[...], s2_ref[...]
    half = RET_QK_DIM // 2
    for i in range(2):
        sl = slice(i * LANES, (i + 1) * LANES)
        q_ref[0, :, sl] = _rope(p[:, sl], c, s1, s2, half).astype(BF16)
        kk = _rope(p[:, 256 + i * LANES:256 + (i + 1) * LANES], c, s1, s2, half)
        k_ref[0, :, sl] = (kk * (RET_QK_DIM ** -0.5)).astype(BF16)
    v_ref[0] = p[:, 512:1024].astype(BF16)
    g_ref[0] = _silu(p[:, 1024:1536]).astype(BF16)


def _ret_in(x, nw, w, tabs):
    b, s, _ = x.shape
    tm = IN_TOKEN_TILE
    tok = lambda width: pl.BlockSpec((1, tm, width), lambda bi, i: (bi, i, 0))
    tab = pl.BlockSpec((tm, LANES), lambda bi, i: (i, 0))
    return pl.pallas_call(
        _ret_in_kernel,
        grid=(b, s // tm),
        in_specs=[tok(D_MODEL), _full((1, D_MODEL)), _full(w.shape), tab, tab, tab],
        out_specs=[tok(256), tok(256), tok(512), tok(512)],
        out_shape=[jax.ShapeDtypeStruct((b, s, n), BF16) for n in (256, 256, 512, 512)],
        compiler_params=_cparams(("parallel", "parallel")),
        name="ret_in",
    )(x, nw, w, *tabs)


def _ones_rows(rows, cols, period, at):
    r = lax.broadcasted_iota(jnp.int32, (rows, cols), 0)
    return jnp.where(r % period == at, 1.0, 0.0).astype(F32)


def _mla_in_kernel(x_ref, nw_ref, w1_ref, qnw_ref, kvnw_ref, wq_ref, wqsw_ref, wk_ref, wvt_ref,
                   c_ref, s1_ref, s2_ref, q_ref, k_ref, vt_ref):
    h = _rms(x_ref[0], nw_ref[...], NORM_EPS).astype(BF16)
    p = _dot(h, w1_ref[...])
    cqn = _rms(p[:, 0:256], qnw_ref[...], NORM_EPS).astype(BF16)
    ckvn = _rms(p[:, 256:384], kvnw_ref[...], NORM_EPS).astype(BF16)
    c, s1, s2 = c_ref[...], s1_ref[...], s2_ref[...]
    half = MLA_ROPE_DIM // 2
    krr = _rope(p[:, 384:512], c, s1, s2, half)
    qp = _dot(cqn, wq_ref[...])
    qsw = _dot(cqn, wqsw_ref[...])
    sin = s2 - s1
    kp = _dot(ckvn, wk_ref[...])
    qscale = ((MLA_NOPE_DIM + MLA_ROPE_DIM) ** -0.5) * LOG2E
    for hd in range(MLA_HEADS):
        sl = slice(hd * HEAD_PAD, (hd + 1) * HEAD_PAD)
        q_ref[0, :, sl] = ((qp[:, sl] * c + qsw[:, sl] * sin) * qscale).astype(BF16)
        k_ref[0, :, sl] = (kp[:, sl] + krr).astype(BF16)
    vt = _nt(wvt_ref[...], ckvn)
    vt_ref[0] = (vt + _ones_rows(vt.shape[0], vt.shape[1], MLA_VA, MLA_V_DIM)).astype(BF16)


def _mla_in(x, nw, w1, qnw, kvnw, wq, wqsw, wk, wvt, tabs):
    b, s, _ = x.shape
    tm = IN_TOKEN_TILE
    tok = lambda width: pl.BlockSpec((1, tm, width), lambda bi, i: (bi, i, 0))
    tab = pl.BlockSpec((tm, LANES), lambda bi, i: (i, 0))
    rows = MLA_HEADS * MLA_VA
    return pl.pallas_call(
        _mla_in_kernel,
        grid=(b, s // tm),
        in_specs=[tok(D_MODEL), _full((1, D_MODEL)), _full(w1.shape), _full(qnw.shape), _full(kvnw.shape),
                  _full(wq.shape), _full(wqsw.shape), _full(wk.shape), _full(wvt.shape), tab, tab, tab],
        out_specs=[tok(1024), tok(1024), pl.BlockSpec((1, rows, tm), lambda bi, i: (bi, 0, i))],
        out_shape=[jax.ShapeDtypeStruct((b, s, 1024), BF16), jax.ShapeDtypeStruct((b, s, 1024), BF16),
                   jax.ShapeDtypeStruct((b, rows, s), BF16)],
        compiler_params=_cparams(("parallel", "parallel")),
        name="mla_in",
    )(x, nw, w1, qnw, kvnw, wq, wqsw, wk, wvt, *tabs)


def _diff_in_kernel(x_ref, nw_ref, wqk_ref, wvt_ref, c_ref, s1_ref, s2_ref, q_ref, k_ref, vt_ref):
    h = _rms(x_ref[0], nw_ref[...], NORM_EPS).astype(BF16)
    p = _dot(h, wqk_ref[...])
    c, s1, s2 = c_ref[...], s1_ref[...], s2_ref[...]
    half = DIFF_HEAD_DIM // 2
    qscale = (DIFF_HEAD_DIM ** -0.5) * LOG2E
    for hd in range(DIFF_HEADS):
        sl = slice(hd * HEAD_PAD, (hd + 1) * HEAD_PAD)
        q_ref[0, :, sl] = (_rope(p[:, sl], c, s1, s2, half) * qscale).astype(BF16)
        k_ref[0, :, sl] = _rope(p[:, 512 + hd * HEAD_PAD:512 + (hd + 1) * HEAD_PAD], c, s1, s2, half).astype(BF16)
    vt = _nt(wvt_ref[...], h)
    vt_ref[0] = (vt + _ones_rows(vt.shape[0], vt.shape[1], DIFF_VA, DIFF_V_DIM)).astype(BF16)


def _diff_in(x, nw, wqk, wvt, tabs):
    b, s, _ = x.shape
    tm = IN_TOKEN_TILE
    tok = lambda width: pl.BlockSpec((1, tm, width), lambda bi, i: (bi, i, 0))
    tab = pl.BlockSpec((tm, LANES), lambda bi, i: (i, 0))
    rows = DIFF_HEADS * DIFF_VA
    return pl.pallas_call(
        _diff_in_kernel,
        grid=(b, s // tm),
        in_specs=[tok(D_MODEL), _full((1, D_MODEL)), _full(wqk.shape), _full(wvt.shape), tab, tab, tab],
        out_specs=[tok(512), tok(512), pl.BlockSpec((1, rows, tm), lambda bi, i: (bi, 0, i))],
        out_shape=[jax.ShapeDtypeStruct((b, s, 512), BF16), jax.ShapeDtypeStruct((b, s, 512), BF16),
                   jax.ShapeDtypeStruct((b, rows, s), BF16)],
        compiler_params=_cparams(("parallel", "parallel")),
        name="diff_in",
    )(x, nw, wqk, wvt, *tabs)


def _halo_specs(tm, s, width):
    per = tm // HALO
    last = s // HALO - 1
    main = pl.BlockSpec((1, tm, width), lambda bi, i: (bi, i, 0))
    prev = pl.BlockSpec((1, HALO, width), lambda bi, i: (bi, jnp.maximum(i * per - 1, 0), 0))
    nxt = pl.BlockSpec((1, HALO, width), lambda bi, i: (bi, jnp.minimum((i + 1) * per, last), 0))
    return prev, main, nxt


def _conv3(g_ref, tm, w_ref, b_ref):
    u = w_ref[0:1, :] * g_ref[pl.ds(HALO - 1, tm), :]
    u = u + w_ref[1:2, :] * g_ref[pl.ds(HALO, tm), :]
    u = u + w_ref[2:3, :] * g_ref[pl.ds(HALO + 1, tm), :]
    return u + b_ref[...]


CONV_ROW_STRIP = 64
CONV_COL_STRIP = 256


def _conv3_strips(g_ref, tm, w_ref, b_ref, emit):
    for r0 in range(0, tm, CONV_ROW_STRIP):
        for c0 in range(0, g_ref.shape[1], CONV_COL_STRIP):
            rows = slice(r0, r0 + CONV_ROW_STRIP)
            cols = slice(c0, c0 + CONV_COL_STRIP)
            u = w_ref[0:1, cols] * g_ref[HALO - 1 + r0:HALO - 1 + r0 + CONV_ROW_STRIP, cols]
            u = u + w_ref[1:2, cols] * g_ref[HALO + r0:HALO + r0 + CONV_ROW_STRIP, cols]
            u = u + w_ref[2:3, cols] * g_ref[HALO + 1 + r0:HALO + 1 + r0 + CONV_ROW_STRIP, cols]
            emit(rows, cols, u + b_ref[:, cols])


def _zero_halo_at_sequence_ends(g_ref, tm):
    i = pl.program_id(1)

    @pl.when(i == 0)
    def _():
        g_ref[0:HALO, :] = jnp.zeros((HALO, g_ref.shape[1]), F32)

    @pl.when(i == pl.num_programs(1) - 1)
    def _():
        g_ref[HALO + tm:2 * HALO + tm, :] = jnp.zeros((HALO, g_ref.shape[1]), F32)


def _softplus(t):
    return jnp.maximum(t, 0.0) + jnp.log1p(jnp.exp(-jnp.abs(t)))


def _ssm_in_kernel(xp_ref, x_ref, xn_ref, nw_ref, wx_ref, wzd_ref, wdtt_ref, cw_ref, cb_ref, dtb_ref, dtbc_ref,
                   xbc_ref, z_ref, dt_ref, dtt_ref, g_ref):
    tm = x_ref.shape[1]
    nw = nw_ref[...]
    hm = _rms(x_ref[0], nw, NORM_EPS)
    hcat = jnp.concatenate([_rms(xp_ref[0], nw, NORM_EPS), hm, _rms(xn_ref[0], nw, NORM_EPS)], axis=0)
    g_ref[...] = _dot(hcat.astype(BF16), wx_ref[...])
    _zero_halo_at_sequence_ends(g_ref, tm)
    def emit(rows, cols, u):
        xbc_ref[0, rows, cols] = _silu(u)

    _conv3_strips(g_ref, tm, cw_ref, cb_ref, emit)
    hb = hm.astype(BF16)
    zd = _dot(hb, wzd_ref[...])
    z_ref[0] = _silu(zd[:, 0:512])
    sp = _softplus(zd[:, 512:640] + dtb_ref[...])
    lane = lax.broadcasted_iota(jnp.int32, sp.shape, 1)
    dt_ref[0] = jnp.where(lane < 2 * SSM_HEADS, sp, 0.0)
    dtt_ref[0] = _softplus(_nt(wdtt_ref[...], hb) + dtbc_ref[...])


def _ssm_in(x, nw, wx, wzd, wdtt, cw, cb, dtb, dtbc):
    b, s, _ = x.shape
    tm = IN_TOKEN_TILE
    tok = lambda width: pl.BlockSpec((1, tm, width), lambda bi, i: (bi, i, 0))
    return pl.pallas_call(
        _ssm_in_kernel,
        grid=(b, s // tm),
        in_specs=[*_halo_specs(tm, s, D_MODEL), _full((1, D_MODEL)), _full(wx.shape), _full(wzd.shape),
                  _full(wdtt.shape), _full(cw.shape), _full(cb.shape), _full(dtb.shape), _full(dtbc.shape)],
        out_specs=[tok(1024), tok(512), tok(LANES), pl.BlockSpec((1, 2 * SSM_HEADS, tm), lambda bi, i: (bi, 0, i))],
        out_shape=[jax.ShapeDtypeStruct((b, s, 1024), F32), jax.ShapeDtypeStruct((b, s, 512), F32),
                   jax.ShapeDtypeStruct((b, s, LANES), F32), jax.ShapeDtypeStruct((b, 2 * SSM_HEADS, s), F32)],
        scratch_shapes=[pltpu.VMEM((tm + 2 * HALO, SSM_CONV_CH), F32)],
        compiler_params=_cparams(("parallel", "parallel")),
        name="ssm_in",
    )(x, x, x, nw, wx, wzd, wdtt, cw, cb, dtb, dtbc)


def _ret_state_kernel(kf_ref, vf_ref, kb_ref, vb_ref, kdf_ref, kdb_ref, cdf_ref, cdb_ref,
                      sf_ref, sb_ref, st_ref):
    @pl.when(pl.program_id(1) == 0)
    def _():
        st_ref[...] = jnp.zeros(st_ref.shape, F32)

    def scan(d, k_ref, v_ref, kd_ref, cd_ref, out_ref):
        incs = []
        for g in range(CHUNKS_PER_STEP):
            rows = slice(g * CHUNK, (g + 1) * CHUNK)
            kdt = (k_ref[0, rows, :].astype(F32) * kd_ref[...]).T.astype(BF16)
            v = v_ref[0, rows, :]
            incs.append(jnp.concatenate(
                [_dot(kdt[hd * RET_QK_DIM:(hd + 1) * RET_QK_DIM], v[:, hd * RET_V_DIM:(hd + 1) * RET_V_DIM])
                 for hd in range(RET_HEADS)], axis=0))
        st = st_ref[d]
        order = range(CHUNKS_PER_STEP) if d == 0 else range(CHUNKS_PER_STEP - 1, -1, -1)
        for g in order:
            out_ref[0, g] = st
            st = cd_ref[...] * st + incs[g]
        st_ref[d] = st

    scan(0, kf_ref, vf_ref, kdf_ref, cdf_ref, sf_ref)
    scan(1, kb_ref, vb_ref, kdb_ref, cdb_ref, sb_ref)


def _ret_state(k, v, kdf, kdb, cdf, cdb):
    b, s, _ = k.shape
    g = CHUNKS_PER_STEP
    n = s // (g * CHUNK)
    fwd = lambda width: pl.BlockSpec((1, g * CHUNK, width), lambda bi, c: (bi, c, 0))
    bwd = lambda width: pl.BlockSpec((1, g * CHUNK, width), lambda bi, c: (bi, n - 1 - c, 0))
    rows = RET_HEADS * RET_QK_DIM
    st_shape = jax.ShapeDtypeStruct((b, n * g, rows, RET_V_DIM), F32)
    return pl.pallas_call(
        _ret_state_kernel,
        grid=(b, n),
        in_specs=[fwd(256), fwd(512), bwd(256), bwd(512), _full(kdf.shape), _full(kdb.shape),
                  _full(cdf.shape), _full(cdb.shape)],
        out_specs=[pl.BlockSpec((1, g, rows, RET_V_DIM), lambda bi, c: (bi, c, 0, 0)),
                   pl.BlockSpec((1, g, rows, RET_V_DIM), lambda bi, c: (bi, n - 1 - c, 0, 0))],
        out_shape=[st_shape, st_shape],
        scratch_shapes=[pltpu.VMEM((2, rows, RET_V_DIM), F32)],
        compiler_params=_cparams(("parallel", "arbitrary")),
        name="ret_state",
    )(k, v, k, v, kdf, kdb, cdf, cdb)


def _ret_out_kernel(q_ref, k_ref, v_ref, g_ref, sf_ref, sb_ref, d_ref, qdf_ref, qdb_ref, nw_ref, o_ref):
    gs = range(CHUNKS_PER_STEP)
    rows = [slice(g * CHUNK, (g + 1) * CHUNK) for g in gs]
    lane = lax.broadcasted_iota(jnp.int32, (CHUNK, LANES), 1)
    q = [q_ref[0, rows[g], :].astype(F32) for g in gs]
    qf = [q[g] * qdf_ref[...] for g in gs]
    qb = [q[g] * qdb_ref[...] for g in gs]
    for hd in range(RET_HEADS):
        pair = slice((hd // 2) * LANES, (hd // 2 + 1) * LANES)
        own = (lane < RET_QK_DIM) if hd % 2 == 0 else (lane >= RET_QK_DIM)
        pick = lambda t: jnp.where(own, t[:, pair], 0.0).astype(BF16)
        vsl = slice(hd * RET_V_DIM, (hd + 1) * RET_V_DIM)
        sc = [_nt(pick(q[g]), k_ref[0, rows[g], pair]) * d_ref[hd] for g in gs]
        cross = [_dot(pick(qf[g]), sf_ref[0, g, pair, :].astype(BF16))
                 + _dot(pick(qb[g]), sb_ref[0, g, pair, :].astype(BF16)) for g in gs]
        y = [_dot(sc[g].astype(BF16), v_ref[0, rows[g], vsl]) + cross[g] for g in gs]
        for g in gs:
            yn = _rms(y[g], nw_ref[:, vsl], HEAD_NORM_EPS)
            o_ref[0, rows[g], vsl] = (g_ref[0, rows[g], vsl].astype(F32) * yn).astype(BF16)


def _ret_out(q, k, v, g, sf, sb, dmat, qdf, qdb, nw):
    b, s, _ = q.shape
    gc = CHUNKS_PER_STEP
    n = s // (gc * CHUNK)
    tok = lambda width: pl.BlockSpec((1, gc * CHUNK, width), lambda bi, c: (bi, c, 0))
    st = pl.BlockSpec((1, gc) + sf.shape[2:], lambda bi, c: (bi, c, 0, 0))
    return pl.pallas_call(
        _ret_out_kernel,
        grid=(b, n),
        in_specs=[tok(256), tok(256), tok(512), tok(512), st, st, _full(dmat.shape), _full(qdf.shape),
                  _full(qdb.shape), _full(nw.shape)],
        out_specs=tok(512),
        out_shape=jax.ShapeDtypeStruct((b, s, 512), BF16),
        compiler_params=_cparams(("parallel", "parallel")),
        name="ret_out",
    )(q, k, v, g, sf, sb, dmat, qdf, qdb, nw)


def _ret_tables(log_decay):
    lg_f = log_decay[0].astype(F32)
    lg_b = log_decay[1].astype(F32)
    pos = jnp.arange(CHUNK, dtype=F32)
    rel = pos[:, None] - pos[None, :]
    low = jnp.where(rel >= 0, jnp.exp(jnp.maximum(rel, 0.0)[None] * lg_f[:, None, None]), 0.0)
    upp = jnp.where(rel <= 0, jnp.exp(jnp.maximum(-rel, 0.0)[None] * lg_b[:, None, None]), 0.0)
    dmat = low + upp
    wide = lambda t: jnp.repeat(t, RET_QK_DIM, axis=1)
    qdf = wide(jnp.exp((pos + 1.0)[:, None] * lg_f))
    qdb = wide(jnp.exp((CHUNK - pos)[:, None] * lg_b))
    kdf = wide(jnp.exp((CHUNK - 1.0 - pos)[:, None] * lg_f))
    kdb = wide(jnp.exp(pos[:, None] * lg_b))
    tall = lambda lg: jnp.broadcast_to(jnp.repeat(jnp.exp(CHUNK * lg), RET_QK_DIM)[:, None],
                                       (RET_HEADS * RET_QK_DIM, RET_V_DIM))
    return dmat, qdf, qdb, kdf, kdb, tall(lg_f), tall(lg_b)


def _split3(x):
    hi = x.astype(BF16)
    r1 = x - hi.astype(F32)
    mid = r1.astype(BF16)
    lo = (r1 - mid.astype(F32)).astype(BF16)
    return hi, mid, lo


def _tri(lower):
    ii = lax.broadcasted_iota(jnp.int32, (CHUNK, CHUNK), 0)
    jj = lax.broadcasted_iota(jnp.int32, (CHUNK, CHUNK), 1)
    return jnp.where((ii >= jj) if lower else (jj >= ii), 1.0, 0.0).astype(BF16)


def _cumsum_time_on_lanes(xt, tri):
    hi, mid, lo = _split3(xt)
    return _nt(hi, tri) + _nt(mid, tri) + _nt(lo, tri)


def _cumsum_time_on_rows(x, tri):
    hi, mid, lo = _split3(x)
    return _dot(tri, hi) + _dot(tri, mid) + _dot(tri, lo)


def _ssd_state_kernel(xf_ref, dtf_ref, xb_ref, dtb_ref, alog_ref, sf_ref, sb_ref, st_ref):
    @pl.when(pl.program_id(1) == 0)
    def _():
        st_ref[...] = jnp.zeros(st_ref.shape, F32)

    head_of_lane = lax.broadcasted_iota(jnp.int32, (1, SSM_INNER), 1) // SSM_HEAD_DIM

    per_group = SSM_INNER // SSM_GROUPS

    def scan(d, x_ref, dt_ref, out_ref):
        gs = range(CHUNKS_PER_STEP)
        rows = [slice(g * CHUNK, (g + 1) * CHUNK) for g in gs]
        r0 = d * SSM_HEADS
        a = -jnp.exp(alog_ref[r0:r0 + SSM_HEADS, :])
        tri = _tri(lower=(d == 0))
        dtt = [dt_ref[0, r0:r0 + SSM_HEADS, rows[g]] for g in gs]
        cum = [_cumsum_time_on_lanes(dtt[g] * a, tri) for g in gs]
        endb = [jnp.broadcast_to(cum[g][:, 0:1] if d == 1 else cum[g][:, CHUNK - 1:CHUNK], cum[g].shape)
                for g in gs]
        wt = [jnp.exp(endb[g] - cum[g]) * dtt[g] for g in gs]
        edec = [jnp.exp(endb[g]) for g in gs]
        xt = [x_ref[0, rows[g], 0:SSM_INNER].T for g in gs]
        bt = [[x_ref[0, rows[g], SSM_INNER + k * SSM_STATE:SSM_INNER + (k + 1) * SSM_STATE].T.astype(BF16)
               for k in range(SSM_GROUPS)] for g in gs]
        xwt = [jnp.concatenate([xt[g][hd * SSM_HEAD_DIM:(hd + 1) * SSM_HEAD_DIM] * wt[g][hd:hd + 1]
                                for hd in range(SSM_HEADS)], axis=0).astype(BF16) for g in gs]
        inc = [jnp.concatenate([_nt(bt[g][k], xwt[g][k * per_group:(k + 1) * per_group])
                                for k in range(SSM_GROUPS)], axis=1) for g in gs]
        dec = []
        for g in gs:
            row_dec = jnp.zeros((1, SSM_INNER), F32)
            for hd in range(SSM_HEADS):
                row = jnp.concatenate([edec[g][hd:hd + 1]] * (SSM_INNER // LANES), axis=1)
                row_dec = jnp.where(head_of_lane == hd, row, row_dec)
            dec.append(row_dec)
        st = st_ref[d]
        for g in (gs if d == 0 else reversed(gs)):
            out_ref[0, g] = st
            st = st * dec[g] + inc[g]
        st_ref[d] = st

    scan(0, xf_ref, dtf_ref, sf_ref)
    scan(1, xb_ref, dtb_ref, sb_ref)


def _ssd_state(xbc, dtt, alog):
    b, s, _ = xbc.shape
    g = CHUNKS_PER_STEP
    n = s // (g * CHUNK)
    fwd = lambda width: pl.BlockSpec((1, g * CHUNK, width), lambda bi, c: (bi, c, 0))
    bwd = lambda width: pl.BlockSpec((1, g * CHUNK, width), lambda bi, c: (bi, n - 1 - c, 0))
    dt_fwd = pl.BlockSpec((1, 2 * SSM_HEADS, g * CHUNK), lambda bi, c: (bi, 0, c))
    dt_bwd = pl.BlockSpec((1, 2 * SSM_HEADS, g * CHUNK), lambda bi, c: (bi, 0, n - 1 - c))
    st_shape = jax.ShapeDtypeStruct((b, n * g, SSM_STATE, SSM_INNER), F32)
    return pl.pallas_call(
        _ssd_state_kernel,
        grid=(b, n),
        in_specs=[fwd(768), dt_fwd, bwd(768), dt_bwd, _full(alog.shape)],
        out_specs=[pl.BlockSpec((1, g, SSM_STATE, SSM_INNER), lambda bi, c: (bi, c, 0, 0)),
                   pl.BlockSpec((1, g, SSM_STATE, SSM_INNER), lambda bi, c: (bi, n - 1 - c, 0, 0))],
        out_shape=[st_shape, st_shape],
        scratch_shapes=[pltpu.VMEM((2, SSM_STATE, SSM_INNER), F32)],
        compiler_params=_cparams(("parallel", "arbitrary")),
        name="ssd_state",
    )(xbc, dtt, xbc, dtt, alog)


def _ssd_out_kernel(x_ref, dt_ref, dtt_ref, z_ref, sf_ref, sb_ref, alog_ref, arow_ref, dskip_ref, nw_ref, o_ref):
    gs = range(CHUNKS_PER_STEP)
    rows = [slice(g * CHUNK, (g + 1) * CHUNK) for g in gs]
    nh = SSM_HEADS
    low_t, upp_t = _tri(True), _tri(False)
    a_t = -jnp.exp(alog_ref[...])
    a_row = -jnp.exp(arow_ref[...])
    dtt = [dtt_ref[0, :, rows[g]] for g in gs]
    cumt = [jnp.concatenate([_cumsum_time_on_lanes(dtt[g][0:nh] * a_t[0:nh], low_t),
                             _cumsum_time_on_lanes(dtt[g][nh:2 * nh] * a_t[nh:2 * nh], upp_t)], axis=0) for g in gs]
    da = [dt_ref[0, rows[g], :] * a_row for g in gs]
    head_lane = lax.broadcasted_iota(jnp.int32, (CHUNK, LANES), 1)
    cum = [jnp.where(head_lane < nh, _cumsum_time_on_rows(da[g], low_t), _cumsum_time_on_rows(da[g], upp_t))
           for g in gs]
    ii = lax.broadcasted_iota(jnp.int32, (CHUNK, CHUNK), 0)
    jj = lax.broadcasted_iota(jnp.int32, (CHUNK, CHUNK), 1)
    low = ii >= jj
    upp = jj >= ii
    first = jj < SSM_HEAD_DIM
    heads_per_group = nh // SSM_GROUPS
    c0 = SSM_INNER + SSM_GROUPS * SSM_STATE
    ys = [[] for _ in gs]
    for pr in range(nh // 2):
        k = (2 * pr) // heads_per_group
        pair = slice(pr * LANES, (pr + 1) * LANES)
        cg = [x_ref[0, rows[g], c0 + k * SSM_STATE:c0 + (k + 1) * SSM_STATE].astype(BF16) for g in gs]
        bg = [x_ref[0, rows[g], SSM_INNER + k * SSM_STATE:SSM_INNER + (k + 1) * SSM_STATE].astype(BF16) for g in gs]
        cb = [_nt(cg[g], bg[g]) for g in gs]
        off = [_dot(cg[g], sf_ref[0, g, :, pair].astype(BF16)) for g in gs]
        offb = [_dot(cg[g], sb_ref[0, g, :, pair].astype(BF16)) for g in gs]
        diag = [[], []]
        ecol = [[], []]
        for e, hd in enumerate((2 * pr, 2 * pr + 1)):
            colf = [jnp.broadcast_to(cum[g][:, hd:hd + 1], (CHUNK, CHUNK)) for g in gs]
            colb = [jnp.broadcast_to(cum[g][:, nh + hd:nh + hd + 1], (CHUNK, CHUNK)) for g in gs]
            segf = [jnp.exp(jnp.where(low, colf[g] - cumt[g][hd:hd + 1], NEG_BIG)) for g in gs]
            segb = [jnp.exp(jnp.where(upp, colb[g] - cumt[g][nh + hd:nh + hd + 1], NEG_BIG)) for g in gs]
            w = [cb[g] * (segf[g] * dtt[g][hd:hd + 1] + segb[g] * dtt[g][nh + hd:nh + hd + 1]) for g in gs]
            diag[e] = [_dot(w[g].astype(BF16), x_ref[0, rows[g], pair].astype(BF16)) for g in gs]
            ecol[e] = [(jnp.exp(colf[g]), jnp.exp(colb[g])) for g in gs]
        for g in gs:
            yd = jnp.where(first, diag[0][g], diag[1][g])
            ef = jnp.where(first, ecol[0][g][0], ecol[1][g][0])
            eb = jnp.where(first, ecol[0][g][1], ecol[1][g][1])
            ys[g].append(yd + ef * off[g] + eb * offb[g])
    gw = SSM_INNER // SSM_GROUPS
    for g in gs:
        y = jnp.concatenate(ys[g], axis=1) + dskip_ref[...] * x_ref[0, rows[g], 0:SSM_INNER]
        y = y * z_ref[0, rows[g], :]
        for k in range(SSM_GROUPS):
            sl = slice(k * gw, (k + 1) * gw)
            o_ref[0, rows[g], sl] = _rms(y[:, sl], nw_ref[:, sl], HEAD_NORM_EPS).astype(BF16)


def _ssd_out(xbc, dt, dtt, z, sf, sb, alog, arow, dskip, nw):
    b, s, _ = xbc.shape
    gc = CHUNKS_PER_STEP
    n = s // (gc * CHUNK)
    tok = lambda width: pl.BlockSpec((1, gc * CHUNK, width), lambda bi, c: (bi, c, 0))
    st = pl.BlockSpec((1, gc, SSM_STATE, SSM_INNER), lambda bi, c: (bi, c, 0, 0))
    return pl.pallas_call(
        _ssd_out_kernel,
        grid=(b, n),
        in_specs=[tok(1024), tok(LANES), pl.BlockSpec((1, 2 * SSM_HEADS, gc * CHUNK), lambda bi, c: (bi, 0, c)),
                  tok(512), st, st, _full(alog.shape), _full(arow.shape), _full(dskip.shape), _full(nw.shape)],
        out_specs=tok(512),
        out_shape=jax.ShapeDtypeStruct((b, s, 512), BF16),
        compiler_params=_cparams(("parallel", "parallel")),
        name="ssd_out",
    )(xbc, dt, dtt, z, sf, sb, alog, arow, dskip, nw)


def _attn_scores_t(q, k_ref, vt_ref, seq):
    tk = min(ATTN_K_TILE, seq)
    rows = vt_ref.shape[1]

    m = jnp.full((1, ATTN_Q_TILE), NEG_BIG, F32)
    acc = jnp.zeros((rows, ATTN_Q_TILE), F32)
    n = seq // tk
    scores = lambda c: _nt(k_ref[0, c * tk:(c + 1) * tk, :], q)
    pending = [scores(c) for c in range(min(ATTN_LOOKAHEAD, n))]
    for c in range(n):
        s = pending.pop(0)
        if c + ATTN_LOOKAHEAD < n:
            pending.append(scores(c + ATTN_LOOKAHEAD))
        m_new = jnp.maximum(m, jnp.max(s, axis=0, keepdims=True))
        p = jnp.exp2(s - m_new).astype(BF16)
        acc = acc * jnp.exp2(m - m_new) + _dot(vt_ref[0, :, c * tk:(c + 1) * tk], p)
        m = m_new
    return acc


def _mla_attn_kernel(q_ref, k_ref, vt_ref, o_ref):
    tq = ATTN_Q_TILE
    for t in range(ATTN_TILES_PER_STEP):
        cols = slice(t * tq, (t + 1) * tq)
        acc = _attn_scores_t(q_ref[0, cols, :], k_ref, vt_ref, k_ref.shape[1])
        o_ref[0, :, cols] = (acc[0:MLA_V_DIM] / acc[MLA_V_DIM:MLA_V_DIM + 1]).astype(BF16)


def _mla_attn(q, k, vt):
    b, s, _ = q.shape
    tq = ATTN_Q_TILE * ATTN_TILES_PER_STEP
    return pl.pallas_call(
        _mla_attn_kernel,
        grid=(b, MLA_HEADS, s // tq),
        in_specs=[pl.BlockSpec((1, tq, HEAD_PAD), lambda bi, h, i: (bi, i, h)),
                  pl.BlockSpec((1, s, HEAD_PAD), lambda bi, h, i: (bi, 0, h)),
                  pl.BlockSpec((1, MLA_VA, s), lambda bi, h, i: (bi, h, 0))],
        out_specs=pl.BlockSpec((1, MLA_V_DIM, tq), lambda bi, h, i: (bi, h, i)),
        out_shape=jax.ShapeDtypeStruct((b, MLA_HEADS * MLA_V_DIM, s), BF16),
        compiler_params=_cparams(("parallel", "parallel", "parallel")),
        name="mla_attn",
    )(q, k, vt)


def _diff_attn_kernel(q_ref, k_ref, vt_ref, lam_ref, nw_ref, o_ref, *, lam_init):
    tq = ATTN_Q_TILE // 2
    dv = DIFF_V_DIM
    lv = lam_ref[...]
    lam = (jnp.exp(jnp.sum(lv[0:1] * lv[1:2], axis=1, keepdims=True))
           - jnp.exp(jnp.sum(lv[2:3] * lv[3:4], axis=1, keepdims=True)) + lam_init)
    for t in range(ATTN_TILES_PER_STEP):
        cols = slice(t * tq, (t + 1) * tq)
        q = q_ref[0, cols, :].astype(F32)
        lane = lax.broadcasted_iota(jnp.int32, q.shape, 1)
        first = lane < DIFF_HEAD_DIM
        both = jnp.concatenate([jnp.where(first, q, 0.0), jnp.where(first, 0.0, q)], axis=0).astype(BF16)
        acc = _attn_scores_t(both, k_ref, vt_ref, k_ref.shape[1])
        o1 = acc[0:dv, 0:tq] / acc[dv:dv + 1, 0:tq]
        o2 = acc[0:dv, tq:2 * tq] / acc[dv:dv + 1, tq:2 * tq]
        o = o1 - lam * o2
        o = o * lax.rsqrt(jnp.mean(o * o, axis=0, keepdims=True) + HEAD_NORM_EPS) * nw_ref[...]
        o_ref[0, :, cols] = (o * (1.0 - lam_init)).astype(BF16)


def _diff_attn(q, k, vt, lam_vec, nw, lam_init):
    b, s, _ = q.shape
    tq = (ATTN_Q_TILE // 2) * ATTN_TILES_PER_STEP
    return pl.pallas_call(
        functools.partial(_diff_attn_kernel, lam_init=lam_init),
        grid=(b, DIFF_HEADS, s // tq),
        in_specs=[pl.BlockSpec((1, tq, HEAD_PAD), lambda bi, h, i: (bi, i, h)),
                  pl.BlockSpec((1, s, HEAD_PAD), lambda bi, h, i: (bi, 0, h)),
                  pl.BlockSpec((1, DIFF_VA, s), lambda bi, h, i: (bi, h, 0)),
                  _full(lam_vec.shape), _full(nw.shape)],
        out_specs=pl.BlockSpec((1, DIFF_V_DIM, tq), lambda bi, h, i: (bi, h, i)),
        out_shape=jax.ShapeDtypeStruct((b, DIFF_HEADS * DIFF_V_DIM, s), BF16),
        compiler_params=_cparams(("parallel", "parallel", "parallel")),
        name="diff_attn",
    )(q, k, vt, lam_vec, nw)


def _merge_kernel(x_ref, nw_ref, wg_ref, ret_ref, mlat_ref, difft_ref, ssm_ref, wb_ref, wo_ref, o_ref):
    x = x_ref[0]
    h = _rms(x, nw_ref[...], NORM_EPS).astype(BF16)
    rows = lambda t_ref: t_ref[0].astype(F32).T.astype(BF16)
    outs = (ret_ref[0], rows(mlat_ref), rows(difft_ref), ssm_ref[0])
    merged = None
    for i in range(N_BRANCH):
        gate = _sigmoid(_dot(h, wg_ref[:, i * D_MODEL:(i + 1) * D_MODEL]))
        term = gate * _dot(outs[i], wb_ref[i])
        merged = term if merged is None else merged + term
    o_ref[0] = x + _dot(merged.astype(BF16), wo_ref[...])


def _merge(x, nw, wg, ret, mlat, difft, ssm, wb, wo):
    b, s, _ = x.shape
    tm = TOKEN_TILE
    tok = lambda width: pl.BlockSpec((1, tm, width), lambda bi, i: (bi, i, 0))
    tr = pl.BlockSpec((1, BRANCH_WIDTH, tm), lambda bi, i: (bi, 0, i))
    return pl.pallas_call(
        _merge_kernel,
        grid=(b, s // tm),
        in_specs=[tok(D_MODEL), _full((1, D_MODEL)), _full(wg.shape), tok(512), tr, tr, tok(512),
                  _full(wb.shape), _full(wo.shape)],
        out_specs=tok(D_MODEL),
        out_shape=jax.ShapeDtypeStruct((b, s, D_MODEL), F32),
        compiler_params=_cparams(("parallel", "parallel")),
        name="merge",
    )(x, nw, wg, ret, mlat, difft, ssm, wb, wo)


FFN_COL_TILE = 1408


def _ffn_kernel(xp_ref, x_ref, xn_ref, nw_ref, wgate_ref, wup_ref, cw_ref, cb_ref, wdown_ref, fnw_ref,
                o_ref, g_ref, *, final_norm):
    tm = x_ref.shape[1]
    nw = nw_ref[...]
    x = x_ref[0]
    hm = _rms(x, nw, NORM_EPS)
    hcat = jnp.concatenate([_rms(xp_ref[0], nw, NORM_EPS), hm, _rms(xn_ref[0], nw, NORM_EPS)],
                           axis=0).astype(BF16)
    hb = hm.astype(BF16)
    acc = x
    for j in range(D_FF // FFN_COL_TILE):
        cols = slice(j * FFN_COL_TILE, (j + 1) * FFN_COL_TILE)
        g_ref[...] = _dot(hcat, wgate_ref[:, cols])
        _zero_halo_at_sequence_ends(g_ref, tm)
        u = _conv3(g_ref, tm, cw_ref.at[:, cols], cb_ref.at[:, cols])
        act = (_silu(u) * _dot(hb, wup_ref[:, cols])).astype(BF16)
        acc = acc + _dot(act, wdown_ref[cols, :])
    if final_norm:
        acc = _rms(acc, fnw_ref[...], NORM_EPS)
    o_ref[0] = acc


def _ffn(x, nw, wgate, wup, cw, cb, wdown, fnw, final_norm):
    b, s, _ = x.shape
    tm = TOKEN_TILE
    return pl.pallas_call(
        functools.partial(_ffn_kernel, final_norm=final_norm),
        grid=(b, s // tm),
        in_specs=[*_halo_specs(tm, s, D_MODEL), _full((1, D_MODEL)), _full(wgate.shape), _full(wup.shape),
                  _full(cw.shape), _full(cb.shape), _full(wdown.shape), _full((1, D_MODEL))],
        out_specs=pl.BlockSpec((1, tm, D_MODEL), lambda bi, i: (bi, i, 0)),
        out_shape=jax.ShapeDtypeStruct((b, s, D_MODEL), F32),
        scratch_shapes=[pltpu.VMEM((tm + 2 * HALO, FFN_COL_TILE), F32)],
        compiler_params=_cparams(("parallel", "parallel")),
        name="ffn",
    )(x, x, x, nw, wgate, wup, cw, cb, wdown, fnw)


def _pad_axis(t, axis, size):
    pad = [(0, 0)] * t.ndim
    pad[axis] = (0, size - t.shape[axis])
    return jnp.pad(t, pad)


def _layer_params(l, p):
    w_in = p["w_in"][l]
    row = lambda t: t.reshape(1, -1).astype(F32)
    d = {}
    d["norm_mix"] = row(p["norm_mix_w"][l])
    d["w_ret"] = w_in[:, _O_RET:_O_MLA_CQ].astype(BF16)
    kr = jnp.pad(w_in[:, _O_MLA_KR:_O_DIFF_QK], ((0, 0), (MLA_NOPE_DIM, HEAD_PAD - MLA_NOPE_DIM - MLA_ROPE_DIM)))
    d["w_mla1"] = jnp.concatenate([w_in[:, _O_MLA_CQ:_O_MLA_KR], kr], axis=1).astype(BF16)
    d["mla_qnw"] = row(p["mla_q_norm_w"][l])
    d["mla_kvnw"] = row(p["mla_kv_norm_w"][l])
    uq = p["mla_w_uq"][l].reshape(MLA_Q_RANK, MLA_HEADS, MLA_NOPE_DIM + MLA_ROPE_DIM)
    d["w_mla_q"] = _pad_axis(uq, 2, HEAD_PAD).reshape(MLA_Q_RANK, MLA_HEADS * HEAD_PAD).astype(BF16)
    r1, r2 = jnp.split(uq[:, :, MLA_NOPE_DIM:], 2, axis=2)
    swapped = jnp.concatenate([jnp.zeros_like(uq[:, :, :MLA_NOPE_DIM]), -r2, r1], axis=2)
    d["w_mla_q_swap"] = _pad_axis(swapped, 2, HEAD_PAD).reshape(MLA_Q_RANK, MLA_HEADS * HEAD_PAD).astype(BF16)
    ukv = p["mla_w_ukv"][l].reshape(MLA_KV_RANK, MLA_HEADS, MLA_NOPE_DIM + MLA_V_DIM)
    d["w_mla_k"] = _pad_axis(ukv[:, :, :MLA_NOPE_DIM], 2, HEAD_PAD).reshape(
        MLA_KV_RANK, MLA_HEADS * HEAD_PAD).astype(BF16)
    vt = jnp.transpose(ukv[:, :, MLA_NOPE_DIM:], (1, 2, 0))
    d["w_mla_vt"] = _pad_axis(vt, 1, MLA_VA).reshape(MLA_HEADS * MLA_VA, MLA_KV_RANK).astype(BF16)
    d["w_diff_qk"] = w_in[:, _O_DIFF_QK:_O_DIFF_V].astype(BF16)
    dvt = w_in[:, _O_DIFF_V:_O_SSM_Z].T.reshape(DIFF_HEADS, DIFF_V_DIM, D_MODEL)
    d["w_diff_vt"] = _pad_axis(dvt, 1, DIFF_VA).reshape(DIFF_HEADS * DIFF_VA, D_MODEL).astype(BF16)
    d["diff_lambda"] = p["diff_lambda"][l].astype(F32)
    d["diff_nw"] = jnp.broadcast_to(p["diff_norm_w"][l].astype(F32)[:, None], (DIFF_V_DIM, ATTN_Q_TILE // 2))
    d["w_ssm_x"] = w_in[:, _O_SSM_XBC:_O_SSM_DT].astype(BF16)
    zdt = jnp.concatenate([w_in[:, _O_SSM_Z:_O_SSM_XBC], w_in[:, _O_SSM_DT:_O_GATE]], axis=1)
    d["w_ssm_zdt"] = _pad_axis(zdt, 1, SSM_INNER + LANES).astype(BF16)
    d["ssm_conv_w"] = p["ssm_conv_w"][l].astype(F32)
    d["ssm_conv_b"] = row(p["ssm_conv_b"][l])
    d["ssm_dt_bias"] = _pad_axis(row(p["ssm_dt_bias"][l]), 1, LANES)
    d["ssm_dt_bias_col"] = p["ssm_dt_bias"][l].astype(F32).reshape(2 * SSM_HEADS, 1)
    d["w_ssm_dtt"] = w_in[:, _O_SSM_DT:_O_GATE].T.astype(BF16)
    d["ssm_alog_row"] = _pad_axis(row(p["ssm_a_log"][l]), 1, LANES)
    d["ssm_alog"] = jnp.broadcast_to(p["ssm_a_log"][l].astype(F32).reshape(2 * SSM_HEADS, 1), (2 * SSM_HEADS, CHUNK))
    d["ssm_dskip"] = jnp.repeat(p["ssm_d"][l].astype(F32), SSM_HEAD_DIM).reshape(1, SSM_INNER)
    d["ssm_nw"] = row(p["ssm_norm_w"][l])
    d["ret_nw"] = row(p["ret_norm_w"][l])
    d["ret_log_decay"] = p["ret_log_decay"][l]
    d["w_gate"] = w_in[:, _O_GATE:_IN_COLS].astype(BF16)
    d["w_branch"] = p["w_branch"][l].astype(BF16)
    d["w_out"] = p["w_out"][l].astype(BF16)
    d["norm_ffn"] = row(p["norm_ffn_w"][l])
    d["ffn_w_gate"] = p["ffn_w_gate"][l].astype(BF16)
    d["ffn_w_up"] = p["ffn_w_up"][l].astype(BF16)
    d["ffn_conv_w"] = p["ffn_conv_w"][l].astype(F32)
    d["ffn_conv_b"] = row(p["ffn_conv_b"][l])
    d["ffn_w_down"] = p["ffn_w_down"][l].astype(BF16)
    d["final_nw"] = row(p["final_norm_w"])
    return d


def _encoder_layer(x, l, d, ropes):
    rope64, rope_mla = ropes
    rq, rk, rv, rg = _ret_in(x, d["norm_mix"], d["w_ret"], rope64)
    dmat, qdf, qdb, kdf, kdb, cdf, cdb = _ret_tables(d["ret_log_decay"])
    rsf, rsb = _ret_state(rk, rv, kdf, kdb, cdf, cdb)
    ret = _ret_out(rq, rk, rv, rg, rsf, rsb, dmat, qdf, qdb, d["ret_nw"])
    mq, mk, mvt = _mla_in(x, d["norm_mix"], d["w_mla1"], d["mla_qnw"], d["mla_kvnw"], d["w_mla_q"],
                          d["w_mla_q_swap"], d["w_mla_k"], d["w_mla_vt"], rope_mla)
    mlat = _mla_attn(mq, mk, mvt)
    dq, dk, dvt = _diff_in(x, d["norm_mix"], d["w_diff_qk"], d["w_diff_vt"], rope64)
    lam_init = 0.8 - 0.6 * math.exp(-0.3 * l)
    difft = _diff_attn(dq, dk, dvt, d["diff_lambda"], d["diff_nw"], lam_init)
    xbc, sz, dt, dtt = _ssm_in(x, d["norm_mix"], d["w_ssm_x"], d["w_ssm_zdt"], d["w_ssm_dtt"], d["ssm_conv_w"],
                               d["ssm_conv_b"], d["ssm_dt_bias"], d["ssm_dt_bias_col"])
    ssf, ssb = _ssd_state(xbc, dtt, d["ssm_alog"])
    ssm = _ssd_out(xbc, dt, dtt, sz, ssf, ssb, d["ssm_alog"], d["ssm_alog_row"], d["ssm_dskip"], d["ssm_nw"])
    x = _merge(x, d["norm_mix"], d["w_gate"], ret, mlat, difft, ssm, d["w_branch"], d["w_out"])
    return _ffn(x, d["norm_ffn"], d["ffn_w_gate"], d["ffn_w_up"], d["ffn_conv_w"], d["ffn_conv_b"],
                d["ffn_w_down"], d["final_nw"], final_norm=(l == DEPTH - 1))


def _trunk(x, layers):
    s = x.shape[1]
    ropes = (_rope_tables(s, RET_QK_DIM, RET_QK_DIM, 0),
             _rope_tables(s, MLA_ROPE_DIM, HEAD_PAD, MLA_NOPE_DIM))
    for l in range(DEPTH):
        x = _encoder_layer(x, l, layers[l], ropes)
    return x


def kernel(x_prompt, x_sample, norm_mix_w, w_in, ret_log_decay, ret_norm_w, mla_q_norm_w, mla_w_uq, mla_kv_norm_w, mla_w_ukv, diff_lambda, diff_norm_w, ssm_conv_w, ssm_conv_b, ssm_dt_bias, ssm_a_log, ssm_d, ssm_norm_w, w_branch, w_out, norm_ffn_w, ffn_w_gate, ffn_w_up, ffn_conv_w, ffn_conv_b, ffn_w_down, final_norm_w):
    p = {
        "norm_mix_w": norm_mix_w, "w_in": w_in, "ret_log_decay": ret_log_decay, "ret_norm_w": ret_norm_w,
        "mla_q_norm_w": mla_q_norm_w, "mla_w_uq": mla_w_uq, "mla_kv_norm_w": mla_kv_norm_w,
        "mla_w_ukv": mla_w_ukv, "diff_lambda": diff_lambda, "diff_norm_w": diff_norm_w,
        "ssm_conv_w": ssm_conv_w, "ssm_conv_b": ssm_conv_b, "ssm_dt_bias": ssm_dt_bias,
        "ssm_a_log": ssm_a_log, "ssm_d": ssm_d, "ssm_norm_w": ssm_norm_w, "w_branch": w_branch,
        "w_out": w_out, "norm_ffn_w": norm_ffn_w, "ffn_w_gate": ffn_w_gate, "ffn_w_up": ffn_w_up,
        "ffn_conv_w": ffn_conv_w, "ffn_conv_b": ffn_conv_b, "ffn_w_down": ffn_w_down,
        "final_norm_w": final_norm_w,
    }
    layers = [_layer_params(l, p) for l in range(DEPTH)]
    return _trunk(x_prompt, layers), _trunk(x_sample, layers)
```

```python
import functools
import math

import jax
import jax.numpy as jnp
from jax import lax
from jax.experimental import pallas as pl
from jax.experimental.pallas import tpu as pltpu

F32 = jnp.float32
BF16 = jnp.bfloat16

D_MODEL = 1024
DEPTH = 2
ROPE_THETA = 10000.0
NORM_EPS = 1e-6
HEAD_NORM_EPS = 1e-5
CHUNK = 128
N_BRANCH = 4
BRANCH_WIDTH = 512

RET_HEADS, RET_QK_DIM, RET_V_DIM = 4, 64, 128
MLA_HEADS, MLA_NOPE_DIM, MLA_ROPE_DIM, MLA_V_DIM = 8, 64, 32, 64
MLA_Q_RANK, MLA_KV_RANK = 256, 128
DIFF_HEADS, DIFF_HEAD_DIM = 4, 64
DIFF_V_DIM = 2 * DIFF_HEAD_DIM
SSM_HEADS, SSM_HEAD_DIM, SSM_GROUPS, SSM_STATE = 8, 64, 2, 128
SSM_INNER = SSM_HEADS * SSM_HEAD_DIM
SSM_CONV_CH = SSM_INNER + 2 * SSM_GROUPS * SSM_STATE
D_FF = 2816

_O_RET = 0
_O_MLA_CQ = 1536
_O_MLA_CKV = 1792
_O_MLA_KR = 1920
_O_DIFF_QK = 1952
_O_DIFF_V = 2976
_O_SSM_Z = 3488
_O_SSM_XBC = 4000
_O_SSM_DT = 5024
_O_GATE = 5040
_IN_COLS = 9136

LANES = 128
SUBLANES = 8
HALO = SUBLANES
PACK_ROWS = 16
HEAD_PAD = LANES
MLA_VA = MLA_V_DIM + PACK_ROWS
DIFF_VA = DIFF_V_DIM + PACK_ROWS
TOKEN_TILE = 512
IN_TOKEN_TILE = 1024
CHUNKS_PER_STEP = 8
ATTN_Q_TILE = 512
ATTN_K_TILE = 256
ATTN_LOOKAHEAD = 2
ATTN_TILES_PER_STEP = 4
LOG2E = 1.4426950408889634
NEG_BIG = -1e30
VMEM_LIMIT = 56 * 1024 * 1024


def _cparams(sems, vmem=VMEM_LIMIT):
    return pltpu.CompilerParams(dimension_semantics=sems, vmem_limit_bytes=vmem)


def _full(shape):
    return pl.BlockSpec(shape, lambda *_: (0,) * len(shape))


def _nt(a, b):
    return lax.dot_general(a, b, (((1,), (1,)), ((), ())), preferred_element_type=F32)


def _dot(a, b):
    return jnp.dot(a, b, preferred_element_type=F32)


def _rms(x, w, eps):
    return x * lax.rsqrt(jnp.mean(x * x, axis=-1, keepdims=True) + eps) * w


def _sigmoid(x):
    return 0.5 * jnp.tanh(0.5 * x) + 0.5


def _silu(x):
    return x * _sigmoid(x)


def _rope(x, c, s1, s2, half):
    return x * c + pltpu.roll(x, LANES - half, 1) * s1 + pltpu.roll(x, half, 1) * s2


def _rope_tables(seq, dim, block, off):
    inv_freq = 1.0 / (ROPE_THETA ** (jnp.arange(0, dim, 2, dtype=F32) / dim))
    ang = jnp.arange(seq, dtype=F32)[:, None] * inv_freq[None, :]
    cos, sin = jnp.cos(ang), jnp.sin(ang)
    half = dim // 2
    zero = jnp.zeros_like(sin)
    pre = jnp.zeros((seq, off), F32)
    post = jnp.zeros((seq, block - off - dim), F32)
    c = jnp.concatenate([pre + 1.0, cos, cos, post + 1.0], axis=1)
    s1 = jnp.concatenate([pre, -sin, zero, post], axis=1)
    s2 = jnp.concatenate([pre, zero, sin, post], axis=1)
    reps = LANES // block
    return tuple(jnp.tile(t, (1, reps)) for t in (c, s1, s2))


def _ret_in_kernel(x_ref, nw_ref, w_ref, c_ref, s1_ref, s2_ref, q_ref, k_ref, v_ref, g_ref):
    h = _rms(x_ref[0], nw_ref[...], NORM_EPS).astype(BF16)
    p = _dot(h, w_ref[...])
    c, s1, s2 = c_ref[...], s1_ref[...], s2_ref[...]
    half = RET_QK_DIM // 2
    for i in range(2):
        sl = slice(i * LANES, (i + 1) * LANES)
        q_ref[0, :, sl] = _rope(p[:, sl], c, s1, s2, half).astype(BF16)
        kk = _rope(p[:, 256 + i * LANES:256 + (i + 1) * LANES], c, s1, s2, half)
        k_ref[0, :, sl] = (kk * (RET_QK_DIM ** -0.5)).astype(BF16)
    v_ref[0] = p[:, 512:1024].astype(BF16)
    g_ref[0] = _silu(p[:, 1024:1536]).astype(BF16)


def _ret_in(x, nw, w, tabs):
    b, s, _ = x.shape
    tm = IN_TOKEN_TILE
    tok = lambda width: pl.BlockSpec((1, tm, width), lambda bi, i: (bi, i, 0))
    tab = pl.BlockSpec((tm, LANES), lambda bi, i: (i, 0))
    return pl.pallas_call(
        _ret_in_kernel,
        grid=(b, s // tm),
        in_specs=[tok(D_MODEL), _full((1, D_MODEL)), _full(w.shape), tab, tab, tab],
        out_specs=[tok(256), tok(256), tok(512), tok(512)],
        out_shape=[jax.ShapeDtypeStruct((b, s, n), BF16) for n in (256, 256, 512, 512)],
        compiler_params=_cparams(("parallel", "parallel")),
        name="ret_in",
    )(x, nw, w, *tabs)


def _ones_rows(rows, cols, period, at):
    r = lax.broadcasted_iota(jnp.int32, (rows, cols), 0)
    return jnp.where(r % period == at, 1.0, 0.0).astype(F32)


def _mla_in_kernel(x_ref, nw_ref, w1_ref, qnw_ref, kvnw_ref, wq_ref, wqsw_ref, wk_ref, wvt_ref,
                   c_ref, s1_ref, s2_ref, q_ref, k_ref, vt_ref):
    h = _rms(x_ref[0], nw_ref[...], NORM_EPS).astype(BF16)
    p = _dot(h, w1_ref[...])
    cqn = _rms(p[:, 0:256], qnw_ref[...], NORM_EPS).astype(BF16)
    ckvn = _rms(p[:, 256:384], kvnw_ref[...], NORM_EPS).astype(BF16)
    c, s1, s2 = c_ref[...], s1_ref[...], s2_ref[...]
    half = MLA_ROPE_DIM // 2
    krr = _rope(p[:, 384:512], c, s1, s2, half)
    qp = _dot(cqn, wq_ref[...])
    qsw = _dot(cqn, wqsw_ref[...])
    sin = s2 - s1
    kp = _dot(ckvn, wk_ref[...])
    qscale = ((MLA_NOPE_DIM + MLA_ROPE_DIM) ** -0.5) * LOG2E
    for hd in range(MLA_HEADS):
        sl = slice(hd * HEAD_PAD, (hd + 1) * HEAD_PAD)
        q_ref[0, :, sl] = ((qp[:, sl] * c + qsw[:, sl] * sin) * qscale).astype(BF16)
        k_ref[0, :, sl] = (kp[:, sl] + krr).astype(BF16)
    vt = _nt(wvt_ref[...], ckvn)
    vt_ref[0] = (vt + _ones_rows(vt.shape[0], vt.shape[1], MLA_VA, MLA_V_DIM)).astype(BF16)


def _mla_in(x, nw, w1, qnw, kvnw, wq, wqsw, wk, wvt, tabs):
    b, s, _ = x.shape
    tm = IN_TOKEN_TILE
    tok = lambda width: pl.BlockSpec((1, tm, width), lambda bi, i: (bi, i, 0))
    tab = pl.BlockSpec((tm, LANES), lambda bi, i: (i, 0))
    rows = MLA_HEADS * MLA_VA
    return pl.pallas_call(
        _mla_in_kernel,
        grid=(b, s // tm),
        in_specs=[tok(D_MODEL), _full((1, D_MODEL)), _full(w1.shape), _full(qnw.shape), _full(kvnw.shape),
                  _full(wq.shape), _full(wqsw.shape), _full(wk.shape), _full(wvt.shape), tab, tab, tab],
        out_specs=[tok(1024), tok(1024), pl.BlockSpec((1, rows, tm), lambda bi, i: (bi, 0, i))],
        out_shape=[jax.ShapeDtypeStruct((b, s, 1024), BF16), jax.ShapeDtypeStruct((b, s, 1024), BF16),
                   jax.ShapeDtypeStruct((b, rows, s), BF16)],
        compiler_params=_cparams(("parallel", "parallel")),
        name="mla_in",
    )(x, nw, w1, qnw, kvnw, wq, wqsw, wk, wvt, *tabs)


def _diff_in_kernel(x_ref, nw_ref, wqk_ref, wvt_ref, c_ref, s1_ref, s2_ref, q_ref, k_ref, vt_ref):
    h = _rms(x_ref[0], nw_ref[...], NORM_EPS).astype(BF16)
    p = _dot(h, wqk_ref[...])
    c, s1, s2 = c_ref[...], s1_ref[...], s2_ref[...]
    half = DIFF_HEAD_DIM // 2
    qscale = (DIFF_HEAD_DIM ** -0.5) * LOG2E
    for hd in range(DIFF_HEADS):
        sl = slice(hd * HEAD_PAD, (hd + 1) * HEAD_PAD)
        q_ref[0, :, sl] = (_rope(p[:, sl], c, s1, s2, half) * qscale).astype(BF16)
        k_ref[0, :, sl] = _rope(p[:, 512 + hd * HEAD_PAD:512 + (hd + 1) * HEAD_PAD], c, s1, s2, half).astype(BF16)
    vt = _nt(wvt_ref[...], h)
    vt_ref[0] = (vt + _ones_rows(vt.shape[0], vt.shape[1], DIFF_VA, DIFF_V_DIM)).astype(BF16)


def _diff_in(x, nw, wqk, wvt, tabs):
    b, s, _ = x.shape
    tm = IN_TOKEN_TILE
    tok = lambda width: pl.BlockSpec((1, tm, width), lambda bi, i: (bi, i, 0))
    tab = pl.BlockSpec((tm, LANES), lambda bi, i: (i, 0))
    rows = DIFF_HEADS * DIFF_VA
    return pl.pallas_call(
        _diff_in_kernel,
        grid=(b, s // tm),
        in_specs=[tok(D_MODEL), _full((1, D_MODEL)), _full(wqk.shape), _full(wvt.shape), tab, tab, tab],
        out_specs=[tok(512), tok(512), pl.BlockSpec((1, rows, tm), lambda bi, i: (bi, 0, i))],
        out_shape=[jax.ShapeDtypeStruct((b, s, 512), BF16), jax.ShapeDtypeStruct((b, s, 512), BF16),
                   jax.ShapeDtypeStruct((b, rows, s), BF16)],
        compiler_params=_cparams(("parallel", "parallel")),
        name="diff_in",
    )(x, nw, wqk, wvt, *tabs)


def _halo_specs(tm, s, width):
    per = tm // HALO
    last = s // HALO - 1
    main = pl.BlockSpec((1, tm, width), lambda bi, i: (bi, i, 0))
    prev = pl.BlockSpec((1, HALO, width), lambda bi, i: (bi, jnp.maximum(i * per - 1, 0), 0))
    nxt = pl.BlockSpec((1, HALO, width), lambda bi, i: (bi, jnp.minimum((i + 1) * per, last), 0))
    return prev, main, nxt


def _conv3(g_ref, tm, w_ref, b_ref):
    u = w_ref[0:1, :] * g_ref[pl.ds(HALO - 1, tm), :]
    u = u + w_ref[1:2, :] * g_ref[pl.ds(HALO, tm), :]
    u = u + w_ref[2:3, :] * g_ref[pl.ds(HALO + 1, tm), :]
    return u + b_ref[...]


CONV_ROW_STRIP = 64
CONV_COL_STRIP = 256


def _conv3_strips(g_ref, tm, w_ref, b_ref, emit):
    for r0 in range(0, tm, CONV_ROW_STRIP):
        for c0 in range(0, g_ref.shape[1], CONV_COL_STRIP):
            rows = slice(r0, r0 + CONV_ROW_STRIP)
            cols = slice(c0, c0 + CONV_COL_STRIP)
            u = w_ref[0:1, cols] * g_ref[HALO - 1 + r0:HALO - 1 + r0 + CONV_ROW_STRIP, cols]
            u = u + w_ref[1:2, cols] * g_ref[HALO + r0:HALO + r0 + CONV_ROW_STRIP, cols]
            u = u + w_ref[2:3, cols] * g_ref[HALO + 1 + r0:HALO + 1 + r0 + CONV_ROW_STRIP, cols]
            emit(rows, cols, u + b_ref[:, cols])


def _zero_halo_at_sequence_ends(g_ref, tm):
    i = pl.program_id(1)

    @pl.when(i == 0)
    def _():
        g_ref[0:HALO, :] = jnp.zeros((HALO, g_ref.shape[1]), F32)

    @pl.when(i == pl.num_programs(1) - 1)
    def _():
        g_ref[HALO + tm:2 * HALO + tm, :] = jnp.zeros((HALO, g_ref.shape[1]), F32)


def _softplus(t):
    return jnp.maximum(t, 0.0) + jnp.log1p(jnp.exp(-jnp.abs(t)))


def _ssm_in_kernel(xp_ref, x_ref, xn_ref, nw_ref, wx_ref, wzd_ref, wdtt_ref, cw_ref, cb_ref, dtb_ref, dtbc_ref,
                   xbc_ref, z_ref, dt_ref, dtt_ref, g_ref):
    tm = x_ref.shape[1]
    nw = nw_ref[...]
    hm = _rms(x_ref[0], nw, NORM_EPS)
    hcat = jnp.concatenate([_rms(xp_ref[0], nw, NORM_EPS), hm, _rms(xn_ref[0], nw, NORM_EPS)], axis=0)
    g_ref[...] = _dot(hcat.astype(BF16), wx_ref[...])
    _zero_halo_at_sequence_ends(g_ref, tm)
    def emit(rows, cols, u):
        xbc_ref[0, rows, cols] = _silu(u)

    _conv3_strips(g_ref, tm, cw_ref, cb_ref, emit)
    hb = hm.astype(BF16)
    zd = _dot(hb, wzd_ref[...])
    z_ref[0] = _silu(zd[:, 0:512])
    sp = _softplus(zd[:, 512:640] + dtb_ref[...])
    lane = lax.broadcasted_iota(jnp.int32, sp.shape, 1)
    dt_ref[0] = jnp.where(lane < 2 * SSM_HEADS, sp, 0.0)
    dtt_ref[0] = _softplus(_nt(wdtt_ref[...], hb) + dtbc_ref[...])


def _ssm_in(x, nw, wx, wzd, wdtt, cw, cb, dtb, dtbc):
    b, s, _ = x.shape
    tm = IN_TOKEN_TILE
    tok = lambda width: pl.BlockSpec((1, tm, width), lambda bi, i: (bi, i, 0))
    return pl.pallas_call(
        _ssm_in_kernel,
        grid=(b, s // tm),
        in_specs=[*_halo_specs(tm, s, D_MODEL), _full((1, D_MODEL)), _full(wx.shape), _full(wzd.shape),
                  _full(wdtt.shape), _full(cw.shape), _full(cb.shape), _full(dtb.shape), _full(dtbc.shape)],
        out_specs=[tok(1024), tok(512), tok(LANES), pl.BlockSpec((1, 2 * SSM_HEADS, tm), lambda bi, i: (bi, 0, i))],
        out_shape=[jax.ShapeDtypeStruct((b, s, 1024), F32), jax.ShapeDtypeStruct((b, s, 512), F32),
                   jax.ShapeDtypeStruct((b, s, LANES), F32), jax.ShapeDtypeStruct((b, 2 * SSM_HEADS, s), F32)],
        scratch_shapes=[pltpu.VMEM((tm + 2 * HALO, SSM_CONV_CH), F32)],
        compiler_params=_cparams(("parallel", "parallel")),
        name="ssm_in",
    )(x, x, x, nw, wx, wzd, wdtt, cw, cb, dtb, dtbc)


def _ret_state_kernel(kf_ref, vf_ref, kb_ref, vb_ref, kdf_ref, kdb_ref, cdf_ref, cdb_ref,
                      sf_ref, sb_ref, st_ref):
    @pl.when(pl.program_id(1) == 0)
    def _():
        st_ref[...] = jnp.zeros(st_ref.shape, F32)

    def scan(d, k_ref, v_ref, kd_ref, cd_ref, out_ref):
        incs = []
        for g in range(CHUNKS_PER_STEP):
            rows = slice(g * CHUNK, (g + 1) * CHUNK)
            kdt = (k_ref[0, rows, :].astype(F32) * kd_ref[...]).T.astype(BF16)
            v = v_ref[0, rows, :]
            incs.append(jnp.concatenate(
                [_dot(kdt[hd * RET_QK_DIM:(hd + 1) * RET_QK_DIM], v[:, hd * RET_V_DIM:(hd + 1) * RET_V_DIM])
                 for hd in range(RET_HEADS)], axis=0))
        st = st_ref[d]
        order = range(CHUNKS_PER_STEP) if d == 0 else range(CHUNKS_PER_STEP - 1, -1, -1)
        for g in order:
            out_ref[0, g] = st
            st = cd_ref[...] * st + incs[g]
        st_ref[d] = st

    scan(0, kf_ref, vf_ref, kdf_ref, cdf_ref, sf_ref)
    scan(1, kb_ref, vb_ref, kdb_ref, cdb_ref, sb_ref)


def _ret_state(k, v, kdf, kdb, cdf, cdb):
    b, s, _ = k.shape
    g = CHUNKS_PER_STEP
    n = s // (g * CHUNK)
    fwd = lambda width: pl.BlockSpec((1, g * CHUNK, width), lambda bi, c: (bi, c, 0))
    bwd = lambda width: pl.BlockSpec((1, g * CHUNK, width), lambda bi, c: (bi, n - 1 - c, 0))
    rows = RET_HEADS * RET_QK_DIM
    st_shape = jax.ShapeDtypeStruct((b, n * g, rows, RET_V_DIM), F32)
    return pl.pallas_call(
        _ret_state_kernel,
        grid=(b, n),
        in_specs=[fwd(256), fwd(512), bwd(256), bwd(512), _full(kdf.shape), _full(kdb.shape),
                  _full(cdf.shape), _full(cdb.shape)],
        out_specs=[pl.BlockSpec((1, g, rows, RET_V_DIM), lambda bi, c: (bi, c, 0, 0)),
                   pl.BlockSpec((1, g, rows, RET_V_DIM), lambda bi, c: (bi, n - 1 - c, 0, 0))],
        out_shape=[st_shape, st_shape],
        scratch_shapes=[pltpu.VMEM((2, rows, RET_V_DIM), F32)],
        compiler_params=_cparams(("parallel", "arbitrary")),
        name="ret_state",
    )(k, v, k, v, kdf, kdb, cdf, cdb)


def _ret_out_kernel(q_ref, k_ref, v_ref, g_ref, sf_ref, sb_ref, d_ref, qdf_ref, qdb_ref, nw_ref, o_ref):
    gs = range(CHUNKS_PER_STEP)
    rows = [slice(g * CHUNK, (g + 1) * CHUNK) for g in gs]
    lane = lax.broadcasted_iota(jnp.int32, (CHUNK, LANES), 1)
    q = [q_ref[0, rows[g], :].astype(F32) for g in gs]
    qf = [q[g] * qdf_ref[...] for g in gs]
    qb = [q[g] * qdb_ref[...] for g in gs]
    for hd in range(RET_HEADS):
        pair = slice((hd // 2) * LANES, (hd // 2 + 1) * LANES)
        own = (lane < RET_QK_DIM) if hd % 2 == 0 else (lane >= RET_QK_DIM)
        pick = lambda t: jnp.where(own, t[:, pair], 0.0).astype(BF16)
        vsl = slice(hd * RET_V_DIM, (hd + 1) * RET_V_DIM)
        sc = [_nt(pick(q[g]), k_ref[0, rows[g], pair]) * d_ref[hd] for g in gs]
        cross = [_dot(pick(qf[g]), sf_ref[0, g, pair, :].astype(BF16))
                 + _dot(pick(qb[g]), sb_ref[0, g, pair, :].astype(BF16)) for g in gs]
        y = [_dot(sc[g].astype(BF16), v_ref[0, rows[g], vsl]) + cross[g] for g in gs]
        for g in gs:
            yn = _rms(y[g], nw_ref[:, vsl], HEAD_NORM_EPS)
            o_ref[0, rows[g], vsl] = (g_ref[0, rows[g], vsl].astype(F32) * yn).astype(BF16)


def _ret_out(q, k, v, g, sf, sb, dmat, qdf, qdb, nw):
    b, s, _ = q.shape
    gc = CHUNKS_PER_STEP
    n = s // (gc * CHUNK)
    tok = lambda width: pl.BlockSpec((1, gc * CHUNK, width), lambda bi, c: (bi, c, 0))
    st = pl.BlockSpec((1, gc) + sf.shape[2:], lambda bi, c: (bi, c, 0, 0))
    return pl.pallas_call(
        _ret_out_kernel,
        grid=(b, n),
        in_specs=[tok(256), tok(256), tok(512), tok(512), st, st, _full(dmat.shape), _full(qdf.shape),
                  _full(qdb.shape), _full(nw.shape)],
        out_specs=tok(512),
        out_shape=jax.ShapeDtypeStruct((b, s, 512), BF16),
        compiler_params=_cparams(("parallel", "parallel")),
        name="ret_out",
    )(q, k, v, g, sf, sb, dmat, qdf, qdb, nw)


def _ret_tables(log_decay):
    lg_f = log_decay[0].astype(F32)
    lg_b = log_decay[1].astype(F32)
    pos = jnp.arange(CHUNK, dtype=F32)
    rel = pos[:, None] - pos[None, :]
    low = jnp.where(rel >= 0, jnp.exp(jnp.maximum(rel, 0.0)[None] * lg_f[:, None, None]), 0.0)
    upp = jnp.where(rel <= 0, jnp.exp(jnp.maximum(-rel, 0.0)[None] * lg_b[:, None, None]), 0.0)
    dmat = low + upp
    wide = lambda t: jnp.repeat(t, RET_QK_DIM, axis=1)
    qdf = wide(jnp.exp((pos + 1.0)[:, None] * lg_f))
    qdb = wide(jnp.exp((CHUNK - pos)[:, None] * lg_b))
    kdf = wide(jnp.exp((CHUNK - 1.0 - pos)[:, None] * lg_f))
    kdb = wide(jnp.exp(pos[:, None] * lg_b))
    tall = lambda lg: jnp.broadcast_to(jnp.repeat(jnp.exp(CHUNK * lg), RET_QK_DIM)[:, None],
                                       (RET_HEADS * RET_QK_DIM, RET_V_DIM))
    return dmat, qdf, qdb, kdf, kdb, tall(lg_f), tall(lg_b)


def _split3(x):
    hi = x.astype(BF16)
    r1 = x - hi.astype(F32)
    mid = r1.astype(BF16)
    lo = (r1 - mid.astype(F32)).astype(BF16)
    return hi, mid, lo


def _tri(lower):
    ii = lax.broadcasted_iota(jnp.int32, (CHUNK, CHUNK), 0)
    jj = lax.broadcasted_iota(jnp.int32, (CHUNK, CHUNK), 1)
    return jnp.where((ii >= jj) if lower else (jj >= ii), 1.0, 0.0).astype(BF16)


def _cumsum_time_on_lanes(xt, tri):
    hi, mid, lo = _split3(xt)
    return _nt(hi, tri) + _nt(mid, tri) + _nt(lo, tri)


def _cumsum_time_on_rows(x, tri):
    hi, mid, lo = _split3(x)
    return _dot(tri, hi) + _dot(tri, mid) + _dot(tri, lo)


def _ssd_state_kernel(xf_ref, dtf_ref, xb_ref, dtb_ref, alog_ref, sf_ref, sb_ref, st_ref):
    @pl.when(pl.program_id(1) == 0)
    def _():
        st_ref[...] = jnp.zeros(st_ref.shape, F32)

    head_of_lane = lax.broadcasted_iota(jnp.int32, (1, SSM_INNER), 1) // SSM_HEAD_DIM

    per_group = SSM_INNER // SSM_GROUPS

    def scan(d, x_ref, dt_ref, out_ref):
        gs = range(CHUNKS_PER_STEP)
        rows = [slice(g * CHUNK, (g + 1) * CHUNK) for g in gs]
        r0 = d * SSM_HEADS
        a = -jnp.exp(alog_ref[r0:r0 + SSM_HEADS, :])
        tri = _tri(lower=(d == 0))
        dtt = [dt_ref[0, r0:r0 + SSM_HEADS, rows[g]] for g in gs]
        cum = [_cumsum_time_on_lanes(dtt[g] * a, tri) for g in gs]
        endb = [jnp.broadcast_to(cum[g][:, 0:1] if d == 1 else cum[g][:, CHUNK - 1:CHUNK], cum[g].shape)
                for g in gs]
        wt = [jnp.exp(endb[g] - cum[g]) * dtt[g] for g in gs]
        edec = [jnp.exp(endb[g]) for g in gs]
        xt = [x_ref[0, rows[g], 0:SSM_INNER].T for g in gs]
        bt = [[x_ref[0, rows[g], SSM_INNER + k * SSM_STATE:SSM_INNER + (k + 1) * SSM_STATE].T.astype(BF16)
               for k in range(SSM_GROUPS)] for g in gs]
        xwt = [jnp.concatenate([xt[g][hd * SSM_HEAD_DIM:(hd + 1) * SSM_HEAD_DIM] * wt[g][hd:hd + 1]
                                for hd in range(SSM_HEADS)], axis=0).astype(BF16) for g in gs]
        inc = [jnp.concatenate([_nt(bt[g][k], xwt[g][k * per_group:(k + 1) * per_group])
                                for k in range(SSM_GROUPS)], axis=1) for g in gs]
        dec = []
        for g in gs:
            row_dec = jnp.zeros((1, SSM_INNER), F32)
            for hd in range(SSM_HEADS):
                row = jnp.concatenate([edec[g][hd:hd + 1]] * (SSM_INNER // LANES), axis=1)
                row_dec = jnp.where(head_of_lane == hd, row, row_dec)
            dec.append(row_dec)
        st = st_ref[d]
        for g in (gs if d == 0 else reversed(gs)):
            out_ref[0, g] = st
            st = st * dec[g] + inc[g]
        st_ref[d] = st

    scan(0, xf_ref, dtf_ref, sf_ref)
    scan(1, xb_ref, dtb_ref, sb_ref)


def _ssd_state(xbc, dtt, alog):
    b, s, _ = xbc.shape
    g = CHUNKS_PER_STEP
    n = s // (g * CHUNK)
    fwd = lambda width: pl.BlockSpec((1, g * CHUNK, width), lambda bi, c: (bi, c, 0))
    bwd = lambda width: pl.BlockSpec((1, g * CHUNK, width), lambda bi, c: (bi, n - 1 - c, 0))
    dt_fwd = pl.BlockSpec((1, 2 * SSM_HEADS, g * CHUNK), lambda bi, c: (bi, 0, c))
    dt_bwd = pl.BlockSpec((1, 2 * SSM_HEADS, g * CHUNK), lambda bi, c: (bi, 0, n - 1 - c))
    st_shape = jax.ShapeDtypeStruct((b, n * g, SSM_STATE, SSM_INNER), F32)
    return pl.pallas_call(
        _ssd_state_kernel,
        grid=(b, n),
        in_specs=[fwd(768), dt_fwd, bwd(768), dt_bwd, _full(alog.shape)],
        out_specs=[pl.BlockSpec((1, g, SSM_STATE, SSM_INNER), lambda bi, c: (bi, c, 0, 0)),
                   pl.BlockSpec((1, g, SSM_STATE, SSM_INNER), lambda bi, c: (bi, n - 1 - c, 0, 0))],
        out_shape=[st_shape, st_shape],
        scratch_shapes=[pltpu.VMEM((2, SSM_STATE, SSM_INNER), F32)],
        compiler_params=_cparams(("parallel", "arbitrary")),
        name="ssd_state",
    )(xbc, dtt, xbc, dtt, alog)


def _ssd_out_kernel(x_ref, dt_ref, dtt_ref, z_ref, sf_ref, sb_ref, alog_ref, arow_ref, dskip_ref, nw_ref, o_ref):
    gs = range(CHUNKS_PER_STEP)
    rows = [slice(g * CHUNK, (g + 1) * CHUNK) for g in gs]
    nh = SSM_HEADS
    low_t, upp_t = _tri(True), _tri(False)
    a_t = -jnp.exp(alog_ref[...])
    a_row = -jnp.exp(arow_ref[...])
    dtt = [dtt_ref[0, :, rows[g]] for g in gs]
    cumt = [jnp.concatenate([_cumsum_time_on_lanes(dtt[g][0:nh] * a_t[0:nh], low_t),
                             _cumsum_time_on_lanes(dtt[g][nh:2 * nh] * a_t[nh:2 * nh], upp_t)], axis=0) for g in gs]
    da = [dt_ref[0, rows[g], :] * a_row for g in gs]
    head_lane = lax.broadcasted_iota(jnp.int32, (CHUNK, LANES), 1)
    cum = [jnp.where(head_lane < nh, _cumsum_time_on_rows(da[g], low_t), _cumsum_time_on_rows(da[g], upp_t))
           for g in gs]
    ii = lax.broadcasted_iota(jnp.int32, (CHUNK, CHUNK), 0)
    jj = lax.broadcasted_iota(jnp.int32, (CHUNK, CHUNK), 1)
    low = ii >= jj
    upp = jj >= ii
    first = jj < SSM_HEAD_DIM
    heads_per_group = nh // SSM_GROUPS
    c0 = SSM_INNER + SSM_GROUPS * SSM_STATE
    ys = [[] for _ in gs]
    for pr in range(nh // 2):
        k = (2 * pr) // heads_per_group
        pair = slice(pr * LANES, (pr + 1) * LANES)
        cg = [x_ref[0, rows[g], c0 + k * SSM_STATE:c0 + (k + 1) * SSM_STATE].astype(BF16) for g in gs]
        bg = [x_ref[0, rows[g], SSM_INNER + k * SSM_STATE:SSM_INNER + (k + 1) * SSM_STATE].astype(BF16) for g in gs]
        cb = [_nt(cg[g], bg[g]) for g in gs]
        off = [_dot(cg[g], sf_ref[0, g, :, pair].astype(BF16)) for g in gs]
        offb = [_dot(cg[g], sb_ref[0, g, :, pair].astype(BF16)) for g in gs]
        diag = [[], []]
        ecol = [[], []]
        for e, hd in enumerate((2 * pr, 2 * pr + 1)):
            colf = [jnp.broadcast_to(cum[g][:, hd:hd + 1], (CHUNK, CHUNK)) for g in gs]
            colb = [jnp.broadcast_to(cum[g][:, nh + hd:nh + hd + 1], (CHUNK, CHUNK)) for g in gs]
            segf = [jnp.exp(jnp.where(low, colf[g] - cumt[g][hd:hd + 1], NEG_BIG)) for g in gs]
            segb = [jnp.exp(jnp.where(upp, colb[g] - cumt[g][nh + hd:nh + hd + 1], NEG_BIG)) for g in gs]
            w = [cb[g] * (segf[g] * dtt[g][hd:hd + 1] + segb[g] * dtt[g][nh + hd:nh + hd + 1]) for g in gs]
            diag[e] = [_dot(w[g].astype(BF16), x_ref[0, rows[g], pair].astype(BF16)) for g in gs]
            ecol[e] = [(jnp.exp(colf[g]), jnp.exp(colb[g])) for g in gs]
        for g in gs:
            yd = jnp.where(first, diag[0][g], diag[1][g])
            ef = jnp.where(first, ecol[0][g][0], ecol[1][g][0])
            eb = jnp.where(first, ecol[0][g][1], ecol[1][g][1])
            ys[g].append(yd + ef * off[g] + eb * offb[g])
    gw = SSM_INNER // SSM_GROUPS
    for g in gs:
        y = jnp.concatenate(ys[g], axis=1) + dskip_ref[...] * x_ref[0, rows[g], 0:SSM_INNER]
        y = y * z_ref[0, rows[g], :]
        for k in range(SSM_GROUPS):
            sl = slice(k * gw, (k + 1) * gw)
            o_ref[0, rows[g], sl] = _rms(y[:, sl], nw_ref[:, sl], HEAD_NORM_EPS).astype(BF16)


def _ssd_out(xbc, dt, dtt, z, sf, sb, alog, arow, dskip, nw):
    b, s, _ = xbc.shape
    gc = CHUNKS_PER_STEP
    n = s // (gc * CHUNK)
    tok = lambda width: pl.BlockSpec((1, gc * CHUNK, width), lambda bi, c: (bi, c, 0))
    st = pl.BlockSpec((1, gc, SSM_STATE, SSM_INNER), lambda bi, c: (bi, c, 0, 0))
    return pl.pallas_call(
        _ssd_out_kernel,
        grid=(b, n),
        in_specs=[tok(1024), tok(LANES), pl.BlockSpec((1, 2 * SSM_HEADS, gc * CHUNK), lambda bi, c: (bi, 0, c)),
                  tok(512), st, st, _full(alog.shape), _full(arow.shape), _full(dskip.shape), _full(nw.shape)],
        out_specs=tok(512),
        out_shape=jax.ShapeDtypeStruct((b, s, 512), BF16),
        compiler_params=_cparams(("parallel", "parallel")),
        name="ssd_out",
    )(xbc, dt, dtt, z, sf, sb, alog, arow, dskip, nw)


def _attn_scores_t(q, k_ref, vt_ref, seq):
    tk = min(ATTN_K_TILE, seq)
    rows = vt_ref.shape[1]

    m = jnp.full((1, ATTN_Q_TILE), NEG_BIG, F32)
    acc = jnp.zeros((rows, ATTN_Q_TILE), F32)
    n = seq // tk
    scores = lambda c: _nt(k_ref[0, c * tk:(c + 1) * tk, :], q)
    pending = [scores(c) for c in range(min(ATTN_LOOKAHEAD, n))]
    for c in range(n):
        s = pending.pop(0)
        if c + ATTN_LOOKAHEAD < n:
            pending.append(scores(c + ATTN_LOOKAHEAD))
        m_new = jnp.maximum(m, jnp.max(s, axis=0, keepdims=True))
        p = jnp.exp2(s - m_new).astype(BF16)
        acc = acc * jnp.exp2(m - m_new) + _dot(vt_ref[0, :, c * tk:(c + 1) * tk], p)
        m = m_new
    return acc


def _mla_attn_kernel(q_ref, k_ref, vt_ref, o_ref):
    tq = ATTN_Q_TILE
    for t in range(ATTN_TILES_PER_STEP):
        cols = slice(t * tq, (t + 1) * tq)
        acc = _attn_scores_t(q_ref[0, cols, :], k_ref, vt_ref, k_ref.shape[1])
        o_ref[0, :, cols] = (acc[0:MLA_V_DIM] / acc[MLA_V_DIM:MLA_V_DIM + 1]).astype(BF16)


def _mla_attn(q, k, vt):
    b, s, _ = q.shape
    tq = ATTN_Q_TILE * ATTN_TILES_PER_STEP
    return pl.pallas_call(
        _mla_attn_kernel,
        grid=(b, MLA_HEADS, s // tq),
        in_specs=[pl.BlockSpec((1, tq, HEAD_PAD), lambda bi, h, i: (bi, i, h)),
                  pl.BlockSpec((1, s, HEAD_PAD), lambda bi, h, i: (bi, 0, h)),
                  pl.BlockSpec((1, MLA_VA, s), lambda bi, h, i: (bi, h, 0))],
        out_specs=pl.BlockSpec((1, MLA_V_DIM, tq), lambda bi, h, i: (bi, h, i)),
        out_shape=jax.ShapeDtypeStruct((b, MLA_HEADS * MLA_V_DIM, s), BF16),
        compiler_params=_cparams(("parallel", "parallel", "parallel")),
        name="mla_attn",
    )(q, k, vt)


def _diff_attn_kernel(q_ref, k_ref, vt_ref, lam_ref, nw_ref, o_ref, *, lam_init):
    tq = ATTN_Q_TILE // 2
    dv = DIFF_V_DIM
    lv = lam_ref[...]
    lam = (jnp.exp(jnp.sum(lv[0:1] * lv[1:2], axis=1, keepdims=True))
           - jnp.exp(jnp.sum(lv[2:3] * lv[3:4], axis=1, keepdims=True)) + lam_init)
    for t in range(ATTN_TILES_PER_STEP):
        cols = slice(t * tq, (t + 1) * tq)
        q = q_ref[0, cols, :].astype(F32)
        lane = lax.broadcasted_iota(jnp.int32, q.shape, 1)
        first = lane < DIFF_HEAD_DIM
        both = jnp.concatenate([jnp.where(first, q, 0.0), jnp.where(first, 0.0, q)], axis=0).astype(BF16)
        acc = _attn_scores_t(both, k_ref, vt_ref, k_ref.shape[1])
        o1 = acc[0:dv, 0:tq] / acc[dv:dv + 1, 0:tq]
        o2 = acc[0:dv, tq:2 * tq] / acc[dv:dv + 1, tq:2 * tq]
        o = o1 - lam * o2
        o = o * lax.rsqrt(jnp.mean(o * o, axis=0, keepdims=True) + HEAD_NORM_EPS) * nw_ref[...]
        o_ref[0, :, cols] = (o * (1.0 - lam_init)).astype(BF16)


def _diff_attn(q, k, vt, lam_vec, nw, lam_init):
    b, s, _ = q.shape
    tq = (ATTN_Q_TILE // 2) * ATTN_TILES_PER_STEP
    return pl.pallas_call(
        functools.partial(_diff_attn_kernel, lam_init=lam_init),
        grid=(b, DIFF_HEADS, s // tq),
        in_specs=[pl.BlockSpec((1, tq, HEAD_PAD), lambda bi, h, i: (bi, i, h)),
                  pl.BlockSpec((1, s, HEAD_PAD), lambda bi, h, i: (bi, 0, h)),
                  pl.BlockSpec((1, DIFF_VA, s), lambda bi, h, i: (bi, h, 0)),
                  _full(lam_vec.shape), _full(nw.shape)],
        out_specs=pl.BlockSpec((1, DIFF_V_DIM, tq), lambda bi, h, i: (bi, h, i)),
        out_shape=jax.ShapeDtypeStruct((b, DIFF_HEADS * DIFF_V_DIM, s), BF16),
        compiler_params=_cparams(("parallel", "parallel", "parallel")),
        name="diff_attn",
    )(q, k, vt, lam_vec, nw)


def _merge_kernel(x_ref, nw_ref, wg_ref, ret_ref, mlat_ref, difft_ref, ssm_ref, wb_ref, wo_ref, o_ref):
    x = x_ref[0]
    h = _rms(x, nw_ref[...], NORM_EPS).astype(BF16)
    rows = lambda t_ref: t_ref[0].astype(F32).T.astype(BF16)
    outs = (ret_ref[0], rows(mlat_ref), rows(difft_ref), ssm_ref[0])
    merged = None
    for i in range(N_BRANCH):
        gate = _sigmoid(_dot(h, wg_ref[:, i * D_MODEL:(i + 1) * D_MODEL]))
        term = gate * _dot(outs[i], wb_ref[i])
        merged = term if merged is None else merged + term
    o_ref[0] = x + _dot(merged.astype(BF16), wo_ref[...])


def _merge(x, nw, wg, ret, mlat, difft, ssm, wb, wo):
    b, s, _ = x.shape
    tm = TOKEN_TILE
    tok = lambda width: pl.BlockSpec((1, tm, width), lambda bi, i: (bi, i, 0))
    tr = pl.BlockSpec((1, BRANCH_WIDTH, tm), lambda bi, i: (bi, 0, i))
    return pl.pallas_call(
        _merge_kernel,
        grid=(b, s // tm),
        in_specs=[tok(D_MODEL), _full((1, D_MODEL)), _full(wg.shape), tok(512), tr, tr, tok(512),
                  _full(wb.shape), _full(wo.shape)],
        out_specs=tok(D_MODEL),
        out_shape=jax.ShapeDtypeStruct((b, s, D_MODEL), F32),
        compiler_params=_cparams(("parallel", "parallel")),
        name="merge",
    )(x, nw, wg, ret, mlat, difft, ssm, wb, wo)


FFN_COL_TILE = 1408


def _ffn_kernel(xp_ref, x_ref, xn_ref, nw_ref, wgate_ref, wup_ref, cw_ref, cb_ref, wdown_ref, fnw_ref,
                o_ref, g_ref, *, final_norm):
    tm = x_ref.shape[1]
    nw = nw_ref[...]
    x = x_ref[0]
    hm = _rms(x, nw, NORM_EPS)
    hcat = jnp.concatenate([_rms(xp_ref[0], nw, NORM_EPS), hm, _rms(xn_ref[0], nw, NORM_EPS)],
                           axis=0).astype(BF16)
    hb = hm.astype(BF16)
    acc = x
    for j in range(D_FF // FFN_COL_TILE):
        cols = slice(j * FFN_COL_TILE, (j + 1) * FFN_COL_TILE)
        g_ref[...] = _dot(hcat, wgate_ref[:, cols])
        _zero_halo_at_sequence_ends(g_ref, tm)
        u = _conv3(g_ref, tm, cw_ref.at[:, cols], cb_ref.at[:, cols])
        act = (_silu(u) * _dot(hb, wup_ref[:, cols])).astype(BF16)
        acc = acc + _dot(act, wdown_ref[cols, :])
    if final_norm:
        acc = _rms(acc, fnw_ref[...], NORM_EPS)
    o_ref[0] = acc


def _ffn(x, nw, wgate, wup, cw, cb, wdown, fnw, final_norm):
    b, s, _ = x.shape
    tm = TOKEN_TILE
    return pl.pallas_call(
        functools.partial(_ffn_kernel, final_norm=final_norm),
        grid=(b, s // tm),
        in_specs=[*_halo_specs(tm, s, D_MODEL), _full((1, D_MODEL)), _full(wgate.shape), _full(wup.shape),
                  _full(cw.shape), _full(cb.shape), _full(wdown.shape), _full((1, D_MODEL))],
        out_specs=pl.BlockSpec((1, tm, D_MODEL), lambda bi, i: (bi, i, 0)),
        out_shape=jax.ShapeDtypeStruct((b, s, D_MODEL), F32),
        scratch_shapes=[pltpu.VMEM((tm + 2 * HALO, FFN_COL_TILE), F32)],
        compiler_params=_cparams(("parallel", "parallel")),
        name="ffn",
    )(x, x, x, nw, wgate, wup, cw, cb, wdown, fnw)


def _pad_axis(t, axis, size):
    pad = [(0, 0)] * t.ndim
    pad[axis] = (0, size - t.shape[axis])
    return jnp.pad(t, pad)


def _layer_params(l, p):
    w_in = p["w_in"][l]
    row = lambda t: t.reshape(1, -1).astype(F32)
    d = {}
    d["norm_mix"] = row(p["norm_mix_w"][l])
    d["w_ret"] = w_in[:, _O_RET:_O_MLA_CQ].astype(BF16)
    kr = jnp.pad(w_in[:, _O_MLA_KR:_O_DIFF_QK], ((0, 0), (MLA_NOPE_DIM, HEAD_PAD - MLA_NOPE_DIM - MLA_ROPE_DIM)))
    d["w_mla1"] = jnp.concatenate([w_in[:, _O_MLA_CQ:_O_MLA_KR], kr], axis=1).astype(BF16)
    d["mla_qnw"] = row(p["mla_q_norm_w"][l])
    d["mla_kvnw"] = row(p["mla_kv_norm_w"][l])
    uq = p["mla_w_uq"][l].reshape(MLA_Q_RANK, MLA_HEADS, MLA_NOPE_DIM + MLA_ROPE_DIM)
    d["w_mla_q"] = _pad_axis(uq, 2, HEAD_PAD).reshape(MLA_Q_RANK, MLA_HEADS * HEAD_PAD).astype(BF16)
    r1, r2 = jnp.split(uq[:, :, MLA_NOPE_DIM:], 2, axis=2)
    swapped = jnp.concatenate([jnp.zeros_like(uq[:, :, :MLA_NOPE_DIM]), -r2, r1], axis=2)
    d["w_mla_q_swap"] = _pad_axis(swapped, 2, HEAD_PAD).reshape(MLA_Q_RANK, MLA_HEADS * HEAD_PAD).astype(BF16)
    ukv = p["mla_w_ukv"][l].reshape(MLA_KV_RANK, MLA_HEADS, MLA_NOPE_DIM + MLA_V_DIM)
    d["w_mla_k"] = _pad_axis(ukv[:, :, :MLA_NOPE_DIM], 2, HEAD_PAD).reshape(
        MLA_KV_RANK, MLA_HEADS * HEAD_PAD).astype(BF16)
    vt = jnp.transpose(ukv[:, :, MLA_NOPE_DIM:], (1, 2, 0))
    d["w_mla_vt"] = _pad_axis(vt, 1, MLA_VA).reshape(MLA_HEADS * MLA_VA, MLA_KV_RANK).astype(BF16)
    d["w_diff_qk"] = w_in[:, _O_DIFF_QK:_O_DIFF_V].astype(BF16)
    dvt = w_in[:, _O_DIFF_V:_O_SSM_Z].T.reshape(DIFF_HEADS, DIFF_V_DIM, D_MODEL)
    d["w_diff_vt"] = _pad_axis(dvt, 1, DIFF_VA).reshape(DIFF_HEADS * DIFF_VA, D_MODEL).astype(BF16)
    d["diff_lambda"] = p["diff_lambda"][l].astype(F32)
    d["diff_nw"] = jnp.broadcast_to(p["diff_norm_w"][l].astype(F32)[:, None], (DIFF_V_DIM, ATTN_Q_TILE // 2))
    d["w_ssm_x"] = w_in[:, _O_SSM_XBC:_O_SSM_DT].astype(BF16)
    zdt = jnp.concatenate([w_in[:, _O_SSM_Z:_O_SSM_XBC], w_in[:, _O_SSM_DT:_O_GATE]], axis=1)
    d["w_ssm_zdt"] = _pad_axis(zdt, 1, SSM_INNER + LANES).astype(BF16)
    d["ssm_conv_w"] = p["ssm_conv_w"][l].astype(F32)
    d["ssm_conv_b"] = row(p["ssm_conv_b"][l])
    d["ssm_dt_bias"] = _pad_axis(row(p["ssm_dt_bias"][l]), 1, LANES)
    d["ssm_dt_bias_col"] = p["ssm_dt_bias"][l].astype(F32).reshape(2 * SSM_HEADS, 1)
    d["w_ssm_dtt"] = w_in[:, _O_SSM_DT:_O_GATE].T.astype(BF16)
    d["ssm_alog_row"] = _pad_axis(row(p["ssm_a_log"][l]), 1, LANES)
    d["ssm_alog"] = jnp.broadcast_to(p["ssm_a_log"][l].astype(F32).reshape(2 * SSM_HEADS, 1), (2 * SSM_HEADS, CHUNK))
    d["ssm_dskip"] = jnp.repeat(p["ssm_d"][l].astype(F32), SSM_HEAD_DIM).reshape(1, SSM_INNER)
    d["ssm_nw"] = row(p["ssm_norm_w"][l])
    d["ret_nw"] = row(p["ret_norm_w"][l])
    d["ret_log_decay"] = p["ret_log_decay"][l]
    d["w_gate"] = w_in[:, _O_GATE:_IN_COLS].astype(BF16)
    d["w_branch"] = p["w_branch"][l].astype(BF16)
    d["w_out"] = p["w_out"][l].astype(BF16)
    d["norm_ffn"] = row(p["norm_ffn_w"][l])
    d["ffn_w_gate"] = p["ffn_w_gate"][l].astype(BF16)
    d["ffn_w_up"] = p["ffn_w_up"][l].astype(BF16)
    d["ffn_conv_w"] = p["ffn_conv_w"][l].astype(F32)
    d["ffn_conv_b"] = row(p["ffn_conv_b"][l])
    d["ffn_w_down"] = p["ffn_w_down"][l].astype(BF16)
    d["final_nw"] = row(p["final_norm_w"])
    return d


def _encoder_layer(x, l, d, ropes):
    rope64, rope_mla = ropes
    rq, rk, rv, rg = _ret_in(x, d["norm_mix"], d["w_ret"], rope64)
    dmat, qdf, qdb, kdf, kdb, cdf, cdb = _ret_tables(d["ret_log_decay"])
    rsf, rsb = _ret_state(rk, rv, kdf, kdb, cdf, cdb)
    ret = _ret_out(rq, rk, rv, rg, rsf, rsb, dmat, qdf, qdb, d["ret_nw"])
    mq, mk, mvt = _mla_in(x, d["norm_mix"], d["w_mla1"], d["mla_qnw"], d["mla_kvnw"], d["w_mla_q"],
                          d["w_mla_q_swap"], d["w_mla_k"], d["w_mla_vt"], rope_mla)
    mlat = _mla_attn(mq, mk, mvt)
    dq, dk, dvt = _diff_in(x, d["norm_mix"], d["w_diff_qk"], d["w_diff_vt"], rope64)
    lam_init = 0.8 - 0.6 * math.exp(-0.3 * l)
    difft = _diff_attn(dq, dk, dvt, d["diff_lambda"], d["diff_nw"], lam_init)
    xbc, sz, dt, dtt = _ssm_in(x, d["norm_mix"], d["w_ssm_x"], d["w_ssm_zdt"], d["w_ssm_dtt"], d["ssm_conv_w"],
                               d["ssm_conv_b"], d["ssm_dt_bias"], d["ssm_dt_bias_col"])
    ssf, ssb = _ssd_state(xbc, dtt, d["ssm_alog"])
    ssm = _ssd_out(xbc, dt, dtt, sz, ssf, ssb, d["ssm_alog"], d["ssm_alog_row"], d["ssm_dskip"], d["ssm_nw"])
    x = _merge(x, d["norm_mix"], d["w_gate"], ret, mlat, difft, ssm, d["w_branch"], d["w_out"])
    return _ffn(x, d["norm_ffn"], d["ffn_w_gate"], d["ffn_w_up"], d["ffn_conv_w"], d["ffn_conv_b"],
                d["ffn_w_down"], d["final_nw"], final_norm=(l == DEPTH - 1))


def _trunk(x, layers):
    s = x.shape[1]
    ropes = (_rope_tables(s, RET_QK_DIM, RET_QK_DIM, 0),
             _rope_tables(s, MLA_ROPE_DIM, HEAD_PAD, MLA_NOPE_DIM))
    for l in range(DEPTH):
        x = _encoder_layer(x, l, layers[l], ropes)
    return x


def kernel(x_prompt, x_sample, norm_mix_w, w_in, ret_log_decay, ret_norm_w, mla_q_norm_w, mla_w_uq, mla_kv_norm_w, mla_w_ukv, diff_lambda, diff_norm_w, ssm_conv_w, ssm_conv_b, ssm_dt_bias, ssm_a_log, ssm_d, ssm_norm_w, w_branch, w_out, norm_ffn_w, ffn_w_gate, ffn_w_up, ffn_conv_w, ffn_conv_b, ffn_w_down, final_norm_w):
    p = {
        "norm_mix_w": norm_mix_w, "w_in": w_in, "ret_log_decay": ret_log_decay, "ret_norm_w": ret_norm_w,
        "mla_q_norm_w": mla_q_norm_w, "mla_w_uq": mla_w_uq, "mla_kv_norm_w": mla_kv_norm_w,
        "mla_w_ukv": mla_w_ukv, "diff_lambda": diff_lambda, "diff_norm_w": diff_norm_w,
        "ssm_conv_w": ssm_conv_w, "ssm_conv_b": ssm_conv_b, "ssm_dt_bias": ssm_dt_bias,
        "ssm_a_log": ssm_a_log, "ssm_d": ssm_d, "ssm_norm_w": ssm_norm_w, "w_branch": w_branch,
        "w_out": w_out, "norm_ffn_w": norm_ffn_w, "ffn_w_gate": ffn_w_gate, "ffn_w_up": ffn_w_up,
        "ffn_conv_w": ffn_conv_w, "ffn_conv_b": ffn_conv_b, "ffn_w_down": ffn_w_down,
        "final_norm_w": final_norm_w,
    }
    layers = [_layer_params(l, p) for l in range(DEPTH)]
    return _trunk(x_prompt, layers), _trunk(x_sample, layers)
```

```python
import functools
import math

import jax
import jax.numpy as jnp
from jax import lax
from jax.experimental import pallas as pl
from jax.experimental.pallas import tpu as pltpu

F32 = jnp.float32
BF16 = jnp.bfloat16

D_MODEL = 1024
DEPTH = 2
ROPE_THETA = 10000.0
NORM_EPS = 1e-6
HEAD_NORM_EPS = 1e-5
CHUNK = 128
N_BRANCH = 4
BRANCH_WIDTH = 512

RET_HEADS, RET_QK_DIM, RET_V_DIM = 4, 64, 128
MLA_HEADS, MLA_NOPE_DIM, MLA_ROPE_DIM, MLA_V_DIM = 8, 64, 32, 64
MLA_Q_RANK, MLA_KV_RANK = 256, 128
DIFF_HEADS, DIFF_HEAD_DIM = 4, 64
DIFF_V_DIM = 2 * DIFF_HEAD_DIM
SSM_HEADS, SSM_HEAD_DIM, SSM_GROUPS, SSM_STATE = 8, 64, 2, 128
SSM_INNER = SSM_HEADS * SSM_HEAD_DIM
SSM_CONV_CH = SSM_INNER + 2 * SSM_GROUPS * SSM_STATE
D_FF = 2816

_O_RET = 0
_O_MLA_CQ = 1536
_O_MLA_CKV = 1792
_O_MLA_KR = 1920
_O_DIFF_QK = 1952
_O_DIFF_V = 2976
_O_SSM_Z = 3488
_O_SSM_XBC = 4000
_O_SSM_DT = 5024
_O_GATE = 5040
_IN_COLS = 9136

LANES = 128
SUBLANES = 8
HALO = SUBLANES
PACK_ROWS = 16
HEAD_PAD = LANES
MLA_VA = MLA_V_DIM + PACK_ROWS
DIFF_VA = DIFF_V_DIM + PACK_ROWS
TOKEN_TILE = 512
FFN_TOKEN_TILE = 1024
IN_TOKEN_TILE = 1024
CHUNKS_PER_STEP = 8
ATTN_Q_TILE = 512
ATTN_K_TILE = 256
ATTN_LOOKAHEAD = 3
ATTN_TILES_PER_STEP = 4
LOG2E = 1.4426950408889634
NEG_BIG = -1e30
VMEM_LIMIT = 56 * 1024 * 1024
FFN_VMEM_LIMIT = 60 * 1024 * 1024


def _cparams(sems, vmem=VMEM_LIMIT):
    return pltpu.CompilerParams(dimension_semantics=sems, vmem_limit_bytes=vmem)


def _full(shape):
    return pl.BlockSpec(shape, lambda *_: (0,) * len(shape))


def _resident(shape):
    return pl.BlockSpec(shape, lambda *_: (0,) * len(shape), pipeline_mode=pl.Buffered(1))


def _nt(a, b):
    return lax.dot_general(a, b, (((1,), (1,)), ((), ())), preferred_element_type=F32)


def _dot(a, b):
    return jnp.dot(a, b, preferred_element_type=F32)


def _rms(x, w, eps):
    return x * lax.rsqrt(jnp.mean(x * x, axis=-1, keepdims=True) + eps) * w


def _sigmoid(x):
    return 0.5 * jnp.tanh(0.5 * x) + 0.5


def _silu(x):
    return x * _sigmoid(x)


def _rope(x, c, s1, s2, half):
    return x * c + pltpu.roll(x, LANES - half, 1) * s1 + pltpu.roll(x, half, 1) * s2


def _rope_tables(seq, dim, block, off):
    inv_freq = 1.0 / (ROPE_THETA ** (jnp.arange(0, dim, 2, dtype=F32) / dim))
    ang = jnp.arange(seq, dtype=F32)[:, None] * inv_freq[None, :]
    cos, sin = jnp.cos(ang), jnp.sin(ang)
    half = dim // 2
    zero = jnp.zeros_like(sin)
    pre = jnp.zeros((seq, off), F32)
    post = jnp.zeros((seq, block - off - dim), F32)
    c = jnp.concatenate([pre + 1.0, cos, cos, post + 1.0], axis=1)
    s1 = jnp.concatenate([pre, -sin, zero, post], axis=1)
    s2 = jnp.concatenate([pre, zero, sin, post], axis=1)
    reps = LANES // block
    return tuple(jnp.tile(t, (1, reps)) for t in (c, s1, s2))


def _ret_in_kernel(x_ref, nw_ref, w_ref, c_ref, s1_ref, s2_ref, q_ref, k_ref, v_ref, g_ref):
    h = _rms(x_ref[0], nw_ref[...], NORM_EPS).astype(BF16)
    p = _dot(h, w_ref[...])
    c, s1, s2 = c_ref[...], s1_ref[...], s2_ref[...]
    half = RET_QK_DIM // 2
    for i in range(2):
        sl = slice(i * LANES, (i + 1) * LANES)
        q_ref[0, :, sl] = _rope(p[:, sl], c, s1, s2, half).astype(BF16)
        kk = _rope(p[:, 256 + i * LANES:256 + (i + 1) * LANES], c, s1, s2, half)
        k_ref[0, :, sl] = (kk * (RET_QK_DIM ** -0.5)).astype(BF16)
    v_ref[0] = p[:, 512:1024].astype(BF16)
    g_ref[0] = _silu(p[:, 1024:1536]).astype(BF16)


def _ret_in(x, nw, w, tabs):
    b, s, _ = x.shape
    tm = IN_TOKEN_TILE
    tok = lambda width: pl.BlockSpec((1, tm, width), lambda bi, i: (bi, i, 0))
    tab = pl.BlockSpec((tm, LANES), lambda bi, i: (i, 0))
    return pl.pallas_call(
        _ret_in_kernel,
        grid=(b, s // tm),
        in_specs=[tok(D_MODEL), _full((1, D_MODEL)), _full(w.shape), tab, tab, tab],
        out_specs=[tok(256), tok(256), tok(512), tok(512)],
        out_shape=[jax.ShapeDtypeStruct((b, s, n), BF16) for n in (256, 256, 512, 512)],
        compiler_params=_cparams(("parallel", "parallel")),
        name="ret_in",
    )(x, nw, w, *tabs)


def _ones_rows(rows, cols, period, at):
    r = lax.broadcasted_iota(jnp.int32, (rows, cols), 0)
    return jnp.where(r % period == at, 1.0, 0.0).astype(F32)


def _mla_in_kernel(x_ref, nw_ref, w1_ref, qnw_ref, kvnw_ref, wq_ref, wqsw_ref, wk_ref, wvt_ref,
                   c_ref, s1_ref, s2_ref, q_ref, k_ref, vt_ref):
    h = _rms(x_ref[0], nw_ref[...], NORM_EPS).astype(BF16)
    p = _dot(h, w1_ref[...])
    cqn = _rms(p[:, 0:256], qnw_ref[...], NORM_EPS).astype(BF16)
    ckvn = _rms(p[:, 256:384], kvnw_ref[...], NORM_EPS).astype(BF16)
    c, s1, s2 = c_ref[...], s1_ref[...], s2_ref[...]
    half = MLA_ROPE_DIM // 2
    krr = _rope(p[:, 384:512], c, s1, s2, half)
    qp = _dot(cqn, wq_ref[...])
    qsw = _dot(cqn, wqsw_ref[...])
    sin = s2 - s1
    kp = _dot(ckvn, wk_ref[...])
    qscale = ((MLA_NOPE_DIM + MLA_ROPE_DIM) ** -0.5) * LOG2E
    for hd in range(MLA_HEADS):
        sl = slice(hd * HEAD_PAD, (hd + 1) * HEAD_PAD)
        q_ref[0, :, sl] = ((qp[:, sl] * c + qsw[:, sl] * sin) * qscale).astype(BF16)
        k_ref[0, :, sl] = (kp[:, sl] + krr).astype(BF16)
    vt = _nt(wvt_ref[...], ckvn)
    vt_ref[0] = (vt + _ones_rows(vt.shape[0], vt.shape[1], MLA_VA, MLA_V_DIM)).astype(BF16)


def _mla_in(x, nw, w1, qnw, kvnw, wq, wqsw, wk, wvt, tabs):
    b, s, _ = x.shape
    tm = IN_TOKEN_TILE
    tok = lambda width: pl.BlockSpec((1, tm, width), lambda bi, i: (bi, i, 0))
    tab = pl.BlockSpec((tm, LANES), lambda bi, i: (i, 0))
    rows = MLA_HEADS * MLA_VA
    return pl.pallas_call(
        _mla_in_kernel,
        grid=(b, s // tm),
        in_specs=[tok(D_MODEL), _full((1, D_MODEL)), _full(w1.shape), _full(qnw.shape), _full(kvnw.shape),
                  _full(wq.shape), _full(wqsw.shape), _full(wk.shape), _full(wvt.shape), tab, tab, tab],
        out_specs=[tok(1024), tok(1024), pl.BlockSpec((1, rows, tm), lambda bi, i: (bi, 0, i))],
        out_shape=[jax.ShapeDtypeStruct((b, s, 1024), BF16), jax.ShapeDtypeStruct((b, s, 1024), BF16),
                   jax.ShapeDtypeStruct((b, rows, s), BF16)],
        compiler_params=_cparams(("parallel", "parallel")),
        name="mla_in",
    )(x, nw, w1, qnw, kvnw, wq, wqsw, wk, wvt, *tabs)


def _diff_in_kernel(x_ref, nw_ref, wqk_ref, wvt_ref, c_ref, s1_ref, s2_ref, q_ref, k_ref, vt_ref):
    h = _rms(x_ref[0], nw_ref[...], NORM_EPS).astype(BF16)
    p = _dot(h, wqk_ref[...])
    c, s1, s2 = c_ref[...], s1_ref[...], s2_ref[...]
    half = DIFF_HEAD_DIM // 2
    qscale = (DIFF_HEAD_DIM ** -0.5) * LOG2E
    for hd in range(DIFF_HEADS):
        sl = slice(hd * HEAD_PAD, (hd + 1) * HEAD_PAD)
        q_ref[0, :, sl] = (_rope(p[:, sl], c, s1, s2, half) * qscale).astype(BF16)
        k_ref[0, :, sl] = _rope(p[:, 512 + hd * HEAD_PAD:512 + (hd + 1) * HEAD_PAD], c, s1, s2, half).astype(BF16)
    vt = _nt(wvt_ref[...], h)
    vt_ref[0] = (vt + _ones_rows(vt.shape[0], vt.shape[1], DIFF_VA, DIFF_V_DIM)).astype(BF16)


def _diff_in(x, nw, wqk, wvt, tabs):
    b, s, _ = x.shape
    tm = IN_TOKEN_TILE
    tok = lambda width: pl.BlockSpec((1, tm, width), lambda bi, i: (bi, i, 0))
    tab = pl.BlockSpec((tm, LANES), lambda bi, i: (i, 0))
    rows = DIFF_HEADS * DIFF_VA
    return pl.pallas_call(
        _diff_in_kernel,
        grid=(b, s // tm),
        in_specs=[tok(D_MODEL), _full((1, D_MODEL)), _full(wqk.shape), _full(wvt.shape), tab, tab, tab],
        out_specs=[tok(512), tok(512), pl.BlockSpec((1, rows, tm), lambda bi, i: (bi, 0, i))],
        out_shape=[jax.ShapeDtypeStruct((b, s, 512), BF16), jax.ShapeDtypeStruct((b, s, 512), BF16),
                   jax.ShapeDtypeStruct((b, rows, s), BF16)],
        compiler_params=_cparams(("parallel", "parallel")),
        name="diff_in",
    )(x, nw, wqk, wvt, *tabs)


def _halo_specs(tm, s, width):
    per = tm // HALO
    last = s // HALO - 1
    main = pl.BlockSpec((1, tm, width), lambda bi, i: (bi, i, 0))
    prev = pl.BlockSpec((1, HALO, width), lambda bi, i: (bi, jnp.maximum(i * per - 1, 0), 0))
    nxt = pl.BlockSpec((1, HALO, width), lambda bi, i: (bi, jnp.minimum((i + 1) * per, last), 0))
    return prev, main, nxt


def _conv3(g_ref, tm, w_ref, b_ref):
    u = w_ref[0:1, :] * g_ref[pl.ds(HALO - 1, tm), :]
    u = u + w_ref[1:2, :] * g_ref[pl.ds(HALO, tm), :]
    u = u + w_ref[2:3, :] * g_ref[pl.ds(HALO + 1, tm), :]
    return u + b_ref[...]


CONV_ROW_STRIP = 64
CONV_COL_STRIP = 256


def _conv3_strips(g_ref, tm, w_ref, b_ref, emit):
    for r0 in range(0, tm, CONV_ROW_STRIP):
        for c0 in range(0, g_ref.shape[1], CONV_COL_STRIP):
            rows = slice(r0, r0 + CONV_ROW_STRIP)
            cols = slice(c0, c0 + CONV_COL_STRIP)
            u = w_ref[0:1, cols] * g_ref[HALO - 1 + r0:HALO - 1 + r0 + CONV_ROW_STRIP, cols]
            u = u + w_ref[1:2, cols] * g_ref[HALO + r0:HALO + r0 + CONV_ROW_STRIP, cols]
            u = u + w_ref[2:3, cols] * g_ref[HALO + 1 + r0:HALO + 1 + r0 + CONV_ROW_STRIP, cols]
            emit(rows, cols, u + b_ref[:, cols])


def _zero_halo_at_sequence_ends(g_ref, tm):
    i = pl.program_id(1)

    @pl.when(i == 0)
    def _():
        g_ref[0:HALO, :] = jnp.zeros((HALO, g_ref.shape[1]), F32)

    @pl.when(i == pl.num_programs(1) - 1)
    def _():
        g_ref[HALO + tm:2 * HALO + tm, :] = jnp.zeros((HALO, g_ref.shape[1]), F32)


def _softplus(t):
    return jnp.maximum(t, 0.0) + jnp.log1p(jnp.exp(-jnp.abs(t)))


def _ssm_in_kernel(xp_ref, x_ref, xn_ref, nw_ref, wx_ref, wzd_ref, wdtt_ref, cw_ref, cb_ref, dtb_ref, dtbc_ref,
                   xbc_ref, z_ref, dt_ref, dtt_ref, g_ref):
    tm = x_ref.shape[1]
    nw = nw_ref[...]
    hm = _rms(x_ref[0], nw, NORM_EPS)
    hcat = jnp.concatenate([_rms(xp_ref[0], nw, NORM_EPS), hm, _rms(xn_ref[0], nw, NORM_EPS)], axis=0)
    g_ref[...] = _dot(hcat.astype(BF16), wx_ref[...])
    _zero_halo_at_sequence_ends(g_ref, tm)
    def emit(rows, cols, u):
        xbc_ref[0, rows, cols] = _silu(u)

    _conv3_strips(g_ref, tm, cw_ref, cb_ref, emit)
    hb = hm.astype(BF16)
    zd = _dot(hb, wzd_ref[...])
    z_ref[0] = _silu(zd[:, 0:512])
    sp = _softplus(zd[:, 512:640] + dtb_ref[...])
    lane = lax.broadcasted_iota(jnp.int32, sp.shape, 1)
    dt_ref[0] = jnp.where(lane < 2 * SSM_HEADS, sp, 0.0)
    dtt_ref[0] = _softplus(_nt(wdtt_ref[...], hb) + dtbc_ref[...])


def _ssm_in(x, nw, wx, wzd, wdtt, cw, cb, dtb, dtbc):
    b, s, _ = x.shape
    tm = IN_TOKEN_TILE
    tok = lambda width: pl.BlockSpec((1, tm, width), lambda bi, i: (bi, i, 0))
    return pl.pallas_call(
        _ssm_in_kernel,
        grid=(b, s // tm),
        in_specs=[*_halo_specs(tm, s, D_MODEL), _full((1, D_MODEL)), _full(wx.shape), _full(wzd.shape),
                  _full(wdtt.shape), _full(cw.shape), _full(cb.shape), _full(dtb.shape), _full(dtbc.shape)],
        out_specs=[tok(1024), tok(512), tok(LANES), pl.BlockSpec((1, 2 * SSM_HEADS, tm), lambda bi, i: (bi, 0, i))],
        out_shape=[jax.ShapeDtypeStruct((b, s, 1024), F32), jax.ShapeDtypeStruct((b, s, 512), F32),
                   jax.ShapeDtypeStruct((b, s, LANES), F32), jax.ShapeDtypeStruct((b, 2 * SSM_HEADS, s), F32)],
        scratch_shapes=[pltpu.VMEM((tm + 2 * HALO, SSM_CONV_CH), F32)],
        compiler_params=_cparams(("parallel", "parallel")),
        name="ssm_in",
    )(x, x, x, nw, wx, wzd, wdtt, cw, cb, dtb, dtbc)


def _ret_state_kernel(kf_ref, vf_ref, kb_ref, vb_ref, kdf_ref, kdb_ref, cdf_ref, cdb_ref,
                      sf_ref, sb_ref, st_ref):
    @pl.when(pl.program_id(1) == 0)
    def _():
        st_ref[...] = jnp.zeros(st_ref.shape, F32)

    def scan(d, k_ref, v_ref, kd_ref, cd_ref, out_ref):
        incs = []
        for g in range(CHUNKS_PER_STEP):
            rows = slice(g * CHUNK, (g + 1) * CHUNK)
            kdt = (k_ref[0, rows, :].astype(F32) * kd_ref[...]).T.astype(BF16)
            v = v_ref[0, rows, :]
            incs.append(jnp.concatenate(
                [_dot(kdt[hd * RET_QK_DIM:(hd + 1) * RET_QK_DIM], v[:, hd * RET_V_DIM:(hd + 1) * RET_V_DIM])
                 for hd in range(RET_HEADS)], axis=0))
        st = st_ref[d]
        order = range(CHUNKS_PER_STEP) if d == 0 else range(CHUNKS_PER_STEP - 1, -1, -1)
        for g in order:
            out_ref[0, g] = st
            st = cd_ref[...] * st + incs[g]
        st_ref[d] = st

    scan(0, kf_ref, vf_ref, kdf_ref, cdf_ref, sf_ref)
    scan(1, kb_ref, vb_ref, kdb_ref, cdb_ref, sb_ref)


def _ret_state(k, v, kdf, kdb, cdf, cdb):
    b, s, _ = k.shape
    g = CHUNKS_PER_STEP
    n = s // (g * CHUNK)
    fwd = lambda width: pl.BlockSpec((1, g * CHUNK, width), lambda bi, c: (bi, c, 0))
    bwd = lambda width: pl.BlockSpec((1, g * CHUNK, width), lambda bi, c: (bi, n - 1 - c, 0))
    rows = RET_HEADS * RET_QK_DIM
    st_shape = jax.ShapeDtypeStruct((b, n * g, rows, RET_V_DIM), F32)
    return pl.pallas_call(
        _ret_state_kernel,
        grid=(b, n),
        in_specs=[fwd(256), fwd(512), bwd(256), bwd(512), _full(kdf.shape), _full(kdb.shape),
                  _full(cdf.shape), _full(cdb.shape)],
        out_specs=[pl.BlockSpec((1, g, rows, RET_V_DIM), lambda bi, c: (bi, c, 0, 0)),
                   pl.BlockSpec((1, g, rows, RET_V_DIM), lambda bi, c: (bi, n - 1 - c, 0, 0))],
        out_shape=[st_shape, st_shape],
        scratch_shapes=[pltpu.VMEM((2, rows, RET_V_DIM), F32)],
        compiler_params=_cparams(("parallel", "arbitrary")),
        name="ret_state",
    )(k, v, k, v, kdf, kdb, cdf, cdb)


def _ret_out_kernel(q_ref, k_ref, v_ref, g_ref, sf_ref, sb_ref, d_ref, qdf_ref, qdb_ref, nw_ref, o_ref):
    gs = range(CHUNKS_PER_STEP)
    rows = [slice(g * CHUNK, (g + 1) * CHUNK) for g in gs]
    lane = lax.broadcasted_iota(jnp.int32, (CHUNK, LANES), 1)
    q = [q_ref[0, rows[g], :].astype(F32) for g in gs]
    qf = [q[g] * qdf_ref[...] for g in gs]
    qb = [q[g] * qdb_ref[...] for g in gs]
    for hd in range(RET_HEADS):
        pair = slice((hd // 2) * LANES, (hd // 2 + 1) * LANES)
        own = (lane < RET_QK_DIM) if hd % 2 == 0 else (lane >= RET_QK_DIM)
        pick = lambda t: jnp.where(own, t[:, pair], 0.0).astype(BF16)
        vsl = slice(hd * RET_V_DIM, (hd + 1) * RET_V_DIM)
        sc = [_nt(pick(q[g]), k_ref[0, rows[g], pair]) * d_ref[hd] for g in gs]
        cross = [_dot(pick(qf[g]), sf_ref[0, g, pair, :].astype(BF16))
                 + _dot(pick(qb[g]), sb_ref[0, g, pair, :].astype(BF16)) for g in gs]
        y = [_dot(sc[g].astype(BF16), v_ref[0, rows[g], vsl]) + cross[g] for g in gs]
        for g in gs:
            yn = _rms(y[g], nw_ref[:, vsl], HEAD_NORM_EPS)
            o_ref[0, rows[g], vsl] = (g_ref[0, rows[g], vsl].astype(F32) * yn).astype(BF16)


def _ret_out(q, k, v, g, sf, sb, dmat, qdf, qdb, nw):
    b, s, _ = q.shape
    gc = CHUNKS_PER_STEP
    n = s // (gc * CHUNK)
    tok = lambda width: pl.BlockSpec((1, gc * CHUNK, width), lambda bi, c: (bi, c, 0))
    st = pl.BlockSpec((1, gc) + sf.shape[2:], lambda bi, c: (bi, c, 0, 0))
    return pl.pallas_call(
        _ret_out_kernel,
        grid=(b, n),
        in_specs=[tok(256), tok(256), tok(512), tok(512), st, st, _full(dmat.shape), _full(qdf.shape),
                  _full(qdb.shape), _full(nw.shape)],
        out_specs=tok(512),
        out_shape=jax.ShapeDtypeStruct((b, s, 512), BF16),
        compiler_params=_cparams(("parallel", "parallel")),
        name="ret_out",
    )(q, k, v, g, sf, sb, dmat, qdf, qdb, nw)


def _ret_tables(log_decay):
    lg_f = log_decay[0].astype(F32)
    lg_b = log_decay[1].astype(F32)
    pos = jnp.arange(CHUNK, dtype=F32)
    rel = pos[:, None] - pos[None, :]
    low = jnp.where(rel >= 0, jnp.exp(jnp.maximum(rel, 0.0)[None] * lg_f[:, None, None]), 0.0)
    upp = jnp.where(rel <= 0, jnp.exp(jnp.maximum(-rel, 0.0)[None] * lg_b[:, None, None]), 0.0)
    dmat = low + upp
    wide = lambda t: jnp.repeat(t, RET_QK_DIM, axis=1)
    qdf = wide(jnp.exp((pos + 1.0)[:, None] * lg_f))
    qdb = wide(jnp.exp((CHUNK - pos)[:, None] * lg_b))
    kdf = wide(jnp.exp((CHUNK - 1.0 - pos)[:, None] * lg_f))
    kdb = wide(jnp.exp(pos[:, None] * lg_b))
    tall = lambda lg: jnp.broadcast_to(jnp.repeat(jnp.exp(CHUNK * lg), RET_QK_DIM)[:, None],
                                       (RET_HEADS * RET_QK_DIM, RET_V_DIM))
    return dmat, qdf, qdb, kdf, kdb, tall(lg_f), tall(lg_b)


def _split3(x):
    hi = x.astype(BF16)
    r1 = x - hi.astype(F32)
    mid = r1.astype(BF16)
    lo = (r1 - mid.astype(F32)).astype(BF16)
    return hi, mid, lo


def _tri(lower):
    ii = lax.broadcasted_iota(jnp.int32, (CHUNK, CHUNK), 0)
    jj = lax.broadcasted_iota(jnp.int32, (CHUNK, CHUNK), 1)
    return jnp.where((ii >= jj) if lower else (jj >= ii), 1.0, 0.0).astype(BF16)


def _cumsum_time_on_lanes(xt, tri):
    hi, mid, lo = _split3(xt)
    return _nt(hi, tri) + _nt(mid, tri) + _nt(lo, tri)


def _cumsum_time_on_rows(x, tri):
    hi, mid, lo = _split3(x)
    return _dot(tri, hi) + _dot(tri, mid) + _dot(tri, lo)


def _ssd_state_kernel(xf_ref, dtf_ref, xb_ref, dtb_ref, alog_ref, sf_ref, sb_ref, st_ref):
    @pl.when(pl.program_id(1) == 0)
    def _():
        st_ref[...] = jnp.zeros(st_ref.shape, F32)

    head_of_lane = lax.broadcasted_iota(jnp.int32, (1, SSM_INNER), 1) // SSM_HEAD_DIM

    per_group = SSM_INNER // SSM_GROUPS

    def scan(d, x_ref, dt_ref, out_ref):
        gs = range(CHUNKS_PER_STEP)
        rows = [slice(g * CHUNK, (g + 1) * CHUNK) for g in gs]
        r0 = d * SSM_HEADS
        a = -jnp.exp(alog_ref[r0:r0 + SSM_HEADS, :])
        tri = _tri(lower=(d == 0))
        dtt = [dt_ref[0, r0:r0 + SSM_HEADS, rows[g]] for g in gs]
        cum = [_cumsum_time_on_lanes(dtt[g] * a, tri) for g in gs]
        endb = [jnp.broadcast_to(cum[g][:, 0:1] if d == 1 else cum[g][:, CHUNK - 1:CHUNK], cum[g].shape)
                for g in gs]
        wt = [jnp.exp(endb[g] - cum[g]) * dtt[g] for g in gs]
        edec = [jnp.exp(endb[g]) for g in gs]
        xt = [x_ref[0, rows[g], 0:SSM_INNER].T for g in gs]
        bt = [[x_ref[0, rows[g], SSM_INNER + k * SSM_STATE:SSM_INNER + (k + 1) * SSM_STATE].T.astype(BF16)
               for k in range(SSM_GROUPS)] for g in gs]
        xwt = [jnp.concatenate([xt[g][hd * SSM_HEAD_DIM:(hd + 1) * SSM_HEAD_DIM] * wt[g][hd:hd + 1]
                                for hd in range(SSM_HEADS)], axis=0).astype(BF16) for g in gs]
        inc = [jnp.concatenate([_nt(bt[g][k], xwt[g][k * per_group:(k + 1) * per_group])
                                for k in range(SSM_GROUPS)], axis=1) for g in gs]
        dec = []
        for g in gs:
            row_dec = jnp.zeros((1, SSM_INNER), F32)
            for hd in range(SSM_HEADS):
                row = jnp.concatenate([edec[g][hd:hd + 1]] * (SSM_INNER // LANES), axis=1)
                row_dec = jnp.where(head_of_lane == hd, row, row_dec)
            dec.append(row_dec)
        st = st_ref[d]
        for g in (gs if d == 0 else reversed(gs)):
            out_ref[0, g] = st
            st = st * dec[g] + inc[g]
        st_ref[d] = st

    scan(0, xf_ref, dtf_ref, sf_ref)
    scan(1, xb_ref, dtb_ref, sb_ref)


def _ssd_state(xbc, dtt, alog):
    b, s, _ = xbc.shape
    g = CHUNKS_PER_STEP
    n = s // (g * CHUNK)
    fwd = lambda width: pl.BlockSpec((1, g * CHUNK, width), lambda bi, c: (bi, c, 0))
    bwd = lambda width: pl.BlockSpec((1, g * CHUNK, width), lambda bi, c: (bi, n - 1 - c, 0))
    dt_fwd = pl.BlockSpec((1, 2 * SSM_HEADS, g * CHUNK), lambda bi, c: (bi, 0, c))
    dt_bwd = pl.BlockSpec((1, 2 * SSM_HEADS, g * CHUNK), lambda bi, c: (bi, 0, n - 1 - c))
    st_shape = jax.ShapeDtypeStruct((b, n * g, SSM_STATE, SSM_INNER), F32)
    return pl.pallas_call(
        _ssd_state_kernel,
        grid=(b, n),
        in_specs=[fwd(768), dt_fwd, bwd(768), dt_bwd, _full(alog.shape)],
        out_specs=[pl.BlockSpec((1, g, SSM_STATE, SSM_INNER), lambda bi, c: (bi, c, 0, 0)),
                   pl.BlockSpec((1, g, SSM_STATE, SSM_INNER), lambda bi, c: (bi, n - 1 - c, 0, 0))],
        out_shape=[st_shape, st_shape],
        scratch_shapes=[pltpu.VMEM((2, SSM_STATE, SSM_INNER), F32)],
        compiler_params=_cparams(("parallel", "arbitrary")),
        name="ssd_state",
    )(xbc, dtt, xbc, dtt, alog)


def _ssd_out_kernel(x_ref, dt_ref, dtt_ref, z_ref, sf_ref, sb_ref, alog_ref, arow_ref, dskip_ref, nw_ref, o_ref):
    gs = range(CHUNKS_PER_STEP)
    rows = [slice(g * CHUNK, (g + 1) * CHUNK) for g in gs]
    nh = SSM_HEADS
    low_t, upp_t = _tri(True), _tri(False)
    a_t = -jnp.exp(alog_ref[...])
    a_row = -jnp.exp(arow_ref[...])
    dtt = [dtt_ref[0, :, rows[g]] for g in gs]
    cumt = [jnp.concatenate([_cumsum_time_on_lanes(dtt[g][0:nh] * a_t[0:nh], low_t),
                             _cumsum_time_on_lanes(dtt[g][nh:2 * nh] * a_t[nh:2 * nh], upp_t)], axis=0) for g in gs]
    da = [dt_ref[0, rows[g], :] * a_row for g in gs]
    head_lane = lax.broadcasted_iota(jnp.int32, (CHUNK, LANES), 1)
    cum = [jnp.where(head_lane < nh, _cumsum_time_on_rows(da[g], low_t), _cumsum_time_on_rows(da[g], upp_t))
           for g in gs]
    ii = lax.broadcasted_iota(jnp.int32, (CHUNK, CHUNK), 0)
    jj = lax.broadcasted_iota(jnp.int32, (CHUNK, CHUNK), 1)
    low = ii >= jj
    upp = jj >= ii
    first = jj < SSM_HEAD_DIM
    heads_per_group = nh // SSM_GROUPS
    c0 = SSM_INNER + SSM_GROUPS * SSM_STATE
    ys = [[] for _ in gs]
    for pr in range(nh // 2):
        k = (2 * pr) // heads_per_group
        pair = slice(pr * LANES, (pr + 1) * LANES)
        cg = [x_ref[0, rows[g], c0 + k * SSM_STATE:c0 + (k + 1) * SSM_STATE].astype(BF16) for g in gs]
        bg = [x_ref[0, rows[g], SSM_INNER + k * SSM_STATE:SSM_INNER + (k + 1) * SSM_STATE].astype(BF16) for g in gs]
        cb = [_nt(cg[g], bg[g]) for g in gs]
        off = [_dot(cg[g], sf_ref[0, g, :, pair].astype(BF16)) for g in gs]
        offb = [_dot(cg[g], sb_ref[0, g, :, pair].astype(BF16)) for g in gs]
        diag = [[], []]
        ecol = [[], []]
        for e, hd in enumerate((2 * pr, 2 * pr + 1)):
            colf = [jnp.broadcast_to(cum[g][:, hd:hd + 1], (CHUNK, CHUNK)) for g in gs]
            colb = [jnp.broadcast_to(cum[g][:, nh + hd:nh + hd + 1], (CHUNK, CHUNK)) for g in gs]
            segf = [jnp.exp(jnp.where(low, colf[g] - cumt[g][hd:hd + 1], NEG_BIG)) for g in gs]
            segb = [jnp.exp(jnp.where(upp, colb[g] - cumt[g][nh + hd:nh + hd + 1], NEG_BIG)) for g in gs]
            w = [cb[g] * (segf[g] * dtt[g][hd:hd + 1] + segb[g] * dtt[g][nh + hd:nh + hd + 1]) for g in gs]
            diag[e] = [_dot(w[g].astype(BF16), x_ref[0, rows[g], pair].astype(BF16)) for g in gs]
            ecol[e] = [(jnp.exp(colf[g]), jnp.exp(colb[g])) for g in gs]
        for g in gs:
            yd = jnp.where(first, diag[0][g], diag[1][g])
            ef = jnp.where(first, ecol[0][g][0], ecol[1][g][0])
            eb = jnp.where(first, ecol[0][g][1], ecol[1][g][1])
            ys[g].append(yd + ef * off[g] + eb * offb[g])
    gw = SSM_INNER // SSM_GROUPS
    for g in gs:
        y = jnp.concatenate(ys[g], axis=1) + dskip_ref[...] * x_ref[0, rows[g], 0:SSM_INNER]
        y = y * z_ref[0, rows[g], :]
        for k in range(SSM_GROUPS):
            sl = slice(k * gw, (k + 1) * gw)
            o_ref[0, rows[g], sl] = _rms(y[:, sl], nw_ref[:, sl], HEAD_NORM_EPS).astype(BF16)


def _ssd_out(xbc, dt, dtt, z, sf, sb, alog, arow, dskip, nw):
    b, s, _ = xbc.shape
    gc = CHUNKS_PER_STEP
    n = s // (gc * CHUNK)
    tok = lambda width: pl.BlockSpec((1, gc * CHUNK, width), lambda bi, c: (bi, c, 0))
    st = pl.BlockSpec((1, gc, SSM_STATE, SSM_INNER), lambda bi, c: (bi, c, 0, 0))
    return pl.pallas_call(
        _ssd_out_kernel,
        grid=(b, n),
        in_specs=[tok(1024), tok(LANES), pl.BlockSpec((1, 2 * SSM_HEADS, gc * CHUNK), lambda bi, c: (bi, 0, c)),
                  tok(512), st, st, _full(alog.shape), _full(arow.shape), _full(dskip.shape), _full(nw.shape)],
        out_specs=tok(512),
        out_shape=jax.ShapeDtypeStruct((b, s, 512), BF16),
        compiler_params=_cparams(("parallel", "parallel")),
        name="ssd_out",
    )(xbc, dt, dtt, z, sf, sb, alog, arow, dskip, nw)


def _attn_scores_t(q, k_ref, vt_ref, seq):
    tk = min(ATTN_K_TILE, seq)
    rows = vt_ref.shape[1]

    m = jnp.full((1, ATTN_Q_TILE), NEG_BIG, F32)
    acc = jnp.zeros((rows, ATTN_Q_TILE), F32)
    n = seq // tk
    scores = lambda c: _nt(k_ref[0, c * tk:(c + 1) * tk, :], q)
    pending = [scores(c) for c in range(min(ATTN_LOOKAHEAD, n))]
    for c in range(n):
        s = pending.pop(0)
        if c + ATTN_LOOKAHEAD < n:
            pending.append(scores(c + ATTN_LOOKAHEAD))
        m_new = jnp.maximum(m, jnp.max(s, axis=0, keepdims=True))
        v = vt_ref[0, :, c * tk:(c + 1) * tk]
        hw = ATTN_Q_TILE // 2
        pv = [_dot(v, jnp.exp2(s[:, i * hw:(i + 1) * hw] - m_new[:, i * hw:(i + 1) * hw]).astype(BF16))
              for i in range(2)]
        acc = acc * jnp.exp2(m - m_new) + jnp.concatenate(pv, axis=1)
        m = m_new
    return acc


def _mla_attn_kernel(q_ref, k_ref, vt_ref, o_ref):
    tq = ATTN_Q_TILE
    for t in range(ATTN_TILES_PER_STEP):
        cols = slice(t * tq, (t + 1) * tq)
        acc = _attn_scores_t(q_ref[0, cols, :], k_ref, vt_ref, k_ref.shape[1])
        o_ref[0, :, cols] = (acc[0:MLA_V_DIM] / acc[MLA_V_DIM:MLA_V_DIM + 1]).astype(BF16)


def _mla_attn(q, k, vt):
    b, s, _ = q.shape
    tq = ATTN_Q_TILE * ATTN_TILES_PER_STEP
    return pl.pallas_call(
        _mla_attn_kernel,
        grid=(b, MLA_HEADS, s // tq),
        in_specs=[pl.BlockSpec((1, tq, HEAD_PAD), lambda bi, h, i: (bi, i, h)),
                  pl.BlockSpec((1, s, HEAD_PAD), lambda bi, h, i: (bi, 0, h)),
                  pl.BlockSpec((1, MLA_VA, s), lambda bi, h, i: (bi, h, 0))],
        out_specs=pl.BlockSpec((1, MLA_V_DIM, tq), lambda bi, h, i: (bi, h, i)),
        out_shape=jax.ShapeDtypeStruct((b, MLA_HEADS * MLA_V_DIM, s), BF16),
        compiler_params=_cparams(("parallel", "parallel", "parallel")),
        name="mla_attn",
    )(q, k, vt)


def _diff_attn_kernel(q_ref, k_ref, vt_ref, lam_ref, nw_ref, o_ref, *, lam_init):
    tq = ATTN_Q_TILE // 2
    dv = DIFF_V_DIM
    lv = lam_ref[...]
    lam = (jnp.exp(jnp.sum(lv[0:1] * lv[1:2], axis=1, keepdims=True))
           - jnp.exp(jnp.sum(lv[2:3] * lv[3:4], axis=1, keepdims=True)) + lam_init)
    for t in range(ATTN_TILES_PER_STEP):
        cols = slice(t * tq, (t + 1) * tq)
        q = q_ref[0, cols, :].astype(F32)
        lane = lax.broadcasted_iota(jnp.int32, q.shape, 1)
        first = lane < DIFF_HEAD_DIM
        both = jnp.concatenate([jnp.where(first, q, 0.0), jnp.where(first, 0.0, q)], axis=0).astype(BF16)
        acc = _attn_scores_t(both, k_ref, vt_ref, k_ref.shape[1])
        o1 = acc[0:dv, 0:tq] / acc[dv:dv + 1, 0:tq]
        o2 = acc[0:dv, tq:2 * tq] / acc[dv:dv + 1, tq:2 * tq]
        o = o1 - lam * o2
        o = o * lax.rsqrt(jnp.mean(o * o, axis=0, keepdims=True) + HEAD_NORM_EPS) * nw_ref[...]
        o_ref[0, :, cols] = (o * (1.0 - lam_init)).astype(BF16)


def _diff_attn(q, k, vt, lam_vec, nw, lam_init):
    b, s, _ = q.shape
    tq = (ATTN_Q_TILE // 2) * ATTN_TILES_PER_STEP
    return pl.pallas_call(
        functools.partial(_diff_attn_kernel, lam_init=lam_init),
        grid=(b, DIFF_HEADS, s // tq),
        in_specs=[pl.BlockSpec((1, tq, HEAD_PAD), lambda bi, h, i: (bi, i, h)),
                  pl.BlockSpec((1, s, HEAD_PAD), lambda bi, h, i: (bi, 0, h)),
                  pl.BlockSpec((1, DIFF_VA, s), lambda bi, h, i: (bi, h, 0)),
                  _full(lam_vec.shape), _full(nw.shape)],
        out_specs=pl.BlockSpec((1, DIFF_V_DIM, tq), lambda bi, h, i: (bi, h, i)),
        out_shape=jax.ShapeDtypeStruct((b, DIFF_HEADS * DIFF_V_DIM, s), BF16),
        compiler_params=_cparams(("parallel", "parallel", "parallel")),
        name="diff_attn",
    )(q, k, vt, lam_vec, nw)


def _merge_kernel(x_ref, nw_ref, wg_ref, ret_ref, mlat_ref, difft_ref, ssm_ref, wb_ref, wo_ref, o_ref):
    x = x_ref[0]
    h = _rms(x, nw_ref[...], NORM_EPS).astype(BF16)
    rows = lambda t_ref: t_ref[0].astype(F32).T.astype(BF16)
    outs = (ret_ref[0], rows(mlat_ref), rows(difft_ref), ssm_ref[0])
    merged = None
    for i in range(N_BRANCH):
        gate = _sigmoid(_dot(h, wg_ref[:, i * D_MODEL:(i + 1) * D_MODEL]))
        term = gate * _dot(outs[i], wb_ref[i])
        merged = term if merged is None else merged + term
    o_ref[0] = x + _dot(merged.astype(BF16), wo_ref[...])


def _merge(x, nw, wg, ret, mlat, difft, ssm, wb, wo):
    b, s, _ = x.shape
    tm = TOKEN_TILE
    tok = lambda width: pl.BlockSpec((1, tm, width), lambda bi, i: (bi, i, 0))
    tr = pl.BlockSpec((1, BRANCH_WIDTH, tm), lambda bi, i: (bi, 0, i))
    return pl.pallas_call(
        _merge_kernel,
        grid=(b, s // tm),
        in_specs=[tok(D_MODEL), _full((1, D_MODEL)), _full(wg.shape), tok(512), tr, tr, tok(512),
                  _full(wb.shape), _full(wo.shape)],
        out_specs=tok(D_MODEL),
        out_shape=jax.ShapeDtypeStruct((b, s, D_MODEL), F32),
        compiler_params=_cparams(("parallel", "parallel")),
        name="merge",
    )(x, nw, wg, ret, mlat, difft, ssm, wb, wo)


FFN_COL_TILE = 1408


def _ffn_kernel(xp_ref, x_ref, xn_ref, nw_ref, wgate_ref, wup_ref, cw_ref, cb_ref, wdown_ref, fnw_ref,
                o_ref, g_ref, *, final_norm):
    tm = x_ref.shape[1]
    nw = nw_ref[...]
    x = x_ref[0]
    hm = _rms(x, nw, NORM_EPS)
    hcat = jnp.concatenate([_rms(xp_ref[0], nw, NORM_EPS), hm, _rms(xn_ref[0], nw, NORM_EPS)],
                           axis=0).astype(BF16)
    hb = hm.astype(BF16)
    acc = x
    for j in range(D_FF // FFN_COL_TILE):
        cols = slice(j * FFN_COL_TILE, (j + 1) * FFN_COL_TILE)
        g_ref[...] = _dot(hcat, wgate_ref[:, cols])
        _zero_halo_at_sequence_ends(g_ref, tm)
        u = _conv3(g_ref, tm, cw_ref.at[:, cols], cb_ref.at[:, cols])
        act = (_silu(u) * _dot(hb, wup_ref[:, cols])).astype(BF16)
        acc = acc + _dot(act, wdown_ref[cols, :])
    if final_norm:
        acc = _rms(acc, fnw_ref[...], NORM_EPS)
    o_ref[0] = acc


def _ffn(x, nw, wgate, wup, cw, cb, wdown, fnw, final_norm):
    b, s, _ = x.shape
    tm = FFN_TOKEN_TILE
    return pl.pallas_call(
        functools.partial(_ffn_kernel, final_norm=final_norm),
        grid=(b, s // tm),
        in_specs=[*_halo_specs(tm, s, D_MODEL), _full((1, D_MODEL)), _resident(wgate.shape), _resident(wup.shape),
                  _full(cw.shape), _full(cb.shape), _resident(wdown.shape), _full((1, D_MODEL))],
        out_specs=pl.BlockSpec((1, tm, D_MODEL), lambda bi, i: (bi, i, 0)),
        out_shape=jax.ShapeDtypeStruct((b, s, D_MODEL), F32),
        scratch_shapes=[pltpu.VMEM((tm + 2 * HALO, FFN_COL_TILE), F32)],
        compiler_params=_cparams(("parallel", "parallel"), FFN_VMEM_LIMIT),
        name="ffn",
    )(x, x, x, nw, wgate, wup, cw, cb, wdown, fnw)


def _pad_axis(t, axis, size):
    pad = [(0, 0)] * t.ndim
    pad[axis] = (0, size - t.shape[axis])
    return jnp.pad(t, pad)


def _layer_params(l, p):
    w_in = p["w_in"][l]
    row = lambda t: t.reshape(1, -1).astype(F32)
    d = {}
    d["norm_mix"] = row(p["norm_mix_w"][l])
    d["w_ret"] = w_in[:, _O_RET:_O_MLA_CQ].astype(BF16)
    kr = jnp.pad(w_in[:, _O_MLA_KR:_O_DIFF_QK], ((0, 0), (MLA_NOPE_DIM, HEAD_PAD - MLA_NOPE_DIM - MLA_ROPE_DIM)))
    d["w_mla1"] = jnp.concatenate([w_in[:, _O_MLA_CQ:_O_MLA_KR], kr], axis=1).astype(BF16)
    d["mla_qnw"] = row(p["mla_q_norm_w"][l])
    d["mla_kvnw"] = row(p["mla_kv_norm_w"][l])
    uq = p["mla_w_uq"][l].reshape(MLA_Q_RANK, MLA_HEADS, MLA_NOPE_DIM + MLA_ROPE_DIM)
    d["w_mla_q"] = _pad_axis(uq, 2, HEAD_PAD).reshape(MLA_Q_RANK, MLA_HEADS * HEAD_PAD).astype(BF16)
    r1, r2 = jnp.split(uq[:, :, MLA_NOPE_DIM:], 2, axis=2)
    swapped = jnp.concatenate([jnp.zeros_like(uq[:, :, :MLA_NOPE_DIM]), -r2, r1], axis=2)
    d["w_mla_q_swap"] = _pad_axis(swapped, 2, HEAD_PAD).reshape(MLA_Q_RANK, MLA_HEADS * HEAD_PAD).astype(BF16)
    ukv = p["mla_w_ukv"][l].reshape(MLA_KV_RANK, MLA_HEADS, MLA_NOPE_DIM + MLA_V_DIM)
    d["w_mla_k"] = _pad_axis(ukv[:, :, :MLA_NOPE_DIM], 2, HEAD_PAD).reshape(
        MLA_KV_RANK, MLA_HEADS * HEAD_PAD).astype(BF16)
    vt = jnp.transpose(ukv[:, :, MLA_NOPE_DIM:], (1, 2, 0))
    d["w_mla_vt"] = _pad_axis(vt, 1, MLA_VA).reshape(MLA_HEADS * MLA_VA, MLA_KV_RANK).astype(BF16)
    d["w_diff_qk"] = w_in[:, _O_DIFF_QK:_O_DIFF_V].astype(BF16)
    dvt = w_in[:, _O_DIFF_V:_O_SSM_Z].T.reshape(DIFF_HEADS, DIFF_V_DIM, D_MODEL)
    d["w_diff_vt"] = _pad_axis(dvt, 1, DIFF_VA).reshape(DIFF_HEADS * DIFF_VA, D_MODEL).astype(BF16)
    d["diff_lambda"] = p["diff_lambda"][l].astype(F32)
    d["diff_nw"] = jnp.broadcast_to(p["diff_norm_w"][l].astype(F32)[:, None], (DIFF_V_DIM, ATTN_Q_TILE // 2))
    d["w_ssm_x"] = w_in[:, _O_SSM_XBC:_O_SSM_DT].astype(BF16)
    zdt = jnp.concatenate([w_in[:, _O_SSM_Z:_O_SSM_XBC], w_in[:, _O_SSM_DT:_O_GATE]], axis=1)
    d["w_ssm_zdt"] = _pad_axis(zdt, 1, SSM_INNER + LANES).astype(BF16)
    d["ssm_conv_w"] = p["ssm_conv_w"][l].astype(F32)
    d["ssm_conv_b"] = row(p["ssm_conv_b"][l])
    d["ssm_dt_bias"] = _pad_axis(row(p["ssm_dt_bias"][l]), 1, LANES)
    d["ssm_dt_bias_col"] = p["ssm_dt_bias"][l].astype(F32).reshape(2 * SSM_HEADS, 1)
    d["w_ssm_dtt"] = w_in[:, _O_SSM_DT:_O_GATE].T.astype(BF16)
    d["ssm_alog_row"] = _pad_axis(row(p["ssm_a_log"][l]), 1, LANES)
    d["ssm_alog"] = jnp.broadcast_to(p["ssm_a_log"][l].astype(F32).reshape(2 * SSM_HEADS, 1), (2 * SSM_HEADS, CHUNK))
    d["ssm_dskip"] = jnp.repeat(p["ssm_d"][l].astype(F32), SSM_HEAD_DIM).reshape(1, SSM_INNER)
    d["ssm_nw"] = row(p["ssm_norm_w"][l])
    d["ret_nw"] = row(p["ret_norm_w"][l])
    d["ret_log_decay"] = p["ret_log_decay"][l]
    d["w_gate"] = w_in[:, _O_GATE:_IN_COLS].astype(BF16)
    d["w_branch"] = p["w_branch"][l].astype(BF16)
    d["w_out"] = p["w_out"][l].astype(BF16)
    d["norm_ffn"] = row(p["norm_ffn_w"][l])
    d["ffn_w_gate"] = p["ffn_w_gate"][l].astype(BF16)
    d["ffn_w_up"] = p["ffn_w_up"][l].astype(BF16)
    d["ffn_conv_w"] = p["ffn_conv_w"][l].astype(F32)
    d["ffn_conv_b"] = row(p["ffn_conv_b"][l])
    d["ffn_w_down"] = p["ffn_w_down"][l].astype(BF16)
    d["final_nw"] = row(p["final_norm_w"])
    return d


def _encoder_layer(x, l, d, ropes):
    rope64, rope_mla = ropes
    rq, rk, rv, rg = _ret_in(x, d["norm_mix"], d["w_ret"], rope64)
    dmat, qdf, qdb, kdf, kdb, cdf, cdb = _ret_tables(d["ret_log_decay"])
    rsf, rsb = _ret_state(rk, rv, kdf, kdb, cdf, cdb)
    ret = _ret_out(rq, rk, rv, rg, rsf, rsb, dmat, qdf, qdb, d["ret_nw"])
    mq, mk, mvt = _mla_in(x, d["norm_mix"], d["w_mla1"], d["mla_qnw"], d["mla_kvnw"], d["w_mla_q"],
                          d["w_mla_q_swap"], d["w_mla_k"], d["w_mla_vt"], rope_mla)
    mlat = _mla_attn(mq, mk, mvt)
    dq, dk, dvt = _diff_in(x, d["norm_mix"], d["w_diff_qk"], d["w_diff_vt"], rope64)
    lam_init = 0.8 - 0.6 * math.exp(-0.3 * l)
    difft = _diff_attn(dq, dk, dvt, d["diff_lambda"], d["diff_nw"], lam_init)
    xbc, sz, dt, dtt = _ssm_in(x, d["norm_mix"], d["w_ssm_x"], d["w_ssm_zdt"], d["w_ssm_dtt"], d["ssm_conv_w"],
                               d["ssm_conv_b"], d["ssm_dt_bias"], d["ssm_dt_bias_col"])
    ssf, ssb = _ssd_state(xbc, dtt, d["ssm_alog"])
    ssm = _ssd_out(xbc, dt, dtt, sz, ssf, ssb, d["ssm_alog"], d["ssm_alog_row"], d["ssm_dskip"], d["ssm_nw"])
    x = _merge(x, d["norm_mix"], d["w_gate"], ret, mlat, difft, ssm, d["w_branch"], d["w_out"])
    return _ffn(x, d["norm_ffn"], d["ffn_w_gate"], d["ffn_w_up"], d["ffn_conv_w"], d["ffn_conv_b"],
                d["ffn_w_down"], d["final_nw"], final_norm=(l == DEPTH - 1))


def _trunk(x, layers):
    s = x.shape[1]
    ropes = (_rope_tables(s, RET_QK_DIM, RET_QK_DIM, 0),
             _rope_tables(s, MLA_ROPE_DIM, HEAD_PAD, MLA_NOPE_DIM))
    for l in range(DEPTH):
        x = _encoder_layer(x, l, layers[l], ropes)
    return x


def kernel(x_prompt, x_sample, norm_mix_w, w_in, ret_log_decay, ret_norm_w, mla_q_norm_w, mla_w_uq, mla_kv_norm_w, mla_w_ukv, diff_lambda, diff_norm_w, ssm_conv_w, ssm_conv_b, ssm_dt_bias, ssm_a_log, ssm_d, ssm_norm_w, w_branch, w_out, norm_ffn_w, ffn_w_gate, ffn_w_up, ffn_conv_w, ffn_conv_b, ffn_w_down, final_norm_w):
    p = {
        "norm_mix_w": norm_mix_w, "w_in": w_in, "ret_log_decay": ret_log_decay, "ret_norm_w": ret_norm_w,
        "mla_q_norm_w": mla_q_norm_w, "mla_w_uq": mla_w_uq, "mla_kv_norm_w": mla_kv_norm_w,
        "mla_w_ukv": mla_w_ukv, "diff_lambda": diff_lambda, "diff_norm_w": diff_norm_w,
        "ssm_conv_w": ssm_conv_w, "ssm_conv_b": ssm_conv_b, "ssm_dt_bias": ssm_dt_bias,
        "ssm_a_log": ssm_a_log, "ssm_d": ssm_d, "ssm_norm_w": ssm_norm_w, "w_branch": w_branch,
        "w_out": w_out, "norm_ffn_w": norm_ffn_w, "ffn_w_gate": ffn_w_gate, "ffn_w_up": ffn_w_up,
        "ffn_conv_w": ffn_conv_w, "ffn_conv_b": ffn_conv_b, "ffn_w_down": ffn_w_down,
        "final_norm_w": final_norm_w,
    }
    layers = [_layer_params(l, p) for l in range(DEPTH)]
    return _trunk(x_prompt, layers), _trunk(x_sample, layers)
```

```python
import functools
import math

import jax
import jax.numpy as jnp
from jax import lax
from jax.experimental import pallas as pl
from jax.experimental.pallas import tpu as pltpu

F32 = jnp.float32
BF16 = jnp.bfloat16

D_MODEL = 1024
DEPTH = 2
ROPE_THETA = 10000.0
NORM_EPS = 1e-6
HEAD_NORM_EPS = 1e-5
CHUNK = 128
N_BRANCH = 4
BRANCH_WIDTH = 512

RET_HEADS, RET_QK_DIM, RET_V_DIM = 4, 64, 128
MLA_HEADS, MLA_NOPE_DIM, MLA_ROPE_DIM, MLA_V_DIM = 8, 64, 32, 64
MLA_Q_RANK, MLA_KV_RANK = 256, 128
DIFF_HEADS, DIFF_HEAD_DIM = 4, 64
DIFF_V_DIM = 2 * DIFF_HEAD_DIM
SSM_HEADS, SSM_HEAD_DIM, SSM_GROUPS, SSM_STATE = 8, 64, 2, 128
SSM_INNER = SSM_HEADS * SSM_HEAD_DIM
SSM_CONV_CH = SSM_INNER + 2 * SSM_GROUPS * SSM_STATE
D_FF = 2816

_O_RET = 0
_O_MLA_CQ = 1536
_O_MLA_CKV = 1792
_O_MLA_KR = 1920
_O_DIFF_QK = 1952
_O_DIFF_V = 2976
_O_SSM_Z = 3488
_O_SSM_XBC = 4000
_O_SSM_DT = 5024
_O_GATE = 5040
_IN_COLS = 9136

LANES = 128
SUBLANES = 8
HALO = SUBLANES
PACK_ROWS = 16
HEAD_PAD = LANES
MLA_VA = MLA_V_DIM + PACK_ROWS
DIFF_VA = DIFF_V_DIM + PACK_ROWS
TOKEN_TILE = 512
FFN_TOKEN_TILE = 1024
IN_TOKEN_TILE = 1024
CHUNKS_PER_STEP = 8
ATTN_Q_TILE = 512
ATTN_K_TILE = 256
ATTN_LOOKAHEAD = 3
ATTN_TILES_PER_STEP = 4
LOG2E = 1.4426950408889634
NEG_BIG = -1e30
VMEM_LIMIT = 56 * 1024 * 1024
FFN_VMEM_LIMIT = 60 * 1024 * 1024


def _cparams(sems, vmem=VMEM_LIMIT):
    return pltpu.CompilerParams(dimension_semantics=sems, vmem_limit_bytes=vmem)


def _full(shape):
    return pl.BlockSpec(shape, lambda *_: (0,) * len(shape))


def _resident(shape):
    return pl.BlockSpec(shape, lambda *_: (0,) * len(shape), pipeline_mode=pl.Buffered(1))


def _nt(a, b):
    return lax.dot_general(a, b, (((1,), (1,)), ((), ())), preferred_element_type=F32)


def _dot(a, b):
    return jnp.dot(a, b, preferred_element_type=F32)


def _rms(x, w, eps):
    return x * lax.rsqrt(jnp.mean(x * x, axis=-1, keepdims=True) + eps) * w


def _sigmoid(x):
    return 0.5 * jnp.tanh(0.5 * x) + 0.5


def _silu(x):
    return x * _sigmoid(x)


def _rope(x, c, s1, s2, half):
    return x * c + pltpu.roll(x, LANES - half, 1) * s1 + pltpu.roll(x, half, 1) * s2


def _rope_tables(seq, dim, block, off):
    inv_freq = 1.0 / (ROPE_THETA ** (jnp.arange(0, dim, 2, dtype=F32) / dim))
    ang = jnp.arange(seq, dtype=F32)[:, None] * inv_freq[None, :]
    cos, sin = jnp.cos(ang), jnp.sin(ang)
    half = dim // 2
    zero = jnp.zeros_like(sin)
    pre = jnp.zeros((seq, off), F32)
    post = jnp.zeros((seq, block - off - dim), F32)
    c = jnp.concatenate([pre + 1.0, cos, cos, post + 1.0], axis=1)
    s1 = jnp.concatenate([pre, -sin, zero, post], axis=1)
    s2 = jnp.concatenate([pre, zero, sin, post], axis=1)
    reps = LANES // block
    return tuple(jnp.tile(t, (1, reps)) for t in (c, s1, s2))


def _ret_in_kernel(x_ref, nw_ref, w_ref, c_ref, s1_ref, s2_ref, q_ref, k_ref, v_ref, g_ref):
    h = _rms(x_ref[0], nw_ref[...], NORM_EPS).astype(BF16)
    p = _dot(h, w_ref[...])
    c, s1, s2 = c_ref[...], s1_ref[...], s2_ref[...]
    half = RET_QK_DIM // 2
    for i in range(2):
        sl = slice(i * LANES, (i + 1) * LANES)
        q_ref[0, :, sl] = _rope(p[:, sl], c, s1, s2, half).astype(BF16)
        kk = _rope(p[:, 256 + i * LANES:256 + (i + 1) * LANES], c, s1, s2, half)
        k_ref[0, :, sl] = (kk * (RET_QK_DIM ** -0.5)).astype(BF16)
    v_ref[0] = p[:, 512:1024].astype(BF16)
    g_ref[0] = _silu(p[:, 1024:1536]).astype(BF16)


def _ret_in(x, nw, w, tabs):
    b, s, _ = x.shape
    tm = IN_TOKEN_TILE
    tok = lambda width: pl.BlockSpec((1, tm, width), lambda bi, i: (bi, i, 0))
    tab = pl.BlockSpec((tm, LANES), lambda bi, i: (i, 0))
    return pl.pallas_call(
        _ret_in_kernel,
        grid=(b, s // tm),
        in_specs=[tok(D_MODEL), _full((1, D_MODEL)), _full(w.shape), tab, tab, tab],
        out_specs=[tok(256), tok(256), tok(512), tok(512)],
        out_shape=[jax.ShapeDtypeStruct((b, s, n), BF16) for n in (256, 256, 512, 512)],
        compiler_params=_cparams(("parallel", "parallel")),
        name="ret_in",
    )(x, nw, w, *tabs)


def _ones_rows(rows, cols, period, at):
    r = lax.broadcasted_iota(jnp.int32, (rows, cols), 0)
    return jnp.where(r % period == at, 1.0, 0.0).astype(F32)


def _mla_in_kernel(x_ref, nw_ref, w1_ref, qnw_ref, kvnw_ref, wq_ref, wqsw_ref, wk_ref, wvt_ref,
                   c_ref, s1_ref, s2_ref, q_ref, k_ref, vt_ref):
    h = _rms(x_ref[0], nw_ref[...], NORM_EPS).astype(BF16)
    p = _dot(h, w1_ref[...])
    cqn = _rms(p[:, 0:256], qnw_ref[...], NORM_EPS).astype(BF16)
    ckvn = _rms(p[:, 256:384], kvnw_ref[...], NORM_EPS).astype(BF16)
    c, s1, s2 = c_ref[...], s1_ref[...], s2_ref[...]
    half = MLA_ROPE_DIM // 2
    krr = _rope(p[:, 384:512], c, s1, s2, half)
    qp = _dot(cqn, wq_ref[...])
    qsw = _dot(cqn, wqsw_ref[...])
    sin = s2 - s1
    kp = _dot(ckvn, wk_ref[...])
    qscale = ((MLA_NOPE_DIM + MLA_ROPE_DIM) ** -0.5) * LOG2E
    for hd in range(MLA_HEADS):
        sl = slice(hd * HEAD_PAD, (hd + 1) * HEAD_PAD)
        q_ref[0, :, sl] = ((qp[:, sl] * c + qsw[:, sl] * sin) * qscale).astype(BF16)
        k_ref[0, :, sl] = (kp[:, sl] + krr).astype(BF16)
    vt = _nt(wvt_ref[...], ckvn)
    vt_ref[0] = (vt + _ones_rows(vt.shape[0], vt.shape[1], MLA_VA, MLA_V_DIM)).astype(BF16)


def _mla_in(x, nw, w1, qnw, kvnw, wq, wqsw, wk, wvt, tabs):
    b, s, _ = x.shape
    tm = IN_TOKEN_TILE
    tok = lambda width: pl.BlockSpec((1, tm, width), lambda bi, i: (bi, i, 0))
    tab = pl.BlockSpec((tm, LANES), lambda bi, i: (i, 0))
    rows = MLA_HEADS * MLA_VA
    return pl.pallas_call(
        _mla_in_kernel,
        grid=(b, s // tm),
        in_specs=[tok(D_MODEL), _full((1, D_MODEL)), _full(w1.shape), _full(qnw.shape), _full(kvnw.shape),
                  _full(wq.shape), _full(wqsw.shape), _full(wk.shape), _full(wvt.shape), tab, tab, tab],
        out_specs=[tok(1024), tok(1024), pl.BlockSpec((1, rows, tm), lambda bi, i: (bi, 0, i))],
        out_shape=[jax.ShapeDtypeStruct((b, s, 1024), BF16), jax.ShapeDtypeStruct((b, s, 1024), BF16),
                   jax.ShapeDtypeStruct((b, rows, s), BF16)],
        compiler_params=_cparams(("parallel", "parallel")),
        name="mla_in",
    )(x, nw, w1, qnw, kvnw, wq, wqsw, wk, wvt, *tabs)


def _diff_in_kernel(x_ref, nw_ref, wqk_ref, wvt_ref, c_ref, s1_ref, s2_ref, q_ref, k_ref, vt_ref):
    h = _rms(x_ref[0], nw_ref[...], NORM_EPS).astype(BF16)
    p = _dot(h, wqk_ref[...])
    c, s1, s2 = c_ref[...], s1_ref[...], s2_ref[...]
    half = DIFF_HEAD_DIM // 2
    qscale = (DIFF_HEAD_DIM ** -0.5) * LOG2E
    for hd in range(DIFF_HEADS):
        sl = slice(hd * HEAD_PAD, (hd + 1) * HEAD_PAD)
        q_ref[0, :, sl] = (_rope(p[:, sl], c, s1, s2, half) * qscale).astype(BF16)
        k_ref[0, :, sl] = _rope(p[:, 512 + hd * HEAD_PAD:512 + (hd + 1) * HEAD_PAD], c, s1, s2, half).astype(BF16)
    vt = _nt(wvt_ref[...], h)
    vt_ref[0] = (vt + _ones_rows(vt.shape[0], vt.shape[1], DIFF_VA, DIFF_V_DIM)).astype(BF16)


def _diff_in(x, nw, wqk, wvt, tabs):
    b, s, _ = x.shape
    tm = IN_TOKEN_TILE
    tok = lambda width: pl.BlockSpec((1, tm, width), lambda bi, i: (bi, i, 0))
    tab = pl.BlockSpec((tm, LANES), lambda bi, i: (i, 0))
    rows = DIFF_HEADS * DIFF_VA
    return pl.pallas_call(
        _diff_in_kernel,
        grid=(b, s // tm),
        in_specs=[tok(D_MODEL), _full((1, D_MODEL)), _full(wqk.shape), _full(wvt.shape), tab, tab, tab],
        out_specs=[tok(512), tok(512), pl.BlockSpec((1, rows, tm), lambda bi, i: (bi, 0, i))],
        out_shape=[jax.ShapeDtypeStruct((b, s, 512), BF16), jax.ShapeDtypeStruct((b, s, 512), BF16),
                   jax.ShapeDtypeStruct((b, rows, s), BF16)],
        compiler_params=_cparams(("parallel", "parallel")),
        name="diff_in",
    )(x, nw, wqk, wvt, *tabs)


def _halo_specs(tm, s, width):
    per = tm // HALO
    last = s // HALO - 1
    main = pl.BlockSpec((1, tm, width), lambda bi, i: (bi, i, 0))
    prev = pl.BlockSpec((1, HALO, width), lambda bi, i: (bi, jnp.maximum(i * per - 1, 0), 0))
    nxt = pl.BlockSpec((1, HALO, width), lambda bi, i: (bi, jnp.minimum((i + 1) * per, last), 0))
    return prev, main, nxt


def _conv3(g_ref, tm, w_ref, b_ref):
    u = w_ref[0:1, :] * g_ref[pl.ds(HALO - 1, tm), :]
    u = u + w_ref[1:2, :] * g_ref[pl.ds(HALO, tm), :]
    u = u + w_ref[2:3, :] * g_ref[pl.ds(HALO + 1, tm), :]
    return u + b_ref[...]


CONV_ROW_STRIP = 64
CONV_COL_STRIP = 256


def _conv3_strips(g_ref, tm, w_ref, b_ref, emit):
    for r0 in range(0, tm, CONV_ROW_STRIP):
        for c0 in range(0, g_ref.shape[1], CONV_COL_STRIP):
            rows = slice(r0, r0 + CONV_ROW_STRIP)
            cols = slice(c0, c0 + CONV_COL_STRIP)
            u = w_ref[0:1, cols] * g_ref[HALO - 1 + r0:HALO - 1 + r0 + CONV_ROW_STRIP, cols]
            u = u + w_ref[1:2, cols] * g_ref[HALO + r0:HALO + r0 + CONV_ROW_STRIP, cols]
            u = u + w_ref[2:3, cols] * g_ref[HALO + 1 + r0:HALO + 1 + r0 + CONV_ROW_STRIP, cols]
            emit(rows, cols, u + b_ref[:, cols])


def _zero_halo_at_sequence_ends(g_ref, tm):
    i = pl.program_id(1)

    @pl.when(i == 0)
    def _():
        g_ref[0:HALO, :] = jnp.zeros((HALO, g_ref.shape[1]), F32)

    @pl.when(i == pl.num_programs(1) - 1)
    def _():
        g_ref[HALO + tm:2 * HALO + tm, :] = jnp.zeros((HALO, g_ref.shape[1]), F32)


def _softplus(t):
    return jnp.maximum(t, 0.0) + jnp.log1p(jnp.exp(-jnp.abs(t)))


def _ssm_in_kernel(xp_ref, x_ref, xn_ref, nw_ref, wx_ref, wzd_ref, wdtt_ref, cw_ref, cb_ref, dtb_ref, dtbc_ref,
                   xbc_ref, z_ref, dt_ref, dtt_ref, g_ref):
    tm = x_ref.shape[1]
    nw = nw_ref[...]
    hm = _rms(x_ref[0], nw, NORM_EPS)
    hcat = jnp.concatenate([_rms(xp_ref[0], nw, NORM_EPS), hm, _rms(xn_ref[0], nw, NORM_EPS)], axis=0)
    g_ref[...] = _dot(hcat.astype(BF16), wx_ref[...])
    _zero_halo_at_sequence_ends(g_ref, tm)
    def emit(rows, cols, u):
        xbc_ref[0, rows, cols] = _silu(u)

    _conv3_strips(g_ref, tm, cw_ref, cb_ref, emit)
    hb = hm.astype(BF16)
    zd = _dot(hb, wzd_ref[...])
    z_ref[0] = _silu(zd[:, 0:512])
    sp = _softplus(zd[:, 512:640] + dtb_ref[...])
    lane = lax.broadcasted_iota(jnp.int32, sp.shape, 1)
    dt_ref[0] = jnp.where(lane < 2 * SSM_HEADS, sp, 0.0)
    dtt_ref[0] = _softplus(_nt(wdtt_ref[...], hb) + dtbc_ref[...])


def _ssm_in(x, nw, wx, wzd, wdtt, cw, cb, dtb, dtbc):
    b, s, _ = x.shape
    tm = IN_TOKEN_TILE
    tok = lambda width: pl.BlockSpec((1, tm, width), lambda bi, i: (bi, i, 0))
    return pl.pallas_call(
        _ssm_in_kernel,
        grid=(b, s // tm),
        in_specs=[*_halo_specs(tm, s, D_MODEL), _full((1, D_MODEL)), _full(wx.shape), _full(wzd.shape),
                  _full(wdtt.shape), _full(cw.shape), _full(cb.shape), _full(dtb.shape), _full(dtbc.shape)],
        out_specs=[tok(1024), tok(512), tok(LANES), pl.BlockSpec((1, 2 * SSM_HEADS, tm), lambda bi, i: (bi, 0, i))],
        out_shape=[jax.ShapeDtypeStruct((b, s, 1024), F32), jax.ShapeDtypeStruct((b, s, 512), F32),
                   jax.ShapeDtypeStruct((b, s, LANES), F32), jax.ShapeDtypeStruct((b, 2 * SSM_HEADS, s), F32)],
        scratch_shapes=[pltpu.VMEM((tm + 2 * HALO, SSM_CONV_CH), F32)],
        compiler_params=_cparams(("parallel", "parallel")),
        name="ssm_in",
    )(x, x, x, nw, wx, wzd, wdtt, cw, cb, dtb, dtbc)


def _ret_state_kernel(kf_ref, vf_ref, kb_ref, vb_ref, kdf_ref, kdb_ref, cdf_ref, cdb_ref,
                      sf_ref, sb_ref, st_ref):
    @pl.when(pl.program_id(1) == 0)
    def _():
        st_ref[...] = jnp.zeros(st_ref.shape, F32)

    def scan(d, k_ref, v_ref, kd_ref, cd_ref, out_ref):
        incs = []
        for g in range(CHUNKS_PER_STEP):
            rows = slice(g * CHUNK, (g + 1) * CHUNK)
            kdt = (k_ref[0, rows, :].astype(F32) * kd_ref[...]).T.astype(BF16)
            v = v_ref[0, rows, :]
            incs.append(jnp.concatenate(
                [_dot(kdt[hd * RET_QK_DIM:(hd + 1) * RET_QK_DIM], v[:, hd * RET_V_DIM:(hd + 1) * RET_V_DIM])
                 for hd in range(RET_HEADS)], axis=0))
        st = st_ref[d]
        order = range(CHUNKS_PER_STEP) if d == 0 else range(CHUNKS_PER_STEP - 1, -1, -1)
        for g in order:
            out_ref[0, g] = st
            st = cd_ref[...] * st + incs[g]
        st_ref[d] = st

    scan(0, kf_ref, vf_ref, kdf_ref, cdf_ref, sf_ref)
    scan(1, kb_ref, vb_ref, kdb_ref, cdb_ref, sb_ref)


def _ret_state(k, v, kdf, kdb, cdf, cdb):
    b, s, _ = k.shape
    g = CHUNKS_PER_STEP
    n = s // (g * CHUNK)
    fwd = lambda width: pl.BlockSpec((1, g * CHUNK, width), lambda bi, c: (bi, c, 0))
    bwd = lambda width: pl.BlockSpec((1, g * CHUNK, width), lambda bi, c: (bi, n - 1 - c, 0))
    rows = RET_HEADS * RET_QK_DIM
    st_shape = jax.ShapeDtypeStruct((b, n * g, rows, RET_V_DIM), F32)
    return pl.pallas_call(
        _ret_state_kernel,
        grid=(b, n),
        in_specs=[fwd(256), fwd(512), bwd(256), bwd(512), _full(kdf.shape), _full(kdb.shape),
                  _full(cdf.shape), _full(cdb.shape)],
        out_specs=[pl.BlockSpec((1, g, rows, RET_V_DIM), lambda bi, c: (bi, c, 0, 0)),
                   pl.BlockSpec((1, g, rows, RET_V_DIM), lambda bi, c: (bi, n - 1 - c, 0, 0))],
        out_shape=[st_shape, st_shape],
        scratch_shapes=[pltpu.VMEM((2, rows, RET_V_DIM), F32)],
        compiler_params=_cparams(("parallel", "arbitrary")),
        name="ret_state",
    )(k, v, k, v, kdf, kdb, cdf, cdb)


def _ret_out_kernel(q_ref, k_ref, v_ref, g_ref, sf_ref, sb_ref, d_ref, qdf_ref, qdb_ref, nw_ref, o_ref):
    gs = range(CHUNKS_PER_STEP)
    rows = [slice(g * CHUNK, (g + 1) * CHUNK) for g in gs]
    lane = lax.broadcasted_iota(jnp.int32, (CHUNK, LANES), 1)
    q = [q_ref[0, rows[g], :].astype(F32) for g in gs]
    qf = [q[g] * qdf_ref[...] for g in gs]
    qb = [q[g] * qdb_ref[...] for g in gs]
    for hd in range(RET_HEADS):
        pair = slice((hd // 2) * LANES, (hd // 2 + 1) * LANES)
        own = (lane < RET_QK_DIM) if hd % 2 == 0 else (lane >= RET_QK_DIM)
        pick = lambda t: jnp.where(own, t[:, pair], 0.0).astype(BF16)
        vsl = slice(hd * RET_V_DIM, (hd + 1) * RET_V_DIM)
        sc = [_nt(pick(q[g]), k_ref[0, rows[g], pair]) * d_ref[hd] for g in gs]
        cross = [_dot(pick(qf[g]), sf_ref[0, g, pair, :].astype(BF16))
                 + _dot(pick(qb[g]), sb_ref[0, g, pair, :].astype(BF16)) for g in gs]
        y = [_dot(sc[g].astype(BF16), v_ref[0, rows[g], vsl]) + cross[g] for g in gs]
        for g in gs:
            yn = _rms(y[g], nw_ref[:, vsl], HEAD_NORM_EPS)
            o_ref[0, rows[g], vsl] = (g_ref[0, rows[g], vsl].astype(F32) * yn).astype(BF16)


def _ret_out(q, k, v, g, sf, sb, dmat, qdf, qdb, nw):
    b, s, _ = q.shape
    gc = CHUNKS_PER_STEP
    n = s // (gc * CHUNK)
    tok = lambda width: pl.BlockSpec((1, gc * CHUNK, width), lambda bi, c: (bi, c, 0))
    st = pl.BlockSpec((1, gc) + sf.shape[2:], lambda bi, c: (bi, c, 0, 0))
    return pl.pallas_call(
        _ret_out_kernel,
        grid=(b, n),
        in_specs=[tok(256), tok(256), tok(512), tok(512), st, st, _full(dmat.shape), _full(qdf.shape),
                  _full(qdb.shape), _full(nw.shape)],
        out_specs=tok(512),
        out_shape=jax.ShapeDtypeStruct((b, s, 512), BF16),
        compiler_params=_cparams(("parallel", "parallel")),
        name="ret_out",
    )(q, k, v, g, sf, sb, dmat, qdf, qdb, nw)


def _ret_tables(log_decay):
    lg_f = log_decay[0].astype(F32)
    lg_b = log_decay[1].astype(F32)
    pos = jnp.arange(CHUNK, dtype=F32)
    rel = pos[:, None] - pos[None, :]
    low = jnp.where(rel >= 0, jnp.exp(jnp.maximum(rel, 0.0)[None] * lg_f[:, None, None]), 0.0)
    upp = jnp.where(rel <= 0, jnp.exp(jnp.maximum(-rel, 0.0)[None] * lg_b[:, None, None]), 0.0)
    dmat = low + upp
    wide = lambda t: jnp.repeat(t, RET_QK_DIM, axis=1)
    qdf = wide(jnp.exp((pos + 1.0)[:, None] * lg_f))
    qdb = wide(jnp.exp((CHUNK - pos)[:, None] * lg_b))
    kdf = wide(jnp.exp((CHUNK - 1.0 - pos)[:, None] * lg_f))
    kdb = wide(jnp.exp(pos[:, None] * lg_b))
    tall = lambda lg: jnp.broadcast_to(jnp.repeat(jnp.exp(CHUNK * lg), RET_QK_DIM)[:, None],
                                       (RET_HEADS * RET_QK_DIM, RET_V_DIM))
    return dmat, qdf, qdb, kdf, kdb, tall(lg_f), tall(lg_b)


def _split3(x):
    hi = x.astype(BF16)
    r1 = x - hi.astype(F32)
    mid = r1.astype(BF16)
    lo = (r1 - mid.astype(F32)).astype(BF16)
    return hi, mid, lo


def _tri(lower):
    ii = lax.broadcasted_iota(jnp.int32, (CHUNK, CHUNK), 0)
    jj = lax.broadcasted_iota(jnp.int32, (CHUNK, CHUNK), 1)
    return jnp.where((ii >= jj) if lower else (jj >= ii), 1.0, 0.0).astype(BF16)


def _cumsum_time_on_lanes(xt, tri):
    hi, mid, lo = _split3(xt)
    return _nt(hi, tri) + _nt(mid, tri) + _nt(lo, tri)


def _cumsum_time_on_rows(x, tri):
    hi, mid, lo = _split3(x)
    return _dot(tri, hi) + _dot(tri, mid) + _dot(tri, lo)


def _ssd_state_kernel(xf_ref, dtf_ref, xb_ref, dtb_ref, alog_ref, sf_ref, sb_ref, st_ref):
    @pl.when(pl.program_id(1) == 0)
    def _():
        st_ref[...] = jnp.zeros(st_ref.shape, F32)

    head_of_lane = lax.broadcasted_iota(jnp.int32, (1, SSM_INNER), 1) // SSM_HEAD_DIM

    per_group = SSM_INNER // SSM_GROUPS

    def scan(d, x_ref, dt_ref, out_ref):
        gs = range(CHUNKS_PER_STEP)
        rows = [slice(g * CHUNK, (g + 1) * CHUNK) for g in gs]
        r0 = d * SSM_HEADS
        a = -jnp.exp(alog_ref[r0:r0 + SSM_HEADS, :])
        tri = _tri(lower=(d == 0))
        dtt = [dt_ref[0, r0:r0 + SSM_HEADS, rows[g]] for g in gs]
        cum = [_cumsum_time_on_lanes(dtt[g] * a, tri) for g in gs]
        endb = [jnp.broadcast_to(cum[g][:, 0:1] if d == 1 else cum[g][:, CHUNK - 1:CHUNK], cum[g].shape)
                for g in gs]
        wt = [jnp.exp(endb[g] - cum[g]) * dtt[g] for g in gs]
        edec = [jnp.exp(endb[g]) for g in gs]
        xt = [x_ref[0, rows[g], 0:SSM_INNER].T for g in gs]
        bt = [[x_ref[0, rows[g], SSM_INNER + k * SSM_STATE:SSM_INNER + (k + 1) * SSM_STATE].T.astype(BF16)
               for k in range(SSM_GROUPS)] for g in gs]
        xwt = [jnp.concatenate([xt[g][hd * SSM_HEAD_DIM:(hd + 1) * SSM_HEAD_DIM] * wt[g][hd:hd + 1]
                                for hd in range(SSM_HEADS)], axis=0).astype(BF16) for g in gs]
        inc = [jnp.concatenate([_nt(bt[g][k], xwt[g][k * per_group:(k + 1) * per_group])
                                for k in range(SSM_GROUPS)], axis=1) for g in gs]
        dec = []
        for g in gs:
            row_dec = jnp.zeros((1, SSM_INNER), F32)
            for hd in range(SSM_HEADS):
                row = jnp.concatenate([edec[g][hd:hd + 1]] * (SSM_INNER // LANES), axis=1)
                row_dec = jnp.where(head_of_lane == hd, row, row_dec)
            dec.append(row_dec)
        st = st_ref[d]
        for g in (gs if d == 0 else reversed(gs)):
            out_ref[0, g] = st
            st = st * dec[g] + inc[g]
        st_ref[d] = st

    scan(0, xf_ref, dtf_ref, sf_ref)
    scan(1, xb_ref, dtb_ref, sb_ref)


def _ssd_state(xbc, dtt, alog):
    b, s, _ = xbc.shape
    g = CHUNKS_PER_STEP
    n = s // (g * CHUNK)
    fwd = lambda width: pl.BlockSpec((1, g * CHUNK, width), lambda bi, c: (bi, c, 0))
    bwd = lambda width: pl.BlockSpec((1, g * CHUNK, width), lambda bi, c: (bi, n - 1 - c, 0))
    dt_fwd = pl.BlockSpec((1, 2 * SSM_HEADS, g * CHUNK), lambda bi, c: (bi, 0, c))
    dt_bwd = pl.BlockSpec((1, 2 * SSM_HEADS, g * CHUNK), lambda bi, c: (bi, 0, n - 1 - c))
    st_shape = jax.ShapeDtypeStruct((b, n * g, SSM_STATE, SSM_INNER), F32)
    return pl.pallas_call(
        _ssd_state_kernel,
        grid=(b, n),
        in_specs=[fwd(768), dt_fwd, bwd(768), dt_bwd, _full(alog.shape)],
        out_specs=[pl.BlockSpec((1, g, SSM_STATE, SSM_INNER), lambda bi, c: (bi, c, 0, 0)),
                   pl.BlockSpec((1, g, SSM_STATE, SSM_INNER), lambda bi, c: (bi, n - 1 - c, 0, 0))],
        out_shape=[st_shape, st_shape],
        scratch_shapes=[pltpu.VMEM((2, SSM_STATE, SSM_INNER), F32)],
        compiler_params=_cparams(("parallel", "arbitrary")),
        name="ssd_state",
    )(xbc, dtt, xbc, dtt, alog)


def _ssd_out_kernel(x_ref, dt_ref, dtt_ref, z_ref, sf_ref, sb_ref, alog_ref, arow_ref, dskip_ref, nw_ref, o_ref):
    gs = range(CHUNKS_PER_STEP)
    rows = [slice(g * CHUNK, (g + 1) * CHUNK) for g in gs]
    nh = SSM_HEADS
    low_t, upp_t = _tri(True), _tri(False)
    a_t = -jnp.exp(alog_ref[...])
    a_row = -jnp.exp(arow_ref[...])
    dtt = [dtt_ref[0, :, rows[g]] for g in gs]
    cumt = [jnp.concatenate([_cumsum_time_on_lanes(dtt[g][0:nh] * a_t[0:nh], low_t),
                             _cumsum_time_on_lanes(dtt[g][nh:2 * nh] * a_t[nh:2 * nh], upp_t)], axis=0) for g in gs]
    da = [dt_ref[0, rows[g], :] * a_row for g in gs]
    head_lane = lax.broadcasted_iota(jnp.int32, (CHUNK, LANES), 1)
    cum = [jnp.where(head_lane < nh, _cumsum_time_on_rows(da[g], low_t), _cumsum_time_on_rows(da[g], upp_t))
           for g in gs]
    ii = lax.broadcasted_iota(jnp.int32, (CHUNK, CHUNK), 0)
    jj = lax.broadcasted_iota(jnp.int32, (CHUNK, CHUNK), 1)
    low = ii >= jj
    upp = jj >= ii
    first = jj < SSM_HEAD_DIM
    heads_per_group = nh // SSM_GROUPS
    c0 = SSM_INNER + SSM_GROUPS * SSM_STATE
    ys = [[] for _ in gs]
    for pr in range(nh // 2):
        k = (2 * pr) // heads_per_group
        pair = slice(pr * LANES, (pr + 1) * LANES)
        cg = [x_ref[0, rows[g], c0 + k * SSM_STATE:c0 + (k + 1) * SSM_STATE].astype(BF16) for g in gs]
        bg = [x_ref[0, rows[g], SSM_INNER + k * SSM_STATE:SSM_INNER + (k + 1) * SSM_STATE].astype(BF16) for g in gs]
        cb = [_nt(cg[g], bg[g]) for g in gs]
        off = [_dot(cg[g], sf_ref[0, g, :, pair].astype(BF16)) for g in gs]
        offb = [_dot(cg[g], sb_ref[0, g, :, pair].astype(BF16)) for g in gs]
        diag = [[], []]
        ecol = [[], []]
        for e, hd in enumerate((2 * pr, 2 * pr + 1)):
            colf = [jnp.broadcast_to(cum[g][:, hd:hd + 1], (CHUNK, CHUNK)) for g in gs]
            colb = [jnp.broadcast_to(cum[g][:, nh + hd:nh + hd + 1], (CHUNK, CHUNK)) for g in gs]
            segf = [jnp.exp(jnp.where(low, colf[g] - cumt[g][hd:hd + 1], NEG_BIG)) for g in gs]
            segb = [jnp.exp(jnp.where(upp, colb[g] - cumt[g][nh + hd:nh + hd + 1], NEG_BIG)) for g in gs]
            w = [cb[g] * (segf[g] * dtt[g][hd:hd + 1] + segb[g] * dtt[g][nh + hd:nh + hd + 1]) for g in gs]
            diag[e] = [_dot(w[g].astype(BF16), x_ref[0, rows[g], pair].astype(BF16)) for g in gs]
            ecol[e] = [(jnp.exp(colf[g]), jnp.exp(colb[g])) for g in gs]
        for g in gs:
            yd = jnp.where(first, diag[0][g], diag[1][g])
            ef = jnp.where(first, ecol[0][g][0], ecol[1][g][0])
            eb = jnp.where(first, ecol[0][g][1], ecol[1][g][1])
            ys[g].append(yd + ef * off[g] + eb * offb[g])
    gw = SSM_INNER // SSM_GROUPS
    for g in gs:
        y = jnp.concatenate(ys[g], axis=1) + dskip_ref[...] * x_ref[0, rows[g], 0:SSM_INNER]
        y = y * z_ref[0, rows[g], :]
        for k in range(SSM_GROUPS):
            sl = slice(k * gw, (k + 1) * gw)
            o_ref[0, rows[g], sl] = _rms(y[:, sl], nw_ref[:, sl], HEAD_NORM_EPS).astype(BF16)


def _ssd_out(xbc, dt, dtt, z, sf, sb, alog, arow, dskip, nw):
    b, s, _ = xbc.shape
    gc = CHUNKS_PER_STEP
    n = s // (gc * CHUNK)
    tok = lambda width: pl.BlockSpec((1, gc * CHUNK, width), lambda bi, c: (bi, c, 0))
    st = pl.BlockSpec((1, gc, SSM_STATE, SSM_INNER), lambda bi, c: (bi, c, 0, 0))
    return pl.pallas_call(
        _ssd_out_kernel,
        grid=(b, n),
        in_specs=[tok(1024), tok(LANES), pl.BlockSpec((1, 2 * SSM_HEADS, gc * CHUNK), lambda bi, c: (bi, 0, c)),
                  tok(512), st, st, _full(alog.shape), _full(arow.shape), _full(dskip.shape), _full(nw.shape)],
        out_specs=tok(512),
        out_shape=jax.ShapeDtypeStruct((b, s, 512), BF16),
        compiler_params=_cparams(("parallel", "parallel")),
        name="ssd_out",
    )(xbc, dt, dtt, z, sf, sb, alog, arow, dskip, nw)


def _attn_scores_t(q, k_ref, vt_ref, seq, column_halves):
    tk = min(ATTN_K_TILE, seq)
    rows = vt_ref.shape[1]

    m = jnp.full((1, ATTN_Q_TILE), NEG_BIG, F32)
    acc = jnp.zeros((rows, ATTN_Q_TILE), F32)
    n = seq // tk
    scores = lambda c: _nt(k_ref[0, c * tk:(c + 1) * tk, :], q)
    pending = [scores(c) for c in range(min(ATTN_LOOKAHEAD, n))]
    for c in range(n):
        s = pending.pop(0)
        if c + ATTN_LOOKAHEAD < n:
            pending.append(scores(c + ATTN_LOOKAHEAD))
        m_new = jnp.maximum(m, jnp.max(s, axis=0, keepdims=True))
        v = vt_ref[0, :, c * tk:(c + 1) * tk]
        hw = ATTN_Q_TILE // column_halves
        pv = [_dot(v, jnp.exp2(s[:, i * hw:(i + 1) * hw] - m_new[:, i * hw:(i + 1) * hw]).astype(BF16))
              for i in range(column_halves)]
        acc = acc * jnp.exp2(m - m_new) + (pv[0] if column_halves == 1 else jnp.concatenate(pv, axis=1))
        m = m_new
    return acc


def _mla_attn_kernel(q_ref, k_ref, vt_ref, o_ref):
    tq = ATTN_Q_TILE
    for t in range(ATTN_TILES_PER_STEP):
        cols = slice(t * tq, (t + 1) * tq)
        acc = _attn_scores_t(q_ref[0, cols, :], k_ref, vt_ref, k_ref.shape[1], column_halves=2)
        o_ref[0, :, cols] = (acc[0:MLA_V_DIM] / acc[MLA_V_DIM:MLA_V_DIM + 1]).astype(BF16)


def _mla_attn(q, k, vt):
    b, s, _ = q.shape
    tq = ATTN_Q_TILE * ATTN_TILES_PER_STEP
    return pl.pallas_call(
        _mla_attn_kernel,
        grid=(b, MLA_HEADS, s // tq),
        in_specs=[pl.BlockSpec((1, tq, HEAD_PAD), lambda bi, h, i: (bi, i, h)),
                  pl.BlockSpec((1, s, HEAD_PAD), lambda bi, h, i: (bi, 0, h)),
                  pl.BlockSpec((1, MLA_VA, s), lambda bi, h, i: (bi, h, 0))],
        out_specs=pl.BlockSpec((1, MLA_V_DIM, tq), lambda bi, h, i: (bi, h, i)),
        out_shape=jax.ShapeDtypeStruct((b, MLA_HEADS * MLA_V_DIM, s), BF16),
        compiler_params=_cparams(("parallel", "parallel", "parallel")),
        name="mla_attn",
    )(q, k, vt)


def _diff_attn_kernel(q_ref, k_ref, vt_ref, lam_ref, nw_ref, o_ref, *, lam_init):
    tq = ATTN_Q_TILE // 2
    dv = DIFF_V_DIM
    lv = lam_ref[...]
    lam = (jnp.exp(jnp.sum(lv[0:1] * lv[1:2], axis=1, keepdims=True))
           - jnp.exp(jnp.sum(lv[2:3] * lv[3:4], axis=1, keepdims=True)) + lam_init)
    for t in range(ATTN_TILES_PER_STEP):
        cols = slice(t * tq, (t + 1) * tq)
        q = q_ref[0, cols, :].astype(F32)
        lane = lax.broadcasted_iota(jnp.int32, q.shape, 1)
        first = lane < DIFF_HEAD_DIM
        both = jnp.concatenate([jnp.where(first, q, 0.0), jnp.where(first, 0.0, q)], axis=0).astype(BF16)
        acc = _attn_scores_t(both, k_ref, vt_ref, k_ref.shape[1], column_halves=1)
        o1 = acc[0:dv, 0:tq] / acc[dv:dv + 1, 0:tq]
        o2 = acc[0:dv, tq:2 * tq] / acc[dv:dv + 1, tq:2 * tq]
        o = o1 - lam * o2
        o = o * lax.rsqrt(jnp.mean(o * o, axis=0, keepdims=True) + HEAD_NORM_EPS) * nw_ref[...]
        o_ref[0, :, cols] = (o * (1.0 - lam_init)).astype(BF16)


def _diff_attn(q, k, vt, lam_vec, nw, lam_init):
    b, s, _ = q.shape
    tq = (ATTN_Q_TILE // 2) * ATTN_TILES_PER_STEP
    return pl.pallas_call(
        functools.partial(_diff_attn_kernel, lam_init=lam_init),
        grid=(b, DIFF_HEADS, s // tq),
        in_specs=[pl.BlockSpec((1, tq, HEAD_PAD), lambda bi, h, i: (bi, i, h)),
                  pl.BlockSpec((1, s, HEAD_PAD), lambda bi, h, i: (bi, 0, h)),
                  pl.BlockSpec((1, DIFF_VA, s), lambda bi, h, i: (bi, h, 0)),
                  _full(lam_vec.shape), _full(nw.shape)],
        out_specs=pl.BlockSpec((1, DIFF_V_DIM, tq), lambda bi, h, i: (bi, h, i)),
        out_shape=jax.ShapeDtypeStruct((b, DIFF_HEADS * DIFF_V_DIM, s), BF16),
        compiler_params=_cparams(("parallel", "parallel", "parallel")),
        name="diff_attn",
    )(q, k, vt, lam_vec, nw)


def _merge_kernel(x_ref, nw_ref, wg_ref, ret_ref, mlat_ref, difft_ref, ssm_ref, wb_ref, wo_ref, o_ref):
    x = x_ref[0]
    h = _rms(x, nw_ref[...], NORM_EPS).astype(BF16)
    rows = lambda t_ref: t_ref[0].astype(F32).T.astype(BF16)
    outs = (ret_ref[0], rows(mlat_ref), rows(difft_ref), ssm_ref[0])
    merged = None
    for i in range(N_BRANCH):
        gate = _sigmoid(_dot(h, wg_ref[:, i * D_MODEL:(i + 1) * D_MODEL]))
        term = gate * _dot(outs[i], wb_ref[i])
        merged = term if merged is None else merged + term
    o_ref[0] = x + _dot(merged.astype(BF16), wo_ref[...])


def _merge(x, nw, wg, ret, mlat, difft, ssm, wb, wo):
    b, s, _ = x.shape
    tm = TOKEN_TILE
    tok = lambda width: pl.BlockSpec((1, tm, width), lambda bi, i: (bi, i, 0))
    tr = pl.BlockSpec((1, BRANCH_WIDTH, tm), lambda bi, i: (bi, 0, i))
    return pl.pallas_call(
        _merge_kernel,
        grid=(b, s // tm),
        in_specs=[tok(D_MODEL), _full((1, D_MODEL)), _full(wg.shape), tok(512), tr, tr, tok(512),
                  _full(wb.shape), _full(wo.shape)],
        out_specs=tok(D_MODEL),
        out_shape=jax.ShapeDtypeStruct((b, s, D_MODEL), F32),
        compiler_params=_cparams(("parallel", "parallel")),
        name="merge",
    )(x, nw, wg, ret, mlat, difft, ssm, wb, wo)


FFN_COL_TILE = 1408


def _ffn_kernel(xp_ref, x_ref, xn_ref, nw_ref, wgate_ref, wup_ref, cw_ref, cb_ref, wdown_ref, fnw_ref,
                o_ref, g_ref, *, final_norm):
    tm = x_ref.shape[1]
    nw = nw_ref[...]
    x = x_ref[0]
    hm = _rms(x, nw, NORM_EPS)
    hcat = jnp.concatenate([_rms(xp_ref[0], nw, NORM_EPS), hm, _rms(xn_ref[0], nw, NORM_EPS)],
                           axis=0).astype(BF16)
    hb = hm.astype(BF16)
    acc = x
    for j in range(D_FF // FFN_COL_TILE):
        cols = slice(j * FFN_COL_TILE, (j + 1) * FFN_COL_TILE)
        g_ref[...] = _dot(hcat, wgate_ref[:, cols])
        _zero_halo_at_sequence_ends(g_ref, tm)
        u = _conv3(g_ref, tm, cw_ref.at[:, cols], cb_ref.at[:, cols])
        act = (_silu(u) * _dot(hb, wup_ref[:, cols])).astype(BF16)
        acc = acc + _dot(act, wdown_ref[cols, :])
    if final_norm:
        acc = _rms(acc, fnw_ref[...], NORM_EPS)
    o_ref[0] = acc


def _ffn(x, nw, wgate, wup, cw, cb, wdown, fnw, final_norm):
    b, s, _ = x.shape
    tm = FFN_TOKEN_TILE
    return pl.pallas_call(
        functools.partial(_ffn_kernel, final_norm=final_norm),
        grid=(b, s // tm),
        in_specs=[*_halo_specs(tm, s, D_MODEL), _full((1, D_MODEL)), _resident(wgate.shape), _resident(wup.shape),
                  _full(cw.shape), _full(cb.shape), _resident(wdown.shape), _full((1, D_MODEL))],
        out_specs=pl.BlockSpec((1, tm, D_MODEL), lambda bi, i: (bi, i, 0)),
        out_shape=jax.ShapeDtypeStruct((b, s, D_MODEL), F32),
        scratch_shapes=[pltpu.VMEM((tm + 2 * HALO, FFN_COL_TILE), F32)],
        compiler_params=_cparams(("parallel", "parallel"), FFN_VMEM_LIMIT),
        name="ffn",
    )(x, x, x, nw, wgate, wup, cw, cb, wdown, fnw)


def _pad_axis(t, axis, size):
    pad = [(0, 0)] * t.ndim
    pad[axis] = (0, size - t.shape[axis])
    return jnp.pad(t, pad)


def _layer_params(l, p):
    w_in = p["w_in"][l]
    row = lambda t: t.reshape(1, -1).astype(F32)
    d = {}
    d["norm_mix"] = row(p["norm_mix_w"][l])
    d["w_ret"] = w_in[:, _O_RET:_O_MLA_CQ].astype(BF16)
    kr = jnp.pad(w_in[:, _O_MLA_KR:_O_DIFF_QK], ((0, 0), (MLA_NOPE_DIM, HEAD_PAD - MLA_NOPE_DIM - MLA_ROPE_DIM)))
    d["w_mla1"] = jnp.concatenate([w_in[:, _O_MLA_CQ:_O_MLA_KR], kr], axis=1).astype(BF16)
    d["mla_qnw"] = row(p["mla_q_norm_w"][l])
    d["mla_kvnw"] = row(p["mla_kv_norm_w"][l])
    uq = p["mla_w_uq"][l].reshape(MLA_Q_RANK, MLA_HEADS, MLA_NOPE_DIM + MLA_ROPE_DIM)
    d["w_mla_q"] = _pad_axis(uq, 2, HEAD_PAD).reshape(MLA_Q_RANK, MLA_HEADS * HEAD_PAD).astype(BF16)
    r1, r2 = jnp.split(uq[:, :, MLA_NOPE_DIM:], 2, axis=2)
    swapped = jnp.concatenate([jnp.zeros_like(uq[:, :, :MLA_NOPE_DIM]), -r2, r1], axis=2)
    d["w_mla_q_swap"] = _pad_axis(swapped, 2, HEAD_PAD).reshape(MLA_Q_RANK, MLA_HEADS * HEAD_PAD).astype(BF16)
    ukv = p["mla_w_ukv"][l].reshape(MLA_KV_RANK, MLA_HEADS, MLA_NOPE_DIM + MLA_V_DIM)
    d["w_mla_k"] = _pad_axis(ukv[:, :, :MLA_NOPE_DIM], 2, HEAD_PAD).reshape(
        MLA_KV_RANK, MLA_HEADS * HEAD_PAD).astype(BF16)
    vt = jnp.transpose(ukv[:, :, MLA_NOPE_DIM:], (1, 2, 0))
    d["w_mla_vt"] = _pad_axis(vt, 1, MLA_VA).reshape(MLA_HEADS * MLA_VA, MLA_KV_RANK).astype(BF16)
    d["w_diff_qk"] = w_in[:, _O_DIFF_QK:_O_DIFF_V].astype(BF16)
    dvt = w_in[:, _O_DIFF_V:_O_SSM_Z].T.reshape(DIFF_HEADS, DIFF_V_DIM, D_MODEL)
    d["w_diff_vt"] = _pad_axis(dvt, 1, DIFF_VA).reshape(DIFF_HEADS * DIFF_VA, D_MODEL).astype(BF16)
    d["diff_lambda"] = p["diff_lambda"][l].astype(F32)
    d["diff_nw"] = jnp.broadcast_to(p["diff_norm_w"][l].astype(F32)[:, None], (DIFF_V_DIM, ATTN_Q_TILE // 2))
    d["w_ssm_x"] = w_in[:, _O_SSM_XBC:_O_SSM_DT].astype(BF16)
    zdt = jnp.concatenate([w_in[:, _O_SSM_Z:_O_SSM_XBC], w_in[:, _O_SSM_DT:_O_GATE]], axis=1)
    d["w_ssm_zdt"] = _pad_axis(zdt, 1, SSM_INNER + LANES).astype(BF16)
    d["ssm_conv_w"] = p["ssm_conv_w"][l].astype(F32)
    d["ssm_conv_b"] = row(p["ssm_conv_b"][l])
    d["ssm_dt_bias"] = _pad_axis(row(p["ssm_dt_bias"][l]), 1, LANES)
    d["ssm_dt_bias_col"] = p["ssm_dt_bias"][l].astype(F32).reshape(2 * SSM_HEADS, 1)
    d["w_ssm_dtt"] = w_in[:, _O_SSM_DT:_O_GATE].T.astype(BF16)
    d["ssm_alog_row"] = _pad_axis(row(p["ssm_a_log"][l]), 1, LANES)
    d["ssm_alog"] = jnp.broadcast_to(p["ssm_a_log"][l].astype(F32).reshape(2 * SSM_HEADS, 1), (2 * SSM_HEADS, CHUNK))
    d["ssm_dskip"] = jnp.repeat(p["ssm_d"][l].astype(F32), SSM_HEAD_DIM).reshape(1, SSM_INNER)
    d["ssm_nw"] = row(p["ssm_norm_w"][l])
    d["ret_nw"] = row(p["ret_norm_w"][l])
    d["ret_log_decay"] = p["ret_log_decay"][l]
    d["w_gate"] = w_in[:, _O_GATE:_IN_COLS].astype(BF16)
    d["w_branch"] = p["w_branch"][l].astype(BF16)
    d["w_out"] = p["w_out"][l].astype(BF16)
    d["norm_ffn"] = row(p["norm_ffn_w"][l])
    d["ffn_w_gate"] = p["ffn_w_gate"][l].astype(BF16)
    d["ffn_w_up"] = p["ffn_w_up"][l].astype(BF16)
    d["ffn_conv_w"] = p["ffn_conv_w"][l].astype(F32)
    d["ffn_conv_b"] = row(p["ffn_conv_b"][l])
    d["ffn_w_down"] = p["ffn_w_down"][l].astype(BF16)
    d["final_nw"] = row(p["final_norm_w"])
    return d


def _encoder_layer(x, l, d, ropes):
    rope64, rope_mla = ropes
    rq, rk, rv, rg = _ret_in(x, d["norm_mix"], d["w_ret"], rope64)
    dmat, qdf, qdb, kdf, kdb, cdf, cdb = _ret_tables(d["ret_log_decay"])
    rsf, rsb = _ret_state(rk, rv, kdf, kdb, cdf, cdb)
    ret = _ret_out(rq, rk, rv, rg, rsf, rsb, dmat, qdf, qdb, d["ret_nw"])
    mq, mk, mvt = _mla_in(x, d["norm_mix"], d["w_mla1"], d["mla_qnw"], d["mla_kvnw"], d["w_mla_q"],
                          d["w_mla_q_swap"], d["w_mla_k"], d["w_mla_vt"], rope_mla)
    mlat = _mla_attn(mq, mk, mvt)
    dq, dk, dvt = _diff_in(x, d["norm_mix"], d["w_diff_qk"], d["w_diff_vt"], rope64)
    lam_init = 0.8 - 0.6 * math.exp(-0.3 * l)
    difft = _diff_attn(dq, dk, dvt, d["diff_lambda"], d["diff_nw"], lam_init)
    xbc, sz, dt, dtt = _ssm_in(x, d["norm_mix"], d["w_ssm_x"], d["w_ssm_zdt"], d["w_ssm_dtt"], d["ssm_conv_w"],
                               d["ssm_conv_b"], d["ssm_dt_bias"], d["ssm_dt_bias_col"])
    ssf, ssb = _ssd_state(xbc, dtt, d["ssm_alog"])
    ssm = _ssd_out(xbc, dt, dtt, sz, ssf, ssb, d["ssm_alog"], d["ssm_alog_row"], d["ssm_dskip"], d["ssm_nw"])
    x = _merge(x, d["norm_mix"], d["w_gate"], ret, mlat, difft, ssm, d["w_branch"], d["w_out"])
    return _ffn(x, d["norm_ffn"], d["ffn_w_gate"], d["ffn_w_up"], d["ffn_conv_w"], d["ffn_conv_b"],
                d["ffn_w_down"], d["final_nw"], final_norm=(l == DEPTH - 1))


def _trunk(x, layers):
    s = x.shape[1]
    ropes = (_rope_tables(s, RET_QK_DIM, RET_QK_DIM, 0),
             _rope_tables(s, MLA_ROPE_DIM, HEAD_PAD, MLA_NOPE_DIM))
    for l in range(DEPTH):
        x = _encoder_layer(x, l, layers[l], ropes)
    return x


def kernel(x_prompt, x_sample, norm_mix_w, w_in, ret_log_decay, ret_norm_w, mla_q_norm_w, mla_w_uq, mla_kv_norm_w, mla_w_ukv, diff_lambda, diff_norm_w, ssm_conv_w, ssm_conv_b, ssm_dt_bias, ssm_a_log, ssm_d, ssm_norm_w, w_branch, w_out, norm_ffn_w, ffn_w_gate, ffn_w_up, ffn_conv_w, ffn_conv_b, ffn_w_down, final_norm_w):
    p = {
        "norm_mix_w": norm_mix_w, "w_in": w_in, "ret_log_decay": ret_log_decay, "ret_norm_w": ret_norm_w,
        "mla_q_norm_w": mla_q_norm_w, "mla_w_uq": mla_w_uq, "mla_kv_norm_w": mla_kv_norm_w,
        "mla_w_ukv": mla_w_ukv, "diff_lambda": diff_lambda, "diff_norm_w": diff_norm_w,
        "ssm_conv_w": ssm_conv_w, "ssm_conv_b": ssm_conv_b, "ssm_dt_bias": ssm_dt_bias,
        "ssm_a_log": ssm_a_log, "ssm_d": ssm_d, "ssm_norm_w": ssm_norm_w, "w_branch": w_branch,
        "w_out": w_out, "norm_ffn_w": norm_ffn_w, "ffn_w_gate": ffn_w_gate, "ffn_w_up": ffn_w_up,
        "ffn_conv_w": ffn_conv_w, "ffn_conv_b": ffn_conv_b, "ffn_w_down": ffn_w_down,
        "final_norm_w": final_norm_w,
    }
    layers = [_layer_params(l, p) for l in range(DEPTH)]
    return _trunk(x_prompt, layers), _trunk(x_sample, layers)
```

```python
import functools
import math

import jax
import jax.numpy as jnp
from jax import lax
from jax.experimental import pallas as pl
from jax.experimental.pallas import tpu as pltpu

F32 = jnp.float32
BF16 = jnp.bfloat16

D_MODEL = 1024
DEPTH = 2
ROPE_THETA = 10000.0
NORM_EPS = 1e-6
HEAD_NORM_EPS = 1e-5
CHUNK = 128
N_BRANCH = 4
BRANCH_WIDTH = 512

RET_HEADS, RET_QK_DIM, RET_V_DIM = 4, 64, 128
MLA_HEADS, MLA_NOPE_DIM, MLA_ROPE_DIM, MLA_V_DIM = 8, 64, 32, 64
MLA_Q_RANK, MLA_KV_RANK = 256, 128
DIFF_HEADS, DIFF_HEAD_DIM = 4, 64
DIFF_V_DIM = 2 * DIFF_HEAD_DIM
SSM_HEADS, SSM_HEAD_DIM, SSM_GROUPS, SSM_STATE = 8, 64, 2, 128
SSM_INNER = SSM_HEADS * SSM_HEAD_DIM
SSM_CONV_CH = SSM_INNER + 2 * SSM_GROUPS * SSM_STATE
D_FF = 2816

_O_RET = 0
_O_MLA_CQ = 1536
_O_MLA_CKV = 1792
_O_MLA_KR = 1920
_O_DIFF_QK = 1952
_O_DIFF_V = 2976
_O_SSM_Z = 3488
_O_SSM_XBC = 4000
_O_SSM_DT = 5024
_O_GATE = 5040
_IN_COLS = 9136

LANES = 128
SUBLANES = 8
HALO = SUBLANES
PACK_ROWS = 16
HEAD_PAD = LANES
MLA_VA = MLA_V_DIM + PACK_ROWS
DIFF_VA = DIFF_V_DIM + PACK_ROWS
TOKEN_TILE = 1024
FFN_TOKEN_TILE = 1024
IN_TOKEN_TILE = 1024
CHUNKS_PER_STEP = 8
ATTN_Q_TILE = 512
ATTN_K_TILE = 256
ATTN_LOOKAHEAD = 3
ATTN_TILES_PER_STEP = 4
LOG2E = 1.4426950408889634
NEG_BIG = -1e30
VMEM_LIMIT = 56 * 1024 * 1024
FFN_VMEM_LIMIT = 60 * 1024 * 1024


def _cparams(sems, vmem=VMEM_LIMIT):
    return pltpu.CompilerParams(dimension_semantics=sems, vmem_limit_bytes=vmem)


def _full(shape):
    return pl.BlockSpec(shape, lambda *_: (0,) * len(shape))


def _resident(shape):
    return pl.BlockSpec(shape, lambda *_: (0,) * len(shape), pipeline_mode=pl.Buffered(1))


def _nt(a, b):
    return lax.dot_general(a, b, (((1,), (1,)), ((), ())), preferred_element_type=F32)


def _dot(a, b):
    return jnp.dot(a, b, preferred_element_type=F32)


def _rms(x, w, eps):
    return x * lax.rsqrt(jnp.mean(x * x, axis=-1, keepdims=True) + eps) * w


def _sigmoid(x):
    return 0.5 * jnp.tanh(0.5 * x) + 0.5


def _silu(x):
    return x * _sigmoid(x)


def _rope(x, c, s1, s2, half):
    return x * c + pltpu.roll(x, LANES - half, 1) * s1 + pltpu.roll(x, half, 1) * s2


def _rope_tables(seq, dim, block, off):
    inv_freq = 1.0 / (ROPE_THETA ** (jnp.arange(0, dim, 2, dtype=F32) / dim))
    ang = jnp.arange(seq, dtype=F32)[:, None] * inv_freq[None, :]
    cos, sin = jnp.cos(ang), jnp.sin(ang)
    half = dim // 2
    zero = jnp.zeros_like(sin)
    pre = jnp.zeros((seq, off), F32)
    post = jnp.zeros((seq, block - off - dim), F32)
    c = jnp.concatenate([pre + 1.0, cos, cos, post + 1.0], axis=1)
    s1 = jnp.concatenate([pre, -sin, zero, post], axis=1)
    s2 = jnp.concatenate([pre, zero, sin, post], axis=1)
    reps = LANES // block
    return tuple(jnp.tile(t, (1, reps)) for t in (c, s1, s2))


def _ret_in_kernel(x_ref, nw_ref, w_ref, c_ref, s1_ref, s2_ref, q_ref, k_ref, v_ref, g_ref):
    h = _rms(x_ref[0], nw_ref[...], NORM_EPS).astype(BF16)
    p = _dot(h, w_ref[...])
    c, s1, s2 = c_ref[...], s1_ref[...], s2_ref[...]
    half = RET_QK_DIM // 2
    for i in range(2):
        sl = slice(i * LANES, (i + 1) * LANES)
        q_ref[0, :, sl] = _rope(p[:, sl], c, s1, s2, half).astype(BF16)
        kk = _rope(p[:, 256 + i * LANES:256 + (i + 1) * LANES], c, s1, s2, half)
        k_ref[0, :, sl] = (kk * (RET_QK_DIM ** -0.5)).astype(BF16)
    v_ref[0] = p[:, 512:1024].astype(BF16)
    g_ref[0] = _silu(p[:, 1024:1536]).astype(BF16)


def _ret_in(x, nw, w, tabs):
    b, s, _ = x.shape
    tm = IN_TOKEN_TILE
    tok = lambda width: pl.BlockSpec((1, tm, width), lambda bi, i: (bi, i, 0))
    tab = pl.BlockSpec((tm, LANES), lambda bi, i: (i, 0))
    return pl.pallas_call(
        _ret_in_kernel,
        grid=(b, s // tm),
        in_specs=[tok(D_MODEL), _full((1, D_MODEL)), _full(w.shape), tab, tab, tab],
        out_specs=[tok(256), tok(256), tok(512), tok(512)],
        out_shape=[jax.ShapeDtypeStruct((b, s, n), BF16) for n in (256, 256, 512, 512)],
        compiler_params=_cparams(("parallel", "parallel")),
        name="ret_in",
    )(x, nw, w, *tabs)


def _ones_rows(rows, cols, period, at):
    r = lax.broadcasted_iota(jnp.int32, (rows, cols), 0)
    return jnp.where(r % period == at, 1.0, 0.0).astype(F32)


def _mla_in_kernel(x_ref, nw_ref, w1_ref, qnw_ref, kvnw_ref, wq_ref, wqsw_ref, wk_ref, wvt_ref,
                   c_ref, s1_ref, s2_ref, q_ref, k_ref, vt_ref):
    h = _rms(x_ref[0], nw_ref[...], NORM_EPS).astype(BF16)
    p = _dot(h, w1_ref[...])
    cqn = _rms(p[:, 0:256], qnw_ref[...], NORM_EPS).astype(BF16)
    ckvn = _rms(p[:, 256:384], kvnw_ref[...], NORM_EPS).astype(BF16)
    c, s1, s2 = c_ref[...], s1_ref[...], s2_ref[...]
    half = MLA_ROPE_DIM // 2
    krr = _rope(p[:, 384:512], c, s1, s2, half)
    qp = _dot(cqn, wq_ref[...])
    qsw = _dot(cqn, wqsw_ref[...])
    sin = s2 - s1
    kp = _dot(ckvn, wk_ref[...])
    qscale = ((MLA_NOPE_DIM + MLA_ROPE_DIM) ** -0.5) * LOG2E
    for hd in range(MLA_HEADS):
        sl = slice(hd * HEAD_PAD, (hd + 1) * HEAD_PAD)
        q_ref[0, :, sl] = ((qp[:, sl] * c + qsw[:, sl] * sin) * qscale).astype(BF16)
        k_ref[0, :, sl] = (kp[:, sl] + krr).astype(BF16)
    vt = _nt(wvt_ref[...], ckvn)
    vt_ref[0] = (vt + _ones_rows(vt.shape[0], vt.shape[1], MLA_VA, MLA_V_DIM)).astype(BF16)


def _mla_in(x, nw, w1, qnw, kvnw, wq, wqsw, wk, wvt, tabs):
    b, s, _ = x.shape
    tm = IN_TOKEN_TILE
    tok = lambda width: pl.BlockSpec((1, tm, width), lambda bi, i: (bi, i, 0))
    tab = pl.BlockSpec((tm, LANES), lambda bi, i: (i, 0))
    rows = MLA_HEADS * MLA_VA
    return pl.pallas_call(
        _mla_in_kernel,
        grid=(b, s // tm),
        in_specs=[tok(D_MODEL), _full((1, D_MODEL)), _full(w1.shape), _full(qnw.shape), _full(kvnw.shape),
                  _full(wq.shape), _full(wqsw.shape), _full(wk.shape), _full(wvt.shape), tab, tab, tab],
        out_specs=[tok(1024), tok(1024), pl.BlockSpec((1, rows, tm), lambda bi, i: (bi, 0, i))],
        out_shape=[jax.ShapeDtypeStruct((b, s, 1024), BF16), jax.ShapeDtypeStruct((b, s, 1024), BF16),
                   jax.ShapeDtypeStruct((b, rows, s), BF16)],
        compiler_params=_cparams(("parallel", "parallel")),
        name="mla_in",
    )(x, nw, w1, qnw, kvnw, wq, wqsw, wk, wvt, *tabs)


def _diff_in_kernel(x_ref, nw_ref, wqk_ref, wvt_ref, c_ref, s1_ref, s2_ref, q_ref, k_ref, vt_ref):
    h = _rms(x_ref[0], nw_ref[...], NORM_EPS).astype(BF16)
    p = _dot(h, wqk_ref[...])
    c, s1, s2 = c_ref[...], s1_ref[...], s2_ref[...]
    half = DIFF_HEAD_DIM // 2
    qscale = (DIFF_HEAD_DIM ** -0.5) * LOG2E
    for hd in range(DIFF_HEADS):
        sl = slice(hd * HEAD_PAD, (hd + 1) * HEAD_PAD)
        q_ref[0, :, sl] = (_rope(p[:, sl], c, s1, s2, half) * qscale).astype(BF16)
        k_ref[0, :, sl] = _rope(p[:, 512 + hd * HEAD_PAD:512 + (hd + 1) * HEAD_PAD], c, s1, s2, half).astype(BF16)
    vt = _nt(wvt_ref[...], h)
    vt_ref[0] = (vt + _ones_rows(vt.shape[0], vt.shape[1], DIFF_VA, DIFF_V_DIM)).astype(BF16)


def _diff_in(x, nw, wqk, wvt, tabs):
    b, s, _ = x.shape
    tm = IN_TOKEN_TILE
    tok = lambda width: pl.BlockSpec((1, tm, width), lambda bi, i: (bi, i, 0))
    tab = pl.BlockSpec((tm, LANES), lambda bi, i: (i, 0))
    rows = DIFF_HEADS * DIFF_VA
    return pl.pallas_call(
        _diff_in_kernel,
        grid=(b, s // tm),
        in_specs=[tok(D_MODEL), _full((1, D_MODEL)), _full(wqk.shape), _full(wvt.shape), tab, tab, tab],
        out_specs=[tok(512), tok(512), pl.BlockSpec((1, rows, tm), lambda bi, i: (bi, 0, i))],
        out_shape=[jax.ShapeDtypeStruct((b, s, 512), BF16), jax.ShapeDtypeStruct((b, s, 512), BF16),
                   jax.ShapeDtypeStruct((b, rows, s), BF16)],
        compiler_params=_cparams(("parallel", "parallel")),
        name="diff_in",
    )(x, nw, wqk, wvt, *tabs)


def _halo_specs(tm, s, width):
    per = tm // HALO
    last = s // HALO - 1
    main = pl.BlockSpec((1, tm, width), lambda bi, i: (bi, i, 0))
    prev = pl.BlockSpec((1, HALO, width), lambda bi, i: (bi, jnp.maximum(i * per - 1, 0), 0))
    nxt = pl.BlockSpec((1, HALO, width), lambda bi, i: (bi, jnp.minimum((i + 1) * per, last), 0))
    return prev, main, nxt


def _conv3(g_ref, tm, w_ref, b_ref):
    u = w_ref[0:1, :] * g_ref[pl.ds(HALO - 1, tm), :]
    u = u + w_ref[1:2, :] * g_ref[pl.ds(HALO, tm), :]
    u = u + w_ref[2:3, :] * g_ref[pl.ds(HALO + 1, tm), :]
    return u + b_ref[...]


CONV_ROW_STRIP = 64
CONV_COL_STRIP = 256


def _conv3_strips(g_ref, tm, w_ref, b_ref, emit):
    for r0 in range(0, tm, CONV_ROW_STRIP):
        for c0 in range(0, g_ref.shape[1], CONV_COL_STRIP):
            rows = slice(r0, r0 + CONV_ROW_STRIP)
            cols = slice(c0, c0 + CONV_COL_STRIP)
            u = w_ref[0:1, cols] * g_ref[HALO - 1 + r0:HALO - 1 + r0 + CONV_ROW_STRIP, cols]
            u = u + w_ref[1:2, cols] * g_ref[HALO + r0:HALO + r0 + CONV_ROW_STRIP, cols]
            u = u + w_ref[2:3, cols] * g_ref[HALO + 1 + r0:HALO + 1 + r0 + CONV_ROW_STRIP, cols]
            emit(rows, cols, u + b_ref[:, cols])


def _zero_halo_at_sequence_ends(g_ref, tm):
    i = pl.program_id(1)

    @pl.when(i == 0)
    def _():
        g_ref[0:HALO, :] = jnp.zeros((HALO, g_ref.shape[1]), F32)

    @pl.when(i == pl.num_programs(1) - 1)
    def _():
        g_ref[HALO + tm:2 * HALO + tm, :] = jnp.zeros((HALO, g_ref.shape[1]), F32)


def _softplus(t):
    return jnp.maximum(t, 0.0) + jnp.log1p(jnp.exp(-jnp.abs(t)))


def _ssm_in_kernel(xp_ref, x_ref, xn_ref, nw_ref, wx_ref, wzd_ref, wdtt_ref, cw_ref, cb_ref, dtb_ref, dtbc_ref,
                   xbc_ref, z_ref, dt_ref, dtt_ref, g_ref):
    tm = x_ref.shape[1]
    nw = nw_ref[...]
    hm = _rms(x_ref[0], nw, NORM_EPS)
    hcat = jnp.concatenate([_rms(xp_ref[0], nw, NORM_EPS), hm, _rms(xn_ref[0], nw, NORM_EPS)], axis=0)
    g_ref[...] = _dot(hcat.astype(BF16), wx_ref[...])
    _zero_halo_at_sequence_ends(g_ref, tm)
    def emit(rows, cols, u):
        xbc_ref[0, rows, cols] = _silu(u)

    _conv3_strips(g_ref, tm, cw_ref, cb_ref, emit)
    hb = hm.astype(BF16)
    zd = _dot(hb, wzd_ref[...])
    z_ref[0] = _silu(zd[:, 0:512])
    sp = _softplus(zd[:, 512:640] + dtb_ref[...])
    lane = lax.broadcasted_iota(jnp.int32, sp.shape, 1)
    dt_ref[0] = jnp.where(lane < 2 * SSM_HEADS, sp, 0.0)
    dtt_ref[0] = _softplus(_nt(wdtt_ref[...], hb) + dtbc_ref[...])


def _ssm_in(x, nw, wx, wzd, wdtt, cw, cb, dtb, dtbc):
    b, s, _ = x.shape
    tm = IN_TOKEN_TILE
    tok = lambda width: pl.BlockSpec((1, tm, width), lambda bi, i: (bi, i, 0))
    return pl.pallas_call(
        _ssm_in_kernel,
        grid=(b, s // tm),
        in_specs=[*_halo_specs(tm, s, D_MODEL), _full((1, D_MODEL)), _full(wx.shape), _full(wzd.shape),
                  _full(wdtt.shape), _full(cw.shape), _full(cb.shape), _full(dtb.shape), _full(dtbc.shape)],
        out_specs=[tok(1024), tok(512), tok(LANES), pl.BlockSpec((1, 2 * SSM_HEADS, tm), lambda bi, i: (bi, 0, i))],
        out_shape=[jax.ShapeDtypeStruct((b, s, 1024), F32), jax.ShapeDtypeStruct((b, s, 512), F32),
                   jax.ShapeDtypeStruct((b, s, LANES), F32), jax.ShapeDtypeStruct((b, 2 * SSM_HEADS, s), F32)],
        scratch_shapes=[pltpu.VMEM((tm + 2 * HALO, SSM_CONV_CH), F32)],
        compiler_params=_cparams(("parallel", "parallel")),
        name="ssm_in",
    )(x, x, x, nw, wx, wzd, wdtt, cw, cb, dtb, dtbc)


def _ret_state_kernel(kf_ref, vf_ref, kb_ref, vb_ref, kdf_ref, kdb_ref, cdf_ref, cdb_ref,
                      sf_ref, sb_ref, st_ref):
    @pl.when(pl.program_id(1) == 0)
    def _():
        st_ref[...] = jnp.zeros(st_ref.shape, F32)

    def scan(d, k_ref, v_ref, kd_ref, cd_ref, out_ref):
        incs = []
        for g in range(CHUNKS_PER_STEP):
            rows = slice(g * CHUNK, (g + 1) * CHUNK)
            kdt = (k_ref[0, rows, :].astype(F32) * kd_ref[...]).T.astype(BF16)
            v = v_ref[0, rows, :]
            incs.append(jnp.concatenate(
                [_dot(kdt[hd * RET_QK_DIM:(hd + 1) * RET_QK_DIM], v[:, hd * RET_V_DIM:(hd + 1) * RET_V_DIM])
                 for hd in range(RET_HEADS)], axis=0))
        st = st_ref[d]
        order = range(CHUNKS_PER_STEP) if d == 0 else range(CHUNKS_PER_STEP - 1, -1, -1)
        for g in order:
            out_ref[0, g] = st
            st = cd_ref[...] * st + incs[g]
        st_ref[d] = st

    scan(0, kf_ref, vf_ref, kdf_ref, cdf_ref, sf_ref)
    scan(1, kb_ref, vb_ref, kdb_ref, cdb_ref, sb_ref)


def _ret_state(k, v, kdf, kdb, cdf, cdb):
    b, s, _ = k.shape
    g = CHUNKS_PER_STEP
    n = s // (g * CHUNK)
    fwd = lambda width: pl.BlockSpec((1, g * CHUNK, width), lambda bi, c: (bi, c, 0))
    bwd = lambda width: pl.BlockSpec((1, g * CHUNK, width), lambda bi, c: (bi, n - 1 - c, 0))
    rows = RET_HEADS * RET_QK_DIM
    st_shape = jax.ShapeDtypeStruct((b, n * g, rows, RET_V_DIM), F32)
    return pl.pallas_call(
        _ret_state_kernel,
        grid=(b, n),
        in_specs=[fwd(256), fwd(512), bwd(256), bwd(512), _full(kdf.shape), _full(kdb.shape),
                  _full(cdf.shape), _full(cdb.shape)],
        out_specs=[pl.BlockSpec((1, g, rows, RET_V_DIM), lambda bi, c: (bi, c, 0, 0)),
                   pl.BlockSpec((1, g, rows, RET_V_DIM), lambda bi, c: (bi, n - 1 - c, 0, 0))],
        out_shape=[st_shape, st_shape],
        scratch_shapes=[pltpu.VMEM((2, rows, RET_V_DIM), F32)],
        compiler_params=_cparams(("parallel", "arbitrary")),
        name="ret_state",
    )(k, v, k, v, kdf, kdb, cdf, cdb)


def _ret_out_kernel(q_ref, k_ref, v_ref, g_ref, sf_ref, sb_ref, d_ref, qdf_ref, qdb_ref, nw_ref, o_ref):
    gs = range(CHUNKS_PER_STEP)
    rows = [slice(g * CHUNK, (g + 1) * CHUNK) for g in gs]
    lane = lax.broadcasted_iota(jnp.int32, (CHUNK, LANES), 1)
    q = [q_ref[0, rows[g], :].astype(F32) for g in gs]
    qf = [q[g] * qdf_ref[...] for g in gs]
    qb = [q[g] * qdb_ref[...] for g in gs]
    for hd in range(RET_HEADS):
        pair = slice((hd // 2) * LANES, (hd // 2 + 1) * LANES)
        own = (lane < RET_QK_DIM) if hd % 2 == 0 else (lane >= RET_QK_DIM)
        pick = lambda t: jnp.where(own, t[:, pair], 0.0).astype(BF16)
        vsl = slice(hd * RET_V_DIM, (hd + 1) * RET_V_DIM)
        sc = [_nt(pick(q[g]), k_ref[0, rows[g], pair]) * d_ref[hd] for g in gs]
        cross = [_dot(pick(qf[g]), sf_ref[0, g, pair, :].astype(BF16))
                 + _dot(pick(qb[g]), sb_ref[0, g, pair, :].astype(BF16)) for g in gs]
        y = [_dot(sc[g].astype(BF16), v_ref[0, rows[g], vsl]) + cross[g] for g in gs]
        for g in gs:
            yn = _rms(y[g], nw_ref[:, vsl], HEAD_NORM_EPS)
            o_ref[0, rows[g], vsl] = (g_ref[0, rows[g], vsl].astype(F32) * yn).astype(BF16)


def _ret_out(q, k, v, g, sf, sb, dmat, qdf, qdb, nw):
    b, s, _ = q.shape
    gc = CHUNKS_PER_STEP
    n = s // (gc * CHUNK)
    tok = lambda width: pl.BlockSpec((1, gc * CHUNK, width), lambda bi, c: (bi, c, 0))
    st = pl.BlockSpec((1, gc) + sf.shape[2:], lambda bi, c: (bi, c, 0, 0))
    return pl.pallas_call(
        _ret_out_kernel,
        grid=(b, n),
        in_specs=[tok(256), tok(256), tok(512), tok(512), st, st, _full(dmat.shape), _full(qdf.shape),
                  _full(qdb.shape), _full(nw.shape)],
        out_specs=tok(512),
        out_shape=jax.ShapeDtypeStruct((b, s, 512), BF16),
        compiler_params=_cparams(("parallel", "parallel")),
        name="ret_out",
    )(q, k, v, g, sf, sb, dmat, qdf, qdb, nw)


def _ret_tables(log_decay):
    lg_f = log_decay[0].astype(F32)
    lg_b = log_decay[1].astype(F32)
    pos = jnp.arange(CHUNK, dtype=F32)
    rel = pos[:, None] - pos[None, :]
    low = jnp.where(rel >= 0, jnp.exp(jnp.maximum(rel, 0.0)[None] * lg_f[:, None, None]), 0.0)
    upp = jnp.where(rel <= 0, jnp.exp(jnp.maximum(-rel, 0.0)[None] * lg_b[:, None, None]), 0.0)
    dmat = low + upp
    wide = lambda t: jnp.repeat(t, RET_QK_DIM, axis=1)
    qdf = wide(jnp.exp((pos + 1.0)[:, None] * lg_f))
    qdb = wide(jnp.exp((CHUNK - pos)[:, None] * lg_b))
    kdf = wide(jnp.exp((CHUNK - 1.0 - pos)[:, None] * lg_f))
    kdb = wide(jnp.exp(pos[:, None] * lg_b))
    tall = lambda lg: jnp.broadcast_to(jnp.repeat(jnp.exp(CHUNK * lg), RET_QK_DIM)[:, None],
                                       (RET_HEADS * RET_QK_DIM, RET_V_DIM))
    return dmat, qdf, qdb, kdf, kdb, tall(lg_f), tall(lg_b)


def _split3(x):
    hi = x.astype(BF16)
    r1 = x - hi.astype(F32)
    mid = r1.astype(BF16)
    lo = (r1 - mid.astype(F32)).astype(BF16)
    return hi, mid, lo


def _tri(lower):
    ii = lax.broadcasted_iota(jnp.int32, (CHUNK, CHUNK), 0)
    jj = lax.broadcasted_iota(jnp.int32, (CHUNK, CHUNK), 1)
    return jnp.where((ii >= jj) if lower else (jj >= ii), 1.0, 0.0).astype(BF16)


def _cumsum_time_on_lanes(xt, tri):
    hi, mid, lo = _split3(xt)
    return _nt(hi, tri) + _nt(mid, tri) + _nt(lo, tri)


def _cumsum_time_on_rows(x, tri):
    hi, mid, lo = _split3(x)
    return _dot(tri, hi) + _dot(tri, mid) + _dot(tri, lo)


def _ssd_state_kernel(xf_ref, dtf_ref, xb_ref, dtb_ref, alog_ref, sf_ref, sb_ref, st_ref):
    @pl.when(pl.program_id(1) == 0)
    def _():
        st_ref[...] = jnp.zeros(st_ref.shape, F32)

    head_of_lane = lax.broadcasted_iota(jnp.int32, (1, SSM_INNER), 1) // SSM_HEAD_DIM

    per_group = SSM_INNER // SSM_GROUPS

    def scan(d, x_ref, dt_ref, out_ref):
        gs = range(CHUNKS_PER_STEP)
        rows = [slice(g * CHUNK, (g + 1) * CHUNK) for g in gs]
        r0 = d * SSM_HEADS
        a = -jnp.exp(alog_ref[r0:r0 + SSM_HEADS, :])
        tri = _tri(lower=(d == 0))
        dtt = [dt_ref[0, r0:r0 + SSM_HEADS, rows[g]] for g in gs]
        cum = [_cumsum_time_on_lanes(dtt[g] * a, tri) for g in gs]
        endb = [jnp.broadcast_to(cum[g][:, 0:1] if d == 1 else cum[g][:, CHUNK - 1:CHUNK], cum[g].shape)
                for g in gs]
        wt = [jnp.exp(endb[g] - cum[g]) * dtt[g] for g in gs]
        edec = [jnp.exp(endb[g]) for g in gs]
        xt = [x_ref[0, rows[g], 0:SSM_INNER].T for g in gs]
        bt = [[x_ref[0, rows[g], SSM_INNER + k * SSM_STATE:SSM_INNER + (k + 1) * SSM_STATE].T.astype(BF16)
               for k in range(SSM_GROUPS)] for g in gs]
        xwt = [jnp.concatenate([xt[g][hd * SSM_HEAD_DIM:(hd + 1) * SSM_HEAD_DIM] * wt[g][hd:hd + 1]
                                for hd in range(SSM_HEADS)], axis=0).astype(BF16) for g in gs]
        inc = [jnp.concatenate([_nt(bt[g][k], xwt[g][k * per_group:(k + 1) * per_group])
                                for k in range(SSM_GROUPS)], axis=1) for g in gs]
        dec = []
        for g in gs:
            row_dec = jnp.zeros((1, SSM_INNER), F32)
            for hd in range(SSM_HEADS):
                row = jnp.concatenate([edec[g][hd:hd + 1]] * (SSM_INNER // LANES), axis=1)
                row_dec = jnp.where(head_of_lane == hd, row, row_dec)
            dec.append(row_dec)
        st = st_ref[d]
        for g in (gs if d == 0 else reversed(gs)):
            out_ref[0, g] = st
            st = st * dec[g] + inc[g]
        st_ref[d] = st

    scan(0, xf_ref, dtf_ref, sf_ref)
    scan(1, xb_ref, dtb_ref, sb_ref)


def _ssd_state(xbc, dtt, alog):
    b, s, _ = xbc.shape
    g = CHUNKS_PER_STEP
    n = s // (g * CHUNK)
    fwd = lambda width: pl.BlockSpec((1, g * CHUNK, width), lambda bi, c: (bi, c, 0))
    bwd = lambda width: pl.BlockSpec((1, g * CHUNK, width), lambda bi, c: (bi, n - 1 - c, 0))
    dt_fwd = pl.BlockSpec((1, 2 * SSM_HEADS, g * CHUNK), lambda bi, c: (bi, 0, c))
    dt_bwd = pl.BlockSpec((1, 2 * SSM_HEADS, g * CHUNK), lambda bi, c: (bi, 0, n - 1 - c))
    st_shape = jax.ShapeDtypeStruct((b, n * g, SSM_STATE, SSM_INNER), F32)
    return pl.pallas_call(
        _ssd_state_kernel,
        grid=(b, n),
        in_specs=[fwd(768), dt_fwd, bwd(768), dt_bwd, _full(alog.shape)],
        out_specs=[pl.BlockSpec((1, g, SSM_STATE, SSM_INNER), lambda bi, c: (bi, c, 0, 0)),
                   pl.BlockSpec((1, g, SSM_STATE, SSM_INNER), lambda bi, c: (bi, n - 1 - c, 0, 0))],
        out_shape=[st_shape, st_shape],
        scratch_shapes=[pltpu.VMEM((2, SSM_STATE, SSM_INNER), F32)],
        compiler_params=_cparams(("parallel", "arbitrary")),
        name="ssd_state",
    )(xbc, dtt, xbc, dtt, alog)


def _ssd_out_kernel(x_ref, dt_ref, dtt_ref, z_ref, sf_ref, sb_ref, alog_ref, arow_ref, dskip_ref, nw_ref, o_ref):
    gs = range(CHUNKS_PER_STEP)
    rows = [slice(g * CHUNK, (g + 1) * CHUNK) for g in gs]
    nh = SSM_HEADS
    low_t, upp_t = _tri(True), _tri(False)
    a_t = -jnp.exp(alog_ref[...])
    a_row = -jnp.exp(arow_ref[...])
    dtt = [dtt_ref[0, :, rows[g]] for g in gs]
    cumt = [jnp.concatenate([_cumsum_time_on_lanes(dtt[g][0:nh] * a_t[0:nh], low_t),
                             _cumsum_time_on_lanes(dtt[g][nh:2 * nh] * a_t[nh:2 * nh], upp_t)], axis=0) for g in gs]
    da = [dt_ref[0, rows[g], :] * a_row for g in gs]
    head_lane = lax.broadcasted_iota(jnp.int32, (CHUNK, LANES), 1)
    cum = [jnp.where(head_lane < nh, _cumsum_time_on_rows(da[g], low_t), _cumsum_time_on_rows(da[g], upp_t))
           for g in gs]
    ii = lax.broadcasted_iota(jnp.int32, (CHUNK, CHUNK), 0)
    jj = lax.broadcasted_iota(jnp.int32, (CHUNK, CHUNK), 1)
    low = ii >= jj
    upp = jj >= ii
    first = jj < SSM_HEAD_DIM
    heads_per_group = nh // SSM_GROUPS
    c0 = SSM_INNER + SSM_GROUPS * SSM_STATE
    ys = [[] for _ in gs]
    for pr in range(nh // 2):
        k = (2 * pr) // heads_per_group
        pair = slice(pr * LANES, (pr + 1) * LANES)
        cg = [x_ref[0, rows[g], c0 + k * SSM_STATE:c0 + (k + 1) * SSM_STATE].astype(BF16) for g in gs]
        bg = [x_ref[0, rows[g], SSM_INNER + k * SSM_STATE:SSM_INNER + (k + 1) * SSM_STATE].astype(BF16) for g in gs]
        cb = [_nt(cg[g], bg[g]) for g in gs]
        off = [_dot(cg[g], sf_ref[0, g, :, pair].astype(BF16)) for g in gs]
        offb = [_dot(cg[g], sb_ref[0, g, :, pair].astype(BF16)) for g in gs]
        diag = [[], []]
        ecol = [[], []]
        for e, hd in enumerate((2 * pr, 2 * pr + 1)):
            colf = [jnp.broadcast_to(cum[g][:, hd:hd + 1], (CHUNK, CHUNK)) for g in gs]
            colb = [jnp.broadcast_to(cum[g][:, nh + hd:nh + hd + 1], (CHUNK, CHUNK)) for g in gs]
            segf = [jnp.exp(jnp.where(low, colf[g] - cumt[g][hd:hd + 1], NEG_BIG)) for g in gs]
            segb = [jnp.exp(jnp.where(upp, colb[g] - cumt[g][nh + hd:nh + hd + 1], NEG_BIG)) for g in gs]
            w = [cb[g] * (segf[g] * dtt[g][hd:hd + 1] + segb[g] * dtt[g][nh + hd:nh + hd + 1]) for g in gs]
            diag[e] = [_dot(w[g].astype(BF16), x_ref[0, rows[g], pair].astype(BF16)) for g in gs]
            ecol[e] = [(jnp.exp(colf[g]), jnp.exp(colb[g])) for g in gs]
        for g in gs:
            yd = jnp.where(first, diag[0][g], diag[1][g])
            ef = jnp.where(first, ecol[0][g][0], ecol[1][g][0])
            eb = jnp.where(first, ecol[0][g][1], ecol[1][g][1])
            ys[g].append(yd + ef * off[g] + eb * offb[g])
    gw = SSM_INNER // SSM_GROUPS
    for g in gs:
        y = jnp.concatenate(ys[g], axis=1) + dskip_ref[...] * x_ref[0, rows[g], 0:SSM_INNER]
        y = y * z_ref[0, rows[g], :]
        for k in range(SSM_GROUPS):
            sl = slice(k * gw, (k + 1) * gw)
            o_ref[0, rows[g], sl] = _rms(y[:, sl], nw_ref[:, sl], HEAD_NORM_EPS).astype(BF16)


def _ssd_out(xbc, dt, dtt, z, sf, sb, alog, arow, dskip, nw):
    b, s, _ = xbc.shape
    gc = CHUNKS_PER_STEP
    n = s // (gc * CHUNK)
    tok = lambda width: pl.BlockSpec((1, gc * CHUNK, width), lambda bi, c: (bi, c, 0))
    st = pl.BlockSpec((1, gc, SSM_STATE, SSM_INNER), lambda bi, c: (bi, c, 0, 0))
    return pl.pallas_call(
        _ssd_out_kernel,
        grid=(b, n),
        in_specs=[tok(1024), tok(LANES), pl.BlockSpec((1, 2 * SSM_HEADS, gc * CHUNK), lambda bi, c: (bi, 0, c)),
                  tok(512), st, st, _full(alog.shape), _full(arow.shape), _full(dskip.shape), _full(nw.shape)],
        out_specs=tok(512),
        out_shape=jax.ShapeDtypeStruct((b, s, 512), BF16),
        compiler_params=_cparams(("parallel", "parallel")),
        name="ssd_out",
    )(xbc, dt, dtt, z, sf, sb, alog, arow, dskip, nw)


def _attn_scores_t(q, k_ref, vt_ref, seq, column_halves):
    tk = min(ATTN_K_TILE, seq)
    rows = vt_ref.shape[1]

    m = jnp.full((1, ATTN_Q_TILE), NEG_BIG, F32)
    acc = jnp.zeros((rows, ATTN_Q_TILE), F32)
    n = seq // tk
    scores = lambda c: _nt(k_ref[0, c * tk:(c + 1) * tk, :], q)
    pending = [scores(c) for c in range(min(ATTN_LOOKAHEAD, n))]
    for c in range(n):
        s = pending.pop(0)
        if c + ATTN_LOOKAHEAD < n:
            pending.append(scores(c + ATTN_LOOKAHEAD))
        m_new = jnp.maximum(m, jnp.max(s, axis=0, keepdims=True))
        v = vt_ref[0, :, c * tk:(c + 1) * tk]
        hw = ATTN_Q_TILE // column_halves
        pv = [_dot(v, jnp.exp2(s[:, i * hw:(i + 1) * hw] - m_new[:, i * hw:(i + 1) * hw]).astype(BF16))
              for i in range(column_halves)]
        acc = acc * jnp.exp2(m - m_new) + (pv[0] if column_halves == 1 else jnp.concatenate(pv, axis=1))
        m = m_new
    return acc


def _mla_attn_kernel(q_ref, k_ref, vt_ref, o_ref):
    tq = ATTN_Q_TILE
    for t in range(ATTN_TILES_PER_STEP):
        cols = slice(t * tq, (t + 1) * tq)
        acc = _attn_scores_t(q_ref[0, cols, :], k_ref, vt_ref, k_ref.shape[1], column_halves=2)
        o_ref[0, :, cols] = (acc[0:MLA_V_DIM] / acc[MLA_V_DIM:MLA_V_DIM + 1]).astype(BF16)


def _mla_attn(q, k, vt):
    b, s, _ = q.shape
    tq = ATTN_Q_TILE * ATTN_TILES_PER_STEP
    return pl.pallas_call(
        _mla_attn_kernel,
        grid=(b, MLA_HEADS, s // tq),
        in_specs=[pl.BlockSpec((1, tq, HEAD_PAD), lambda bi, h, i: (bi, i, h)),
                  pl.BlockSpec((1, s, HEAD_PAD), lambda bi, h, i: (bi, 0, h)),
                  pl.BlockSpec((1, MLA_VA, s), lambda bi, h, i: (bi, h, 0))],
        out_specs=pl.BlockSpec((1, MLA_V_DIM, tq), lambda bi, h, i: (bi, h, i)),
        out_shape=jax.ShapeDtypeStruct((b, MLA_HEADS * MLA_V_DIM, s), BF16),
        compiler_params=_cparams(("parallel", "parallel", "parallel")),
        name="mla_attn",
    )(q, k, vt)


def _diff_attn_kernel(q_ref, k_ref, vt_ref, lam_ref, nw_ref, o_ref, *, lam_init):
    tq = ATTN_Q_TILE // 2
    dv = DIFF_V_DIM
    lv = lam_ref[...]
    lam = (jnp.exp(jnp.sum(lv[0:1] * lv[1:2], axis=1, keepdims=True))
           - jnp.exp(jnp.sum(lv[2:3] * lv[3:4], axis=1, keepdims=True)) + lam_init)
    for t in range(ATTN_TILES_PER_STEP):
        cols = slice(t * tq, (t + 1) * tq)
        q = q_ref[0, cols, :].astype(F32)
        lane = lax.broadcasted_iota(jnp.int32, q.shape, 1)
        first = lane < DIFF_HEAD_DIM
        both = jnp.concatenate([jnp.where(first, q, 0.0), jnp.where(first, 0.0, q)], axis=0).astype(BF16)
        acc = _attn_scores_t(both, k_ref, vt_ref, k_ref.shape[1], column_halves=1)
        o1 = acc[0:dv, 0:tq] / acc[dv:dv + 1, 0:tq]
        o2 = acc[0:dv, tq:2 * tq] / acc[dv:dv + 1, tq:2 * tq]
        o = o1 - lam * o2
        o = o * lax.rsqrt(jnp.mean(o * o, axis=0, keepdims=True) + HEAD_NORM_EPS) * nw_ref[...]
        o_ref[0, :, cols] = (o * (1.0 - lam_init)).astype(BF16)


def _diff_attn(q, k, vt, lam_vec, nw, lam_init):
    b, s, _ = q.shape
    tq = (ATTN_Q_TILE // 2) * ATTN_TILES_PER_STEP
    return pl.pallas_call(
        functools.partial(_diff_attn_kernel, lam_init=lam_init),
        grid=(b, DIFF_HEADS, s // tq),
        in_specs=[pl.BlockSpec((1, tq, HEAD_PAD), lambda bi, h, i: (bi, i, h)),
                  pl.BlockSpec((1, s, HEAD_PAD), lambda bi, h, i: (bi, 0, h)),
                  pl.BlockSpec((1, DIFF_VA, s), lambda bi, h, i: (bi, h, 0)),
                  _full(lam_vec.shape), _full(nw.shape)],
        out_specs=pl.BlockSpec((1, DIFF_V_DIM, tq), lambda bi, h, i: (bi, h, i)),
        out_shape=jax.ShapeDtypeStruct((b, DIFF_HEADS * DIFF_V_DIM, s), BF16),
        compiler_params=_cparams(("parallel", "parallel", "parallel")),
        name="diff_attn",
    )(q, k, vt, lam_vec, nw)


def _merge_kernel(x_ref, nw_ref, wg_ref, ret_ref, mlat_ref, difft_ref, ssm_ref, wb_ref, wo_ref, o_ref):
    x = x_ref[0]
    h = _rms(x, nw_ref[...], NORM_EPS).astype(BF16)
    rows = lambda t_ref: t_ref[0].astype(F32).T.astype(BF16)
    outs = (ret_ref[0], rows(mlat_ref), rows(difft_ref), ssm_ref[0])
    merged = None
    for i in range(N_BRANCH):
        gate = _sigmoid(_dot(h, wg_ref[:, i * D_MODEL:(i + 1) * D_MODEL]))
        term = gate * _dot(outs[i], wb_ref[i])
        merged = term if merged is None else merged + term
    o_ref[0] = x + _dot(merged.astype(BF16), wo_ref[...])


def _merge(x, nw, wg, ret, mlat, difft, ssm, wb, wo):
    b, s, _ = x.shape
    tm = TOKEN_TILE
    tok = lambda width: pl.BlockSpec((1, tm, width), lambda bi, i: (bi, i, 0))
    tr = pl.BlockSpec((1, BRANCH_WIDTH, tm), lambda bi, i: (bi, 0, i))
    return pl.pallas_call(
        _merge_kernel,
        grid=(b, s // tm),
        in_specs=[tok(D_MODEL), _full((1, D_MODEL)), _resident(wg.shape), tok(512), tr, tr, tok(512),
                  _resident(wb.shape), _resident(wo.shape)],
        out_specs=tok(D_MODEL),
        out_shape=jax.ShapeDtypeStruct((b, s, D_MODEL), F32),
        compiler_params=_cparams(("parallel", "parallel")),
        name="merge",
    )(x, nw, wg, ret, mlat, difft, ssm, wb, wo)


FFN_COL_TILE = 1408


def _ffn_kernel(xp_ref, x_ref, xn_ref, nw_ref, wgate_ref, wup_ref, cw_ref, cb_ref, wdown_ref, fnw_ref,
                o_ref, g_ref, *, final_norm):
    tm = x_ref.shape[1]
    nw = nw_ref[...]
    x = x_ref[0]
    hm = _rms(x, nw, NORM_EPS)
    hcat = jnp.concatenate([_rms(xp_ref[0], nw, NORM_EPS), hm, _rms(xn_ref[0], nw, NORM_EPS)],
                           axis=0).astype(BF16)
    hb = hm.astype(BF16)
    acc = x
    for j in range(D_FF // FFN_COL_TILE):
        cols = slice(j * FFN_COL_TILE, (j + 1) * FFN_COL_TILE)
        g_ref[...] = _dot(hcat, wgate_ref[:, cols])
        _zero_halo_at_sequence_ends(g_ref, tm)
        u = _conv3(g_ref, tm, cw_ref.at[:, cols], cb_ref.at[:, cols])
        act = (_silu(u) * _dot(hb, wup_ref[:, cols])).astype(BF16)
        acc = acc + _dot(act, wdown_ref[cols, :])
    if final_norm:
        acc = _rms(acc, fnw_ref[...], NORM_EPS)
    o_ref[0] = acc


def _ffn(x, nw, wgate, wup, cw, cb, wdown, fnw, final_norm):
    b, s, _ = x.shape
    tm = FFN_TOKEN_TILE
    return pl.pallas_call(
        functools.partial(_ffn_kernel, final_norm=final_norm),
        grid=(b, s // tm),
        in_specs=[*_halo_specs(tm, s, D_MODEL), _full((1, D_MODEL)), _resident(wgate.shape), _resident(wup.shape),
                  _full(cw.shape), _full(cb.shape), _resident(wdown.shape), _full((1, D_MODEL))],
        out_specs=pl.BlockSpec((1, tm, D_MODEL), lambda bi, i: (bi, i, 0)),
        out_shape=jax.ShapeDtypeStruct((b, s, D_MODEL), F32),
        scratch_shapes=[pltpu.VMEM((tm + 2 * HALO, FFN_COL_TILE), F32)],
        compiler_params=_cparams(("parallel", "parallel"), FFN_VMEM_LIMIT),
        name="ffn",
    )(x, x, x, nw, wgate, wup, cw, cb, wdown, fnw)


def _pad_axis(t, axis, size):
    pad = [(0, 0)] * t.ndim
    pad[axis] = (0, size - t.shape[axis])
    return jnp.pad(t, pad)


def _layer_params(l, p):
    w_in = p["w_in"][l]
    row = lambda t: t.reshape(1, -1).astype(F32)
    d = {}
    d["norm_mix"] = row(p["norm_mix_w"][l])
    d["w_ret"] = w_in[:, _O_RET:_O_MLA_CQ].astype(BF16)
    kr = jnp.pad(w_in[:, _O_MLA_KR:_O_DIFF_QK], ((0, 0), (MLA_NOPE_DIM, HEAD_PAD - MLA_NOPE_DIM - MLA_ROPE_DIM)))
    d["w_mla1"] = jnp.concatenate([w_in[:, _O_MLA_CQ:_O_MLA_KR], kr], axis=1).astype(BF16)
    d["mla_qnw"] = row(p["mla_q_norm_w"][l])
    d["mla_kvnw"] = row(p["mla_kv_norm_w"][l])
    uq = p["mla_w_uq"][l].reshape(MLA_Q_RANK, MLA_HEADS, MLA_NOPE_DIM + MLA_ROPE_DIM)
    d["w_mla_q"] = _pad_axis(uq, 2, HEAD_PAD).reshape(MLA_Q_RANK, MLA_HEADS * HEAD_PAD).astype(BF16)
    r1, r2 = jnp.split(uq[:, :, MLA_NOPE_DIM:], 2, axis=2)
    swapped = jnp.concatenate([jnp.zeros_like(uq[:, :, :MLA_NOPE_DIM]), -r2, r1], axis=2)
    d["w_mla_q_swap"] = _pad_axis(swapped, 2, HEAD_PAD).reshape(MLA_Q_RANK, MLA_HEADS * HEAD_PAD).astype(BF16)
    ukv = p["mla_w_ukv"][l].reshape(MLA_KV_RANK, MLA_HEADS, MLA_NOPE_DIM + MLA_V_DIM)
    d["w_mla_k"] = _pad_axis(ukv[:, :, :MLA_NOPE_DIM], 2, HEAD_PAD).reshape(
        MLA_KV_RANK, MLA_HEADS * HEAD_PAD).astype(BF16)
    vt = jnp.transpose(ukv[:, :, MLA_NOPE_DIM:], (1, 2, 0))
    d["w_mla_vt"] = _pad_axis(vt, 1, MLA_VA).reshape(MLA_HEADS * MLA_VA, MLA_KV_RANK).astype(BF16)
    d["w_diff_qk"] = w_in[:, _O_DIFF_QK:_O_DIFF_V].astype(BF16)
    dvt = w_in[:, _O_DIFF_V:_O_SSM_Z].T.reshape(DIFF_HEADS, DIFF_V_DIM, D_MODEL)
    d["w_diff_vt"] = _pad_axis(dvt, 1, DIFF_VA).reshape(DIFF_HEADS * DIFF_VA, D_MODEL).astype(BF16)
    d["diff_lambda"] = p["diff_lambda"][l].astype(F32)
    d["diff_nw"] = jnp.broadcast_to(p["diff_norm_w"][l].astype(F32)[:, None], (DIFF_V_DIM, ATTN_Q_TILE // 2))
    d["w_ssm_x"] = w_in[:, _O_SSM_XBC:_O_SSM_DT].astype(BF16)
    zdt = jnp.concatenate([w_in[:, _O_SSM_Z:_O_SSM_XBC], w_in[:, _O_SSM_DT:_O_GATE]], axis=1)
    d["w_ssm_zdt"] = _pad_axis(zdt, 1, SSM_INNER + LANES).astype(BF16)
    d["ssm_conv_w"] = p["ssm_conv_w"][l].astype(F32)
    d["ssm_conv_b"] = row(p["ssm_conv_b"][l])
    d["ssm_dt_bias"] = _pad_axis(row(p["ssm_dt_bias"][l]), 1, LANES)
    d["ssm_dt_bias_col"] = p["ssm_dt_bias"][l].astype(F32).reshape(2 * SSM_HEADS, 1)
    d["w_ssm_dtt"] = w_in[:, _O_SSM_DT:_O_GATE].T.astype(BF16)
    d["ssm_alog_row"] = _pad_axis(row(p["ssm_a_log"][l]), 1, LANES)
    d["ssm_alog"] = jnp.broadcast_to(p["ssm_a_log"][l].astype(F32).reshape(2 * SSM_HEADS, 1), (2 * SSM_HEADS, CHUNK))
    d["ssm_dskip"] = jnp.repeat(p["ssm_d"][l].astype(F32), SSM_HEAD_DIM).reshape(1, SSM_INNER)
    d["ssm_nw"] = row(p["ssm_norm_w"][l])
    d["ret_nw"] = row(p["ret_norm_w"][l])
    d["ret_log_decay"] = p["ret_log_decay"][l]
    d["w_gate"] = w_in[:, _O_GATE:_IN_COLS].astype(BF16)
    d["w_branch"] = p["w_branch"][l].astype(BF16)
    d["w_out"] = p["w_out"][l].astype(BF16)
    d["norm_ffn"] = row(p["norm_ffn_w"][l])
    d["ffn_w_gate"] = p["ffn_w_gate"][l].astype(BF16)
    d["ffn_w_up"] = p["ffn_w_up"][l].astype(BF16)
    d["ffn_conv_w"] = p["ffn_conv_w"][l].astype(F32)
    d["ffn_conv_b"] = row(p["ffn_conv_b"][l])
    d["ffn_w_down"] = p["ffn_w_down"][l].astype(BF16)
    d["final_nw"] = row(p["final_norm_w"])
    return d


def _encoder_layer(x, l, d, ropes):
    rope64, rope_mla = ropes
    rq, rk, rv, rg = _ret_in(x, d["norm_mix"], d["w_ret"], rope64)
    dmat, qdf, qdb, kdf, kdb, cdf, cdb = _ret_tables(d["ret_log_decay"])
    rsf, rsb = _ret_state(rk, rv, kdf, kdb, cdf, cdb)
    ret = _ret_out(rq, rk, rv, rg, rsf, rsb, dmat, qdf, qdb, d["ret_nw"])
    mq, mk, mvt = _mla_in(x, d["norm_mix"], d["w_mla1"], d["mla_qnw"], d["mla_kvnw"], d["w_mla_q"],
                          d["w_mla_q_swap"], d["w_mla_k"], d["w_mla_vt"], rope_mla)
    mlat = _mla_attn(mq, mk, mvt)
    dq, dk, dvt = _diff_in(x, d["norm_mix"], d["w_diff_qk"], d["w_diff_vt"], rope64)
    lam_init = 0.8 - 0.6 * math.exp(-0.3 * l)
    difft = _diff_attn(dq, dk, dvt, d["diff_lambda"], d["diff_nw"], lam_init)
    xbc, sz, dt, dtt = _ssm_in(x, d["norm_mix"], d["w_ssm_x"], d["w_ssm_zdt"], d["w_ssm_dtt"], d["ssm_conv_w"],
                               d["ssm_conv_b"], d["ssm_dt_bias"], d["ssm_dt_bias_col"])
    ssf, ssb = _ssd_state(xbc, dtt, d["ssm_alog"])
    ssm = _ssd_out(xbc, dt, dtt, sz, ssf, ssb, d["ssm_alog"], d["ssm_alog_row"], d["ssm_dskip"], d["ssm_nw"])
    x = _merge(x, d["norm_mix"], d["w_gate"], ret, mlat, difft, ssm, d["w_branch"], d["w_out"])
    return _ffn(x, d["norm_ffn"], d["ffn_w_gate"], d["ffn_w_up"], d["ffn_conv_w"], d["ffn_conv_b"],
                d["ffn_w_down"], d["final_nw"], final_norm=(l == DEPTH - 1))


def _trunk(x, layers):
    s = x.shape[1]
    ropes = (_rope_tables(s, RET_QK_DIM, RET_QK_DIM, 0),
             _rope_tables(s, MLA_ROPE_DIM, HEAD_PAD, MLA_NOPE_DIM))
    for l in range(DEPTH):
        x = _encoder_layer(x, l, layers[l], ropes)
    return x


def kernel(x_prompt, x_sample, norm_mix_w, w_in, ret_log_decay, ret_norm_w, mla_q_norm_w, mla_w_uq, mla_kv_norm_w, mla_w_ukv, diff_lambda, diff_norm_w, ssm_conv_w, ssm_conv_b, ssm_dt_bias, ssm_a_log, ssm_d, ssm_norm_w, w_branch, w_out, norm_ffn_w, ffn_w_gate, ffn_w_up, ffn_conv_w, ffn_conv_b, ffn_w_down, final_norm_w):
    p = {
        "norm_mix_w": norm_mix_w, "w_in": w_in, "ret_log_decay": ret_log_decay, "ret_norm_w": ret_norm_w,
        "mla_q_norm_w": mla_q_norm_w, "mla_w_uq": mla_w_uq, "mla_kv_norm_w": mla_kv_norm_w,
        "mla_w_ukv": mla_w_ukv, "diff_lambda": diff_lambda, "diff_norm_w": diff_norm_w,
        "ssm_conv_w": ssm_conv_w, "ssm_conv_b": ssm_conv_b, "ssm_dt_bias": ssm_dt_bias,
        "ssm_a_log": ssm_a_log, "ssm_d": ssm_d, "ssm_norm_w": ssm_norm_w, "w_branch": w_branch,
        "w_out": w_out, "norm_ffn_w": norm_ffn_w, "ffn_w_gate": ffn_w_gate, "ffn_w_up": ffn_w_up,
        "ffn_conv_w": ffn_conv_w, "ffn_conv_b": ffn_conv_b, "ffn_w_down": ffn_w_down,
        "final_norm_w": final_norm_w,
    }
    layers = [_layer_params(l, p) for l in range(DEPTH)]
    return _trunk(x_prompt, layers), _trunk(x_sample, layers)
```
